```python
import math
import jax
import jax.numpy as jnp
from jax import lax
import numpy as np

D_MODEL = 1024
BATCH = 16
SEQ = 4096
DEPTH = 4

N_MIXERS = 3
N_A = (DEPTH + N_MIXERS - 1) // N_MIXERS
N_B = (DEPTH + N_MIXERS - 2) // N_MIXERS
N_C = (DEPTH + N_MIXERS - 3) // N_MIXERS
MEM_LEN = 256
DEEPNORM_ALPHA = (2 * DEPTH) ** 0.25
DEEPNORM_BETA = (8 * DEPTH) ** -0.25
NORM_EPS = 1e-5
ROPE_THETA = 10000.0
MAX_POS_OFFSET = 1024
MASK_VALUE = -1e30
MIN_FORGET = 1e-6

A_EXPAND = 128
A_HEADS = D_MODEL // A_EXPAND
A_HEAD_V = D_MODEL // A_HEADS
A_CHUNK = 64

B_GROUPS = ((128, 1), (512, 4), (2048, 16))
B_HEAD_DIM = 64
B_HEADS = D_MODEL // B_HEAD_DIM
B_BLOCK = 128

C_GROUP_CH = 16
C_GROUPS = D_MODEL // C_GROUP_CH
C_STATE = 64
DT_MIN = 1e-3
DT_MAX = 1e-1

M_HEADS = 4
M_HEAD_DIM = D_MODEL // M_HEADS

N_EXPERTS = 32
TOP_K = 4
D_EXPERT = D_MODEL
SWIGLU_ALPHA = 1.702
SWIGLU_LIMIT = 7.0
MOE_BLOCK = 128

kernel_name = 'hybrid_hgrn2_dilated_s5_moe_trunk'


def layer_norm(x, g, b):
    xf = x.astype(jnp.float32)
    mu = jnp.mean(xf, axis=-1, keepdims=True)
    var = jnp.mean(jnp.square(xf - mu), axis=-1, keepdims=True)
    return ((xf - mu) * lax.rsqrt(var + NORM_EPS) * g + b).astype(x.dtype)


def rms_norm(x, g):
    xf = x.astype(jnp.float32)
    y = xf * lax.rsqrt(jnp.mean(jnp.square(xf), axis=-1, keepdims=True) + NORM_EPS)
    return (y * g).astype(x.dtype)


def rope(x, positions):
    half = x.shape[-1] // 2
    inv_freq = ROPE_THETA ** (-jnp.arange(half, dtype=jnp.float32) / half)
    ang = positions.astype(jnp.float32)[..., None] * inv_freq
    cos, sin = jnp.cos(ang)[:, :, None, :], jnp.sin(ang)[:, :, None, :]
    xf = x.astype(jnp.float32)
    x1, x2 = xf[..., :half], xf[..., half:]
    return jnp.concatenate([x1 * cos - x2 * sin, x1 * sin + x2 * cos], axis=-1).astype(x.dtype)


def hgrn2_mixer(x, w_in, norm_g, w_out, lower_bound):
    bsz, seq, dm = x.shape
    n_chunks = seq // A_CHUNK
    q, f, i, g = jnp.split(x @ w_in, 4, axis=-1)
    q = jax.nn.silu(q.astype(jnp.float32))
    lb = lower_bound.astype(jnp.float32)
    fg = lb + (1.0 - lb) * jax.nn.sigmoid(f.astype(jnp.float32))
    log_f = jnp.log(jnp.maximum(fg, MIN_FORGET))
    k = 1.0 - fg

    def to_chunks(t):
        return t.reshape(bsz, n_chunks, A_CHUNK, A_HEADS, -1).transpose(1, 0, 3, 2, 4)

    qc, kc, vc = to_chunks(q), to_chunks(k), to_chunks(i.astype(jnp.float32))
    bc = jnp.cumsum(to_chunks(log_f), axis=3)
    causal = jnp.tril(jnp.ones((A_CHUNK, A_CHUNK), dtype=bool))[:, :, None]

    def chunk_step(state, inp):
        q_c, k_c, v_c, b_c = inp
        rel = b_c[:, :, :, None, :] - b_c[:, :, None, :, :]
        decay = jnp.where(causal, jnp.exp(jnp.where(causal, rel, 0.0)), 0.0)
        scores = jnp.einsum('bhtd,bhsd,bhtsd->bhts', q_c, k_c, decay)
        o = (jnp.einsum('bhts,bhsv->bhtv', scores, v_c)
             + jnp.einsum('bhtd,bhdv->bhtv', q_c * jnp.exp(b_c), state))
        b_end = b_c[:, :, -1:, :]
        new_state = (jnp.exp(b_end[:, :, 0, :, None]) * state
                     + jnp.einsum('bhsd,bhsv->bhdv', k_c * jnp.exp(b_end - b_c), v_c))
        return new_state, o

    state0 = jnp.zeros((bsz, A_HEADS, A_EXPAND, A_HEAD_V), jnp.float32)
    _, o = lax.scan(chunk_step, state0, (qc, kc, vc, bc))
    o = o.transpose(1, 0, 3, 2, 4).reshape(bsz, seq, A_HEADS, A_HEAD_V)
    o = rms_norm(o, norm_g.reshape(A_HEADS, A_HEAD_V))
    o = o * jax.nn.silu(g.astype(jnp.float32)).reshape(bsz, seq, A_HEADS, A_HEAD_V)
    return o.reshape(bsz, seq, dm).astype(x.dtype) @ w_out


def dilated_window_attention(q, k, v, window, dilation):
    bsz, seq, nh, dh = q.shape
    sub_len = seq // dilation
    reach = window // dilation
    sub_pad = -(-sub_len // B_BLOCK) * B_BLOCK
    nb = sub_pad // B_BLOCK
    z = bsz * dilation

    def to_blocks(t):
        t = t.reshape(bsz, sub_len, dilation, nh, dh).transpose(0, 2, 1, 3, 4).reshape(z, sub_len, nh, dh)
        t = jnp.pad(t, ((0, 0), (0, sub_pad - sub_len), (0, 0), (0, 0)))
        return t.reshape(z, nb, B_BLOCK, nh, dh)

    def with_prev(t):
        prev = jnp.pad(t, ((0, 0), (1, 0), (0, 0), (0, 0), (0, 0)))[:, :-1]
        return jnp.concatenate([prev, t], axis=2)

    qb = to_blocks(q)
    kk, vv = with_prev(to_blocks(k)), with_prev(to_blocks(v))
    scores = jnp.einsum('znqhd,znkhd->znhqk', qb, kk).astype(jnp.float32) * (dh ** -0.5)
    q_pos = jnp.arange(nb)[:, None] * B_BLOCK + jnp.arange(B_BLOCK)[None, :]
    k_pos = jnp.arange(nb)[:, None] * B_BLOCK - B_BLOCK + jnp.arange(2 * B_BLOCK)[None, :]
    dist = q_pos[:, :, None] - k_pos[:, None, :]
    valid = (dist >= 0) & (dist <= reach) & (k_pos[:, None, :] >= 0)
    scores = jnp.where(valid[None, :, None], scores, MASK_VALUE)
    lse = jax.nn.logsumexp(scores, axis=-1)
    p = jnp.exp(scores - lse[..., None])
    out = jnp.einsum('znhqk,znkhd->znqhd', p, vv.astype(jnp.float32))

    def from_blocks(t):
        tail = t.shape[3:]
        t = t.reshape(z, sub_pad, *tail)[:, :sub_len]
        t = t.reshape(bsz, dilation, sub_len, *tail)
        return jnp.swapaxes(t, 1, 2).reshape(bsz, seq, *tail)

    return from_blocks(out), from_blocks(jnp.swapaxes(lse, 2, 3)[..., None])[..., 0]


def dilated_mixer(x, positions, w_in, w_out):
    bsz, seq, dm = x.shape
    proj = (x @ w_in).reshape(bsz, seq, len(B_GROUPS), 3, B_HEADS, B_HEAD_DIM)
    outs, lses = [], []
    for gi, (window, dilation) in enumerate(B_GROUPS):
        q = rope(proj[:, :, gi, 0], positions)
        k = rope(proj[:, :, gi, 1], positions)
        o, l = dilated_window_attention(q, k, proj[:, :, gi, 2], window, dilation)
        outs.append(o)
        lses.append(l)
    wts = jax.nn.softmax(jnp.stack(lses, axis=0), axis=0)[..., None]
    o = jnp.sum(wts * jnp.stack(outs, axis=0), axis=0)
    return o.reshape(bsz, seq, dm).astype(x.dtype) @ w_out


def s5_mixer(x, a_re, a_im, log_dt, b_re, b_im, c_re, c_im, d_skip, w_glu):
    bsz, seq, dm = x.shape
    f32 = jnp.float32
    u = x.astype(f32).reshape(bsz, seq, C_GROUPS, C_GROUP_CH)
    ar, ai = a_re.astype(f32), a_im.astype(f32)
    dt = jnp.exp(log_dt.astype(f32))[:, None]
    mag = jnp.exp(ar * dt)
    lam_re, lam_im = mag * jnp.cos(ai * dt), mag * jnp.sin(ai * dt)
    den = ar * ar + ai * ai
    fr = ((lam_re - 1.0) * ar + lam_im * ai) / den
    fi = (lam_im * ar - (lam_re - 1.0) * ai) / den
    br, bi = b_re.astype(f32), b_im.astype(f32)
    bb_re = fr[..., None] * br - fi[..., None] * bi
    bb_im = fr[..., None] * bi + fi[..., None] * br
    bu_re = jnp.einsum('bsgc,gnc->bsgn', u, bb_re)
    bu_im = jnp.einsum('bsgc,gnc->bsgn', u, bb_im)
    la_re = jnp.broadcast_to(lam_re[None, None], (1, seq, C_GROUPS, C_STATE))
    la_im = jnp.broadcast_to(lam_im[None, None], (1, seq, C_GROUPS, C_STATE))

    def combine(left, right):
        a1r, a1i, b1r, b1i = left
        a2r, a2i, b2r, b2i = right
        return (a1r * a2r - a1i * a2i, a1r * a2i + a1i * a2r,
                a2r * b1r - a2i * b1i + b2r, a2r * b1i + a2i * b1r + b2i)

    _, _, xr, xi = lax.associative_scan(combine, (la_re, la_im, bu_re, bu_im), axis=1)
    y = (jnp.einsum('bsgn,gcn->bsgc', xr, c_re.astype(f32))
         - jnp.einsum('bsgn,gcn->bsgc', xi, c_im.astype(f32)))
    y = y.reshape(bsz, seq, dm) + d_skip.astype(f32) * x.astype(f32)
    h = jax.nn.gelu(y).astype(x.dtype) @ w_glu
    val, gate = jnp.split(h, 2, axis=-1)
    return (val * jax.nn.sigmoid(gate)).astype(x.dtype)


def memory_cross_attention(x, mem_k, mem_v, w_q, w_o):
    bsz, seq, dm = x.shape
    q = (x @ w_q).reshape(bsz, seq, M_HEADS, M_HEAD_DIM)
    s = jnp.einsum('bshd,bmhd->bhsm', q, mem_k).astype(jnp.float32) * (M_HEAD_DIM ** -0.5)
    p = jax.nn.softmax(s, axis=-1)
    o = jnp.einsum('bhsm,bmhd->bshd', p, mem_v.astype(jnp.float32))
    return o.reshape(bsz, seq, dm).astype(x.dtype) @ w_o


def clamped_swiglu(h):
    h_glu, h_lin = jnp.split(h, 2, axis=-1)
    h_glu = jnp.minimum(h_glu, SWIGLU_LIMIT)
    h_lin = jnp.clip(h_lin, -SWIGLU_LIMIT, SWIGLU_LIMIT)
    return h_glu * jax.nn.sigmoid(SWIGLU_ALPHA * h_glu) * (h_lin + 1.0)


def moe_ffn(x, w_r, b_r, w1, b1, w2, b2):
    bsz, seq, dm = x.shape
    n_tok = bsz * seq
    n_assign = n_tok * TOP_K
    xt = x.reshape(n_tok, dm)
    logits = (xt @ w_r + b_r).astype(jnp.float32)
    top_val, top_idx = lax.top_k(logits, TOP_K)
    gates = jax.nn.softmax(top_val, axis=-1)
    flat_e = top_idx.reshape(-1)
    order = jnp.argsort(flat_e)
    sorted_e = flat_e[order]
    tok = order // TOP_K
    counts = jnp.bincount(flat_e, length=N_EXPERTS)
    padded = (counts + MOE_BLOCK - 1) // MOE_BLOCK * MOE_BLOCK
    pad_end = jnp.cumsum(padded)
    slot = (pad_end - padded)[sorted_e] + (jnp.arange(n_assign) - (jnp.cumsum(counts) - counts)[sorted_e])
    n_blocks = -(-n_assign // MOE_BLOCK) + N_EXPERTS
    xs = jnp.zeros((n_blocks * MOE_BLOCK, dm), x.dtype).at[slot].set(xt[tok])
    block_e = jnp.minimum(jnp.searchsorted(pad_end, jnp.arange(n_blocks) * MOE_BLOCK, side='right'),
                          N_EXPERTS - 1)

    def expert_block(args):
        xb, e = args
        return clamped_swiglu(xb @ w1[e] + b1[e]) @ w2[e] + b2[e]

    ys = lax.map(expert_block, (xs.reshape(n_blocks, MOE_BLOCK, dm), block_e)).reshape(-1, dm)[slot]
    wts = gates.reshape(-1)[order][:, None].astype(ys.dtype)
    y = jnp.zeros_like(xt).at[tok].add(ys * wts)
    return y.reshape(bsz, seq, dm)


def setup_inputs(seed: int = 0) -> dict:
    key = jax.random.key(seed)
    ks = jax.random.split(key, 32)
    f32 = jnp.float32

    def nrm(k, shape, scale):
        return jax.random.normal(k, shape, f32) * scale

    d = D_MODEL
    s_in = d ** -0.5
    s_out = s_in * DEEPNORM_BETA
    positions = (jax.random.randint(ks[2], (BATCH, 1), 0, MAX_POS_OFFSET, dtype=jnp.int32)
                 + jnp.arange(SEQ, dtype=jnp.int32)[None, :])
    return {
        'x': nrm(ks[0], (BATCH, SEQ, d), 1.0),
        'mem': nrm(ks[1], (BATCH, MEM_LEN, d), 1.0),
        'positions': positions,
        'ln_g': 1.0 + nrm(ks[3], (DEPTH, 3, d), 0.02),
        'ln_b': nrm(ks[4], (DEPTH, 3, d), 0.02),
        'a_w_in': nrm(ks[5], (N_A, d, 4 * d), s_in),
        'a_lower_bounds': nrm(ks[6], (DEPTH, d), 0.1),
        'a_norm_g': 1.0 + nrm(ks[7], (N_A, d), 0.02),
        'a_w_out': nrm(ks[8], (N_A, d, d), s_out),
        'b_w_in': nrm(ks[9], (N_B, d, 3 * len(B_GROUPS) * d), s_in),
        'b_w_out': nrm(ks[10], (N_B, d, d), s_out),
        'c_a_re': -0.5 + nrm(ks[11], (N_C, C_GROUPS, C_STATE), 0.01),
        'c_a_im': math.pi * jnp.arange(C_STATE, dtype=f32) + nrm(ks[12], (N_C, C_GROUPS, C_STATE), 0.01),
        'c_log_dt': jax.random.uniform(ks[13], (N_C, C_GROUPS), f32, math.log(DT_MIN), math.log(DT_MAX)),
        'c_b_re': nrm(ks[14], (N_C, C_GROUPS, C_STATE, C_GROUP_CH), (2 * C_GROUP_CH) ** -0.5),
        'c_b_im': nrm(ks[15], (N_C, C_GROUPS, C_STATE, C_GROUP_CH), (2 * C_GROUP_CH) ** -0.5),
        'c_c_re': nrm(ks[16], (N_C, C_GROUPS, C_GROUP_CH, C_STATE), C_STATE ** -0.5),
        'c_c_im': nrm(ks[17], (N_C, C_GROUPS, C_GROUP_CH, C_STATE), C_STATE ** -0.5),
        'c_d': nrm(ks[18], (N_C, d), 0.5),
        'c_w_glu': jnp.concatenate([nrm(ks[19], (N_C, d, d), s_out), nrm(ks[20], (N_C, d, d), s_in)], axis=-1),
        'm_w_kv': nrm(ks[21], (d, 2 * d), s_in),
        'm_w_q': nrm(ks[22], (DEPTH, d, d), s_in),
        'm_w_o': nrm(ks[23], (DEPTH, d, d), s_out),
        'r_w': nrm(ks[24], (DEPTH, d, N_EXPERTS), s_in),
        'r_b': nrm(ks[25], (DEPTH, N_EXPERTS), 0.01),
        'e_w1': nrm(ks[26], (DEPTH, N_EXPERTS, d, 2 * D_EXPERT), s_in),
        'e_b1': nrm(ks[27], (DEPTH, N_EXPERTS, 2 * D_EXPERT), 0.01),
        'e_w2': nrm(ks[28], (DEPTH, N_EXPERTS, D_EXPERT, d), D_EXPERT ** -0.5 * DEEPNORM_BETA),
        'e_b2': nrm(ks[29], (DEPTH, N_EXPERTS, d), 0.01),
    }


def reference(x, mem, positions, ln_g, ln_b, a_w_in, a_lower_bounds, a_norm_g, a_w_out,
              b_w_in, b_w_out, c_a_re, c_a_im, c_log_dt, c_b_re, c_b_im, c_c_re, c_c_im,
              c_d, c_w_glu, m_w_kv, m_w_q, m_w_o, r_w, r_b, e_w1, e_b1, e_w2, e_b2):
    bsz, seq, dm = x.shape
    lb = jax.nn.softmax(a_lower_bounds.astype(jnp.float32), axis=0)
    lb = jnp.cumsum(lb, axis=0) - lb[0]
    mem_kv = (mem @ m_w_kv).reshape(bsz, mem.shape[1], 2, M_HEADS, M_HEAD_DIM)
    mem_k, mem_v = mem_kv[:, :, 0], mem_kv[:, :, 1]
    h = x
    for layer in range(DEPTH):
        kind, j = layer % N_MIXERS, layer // N_MIXERS
        if kind == 0:
            mix = hgrn2_mixer(h, a_w_in[j], a_norm_g[j], a_w_out[j], lb[layer])
        elif kind == 1:
            mix = dilated_mixer(h, positions, b_w_in[j], b_w_out[j])
        else:
            mix = s5_mixer(h, c_a_re[j], c_a_im[j], c_log_dt[j], c_b_re[j], c_b_im[j],
                           c_c_re[j], c_c_im[j], c_d[j], c_w_glu[j])
        h = layer_norm(DEEPNORM_ALPHA * h + mix, ln_g[layer, 0], ln_b[layer, 0])
        h = layer_norm(DEEPNORM_ALPHA * h + memory_cross_attention(h, mem_k, mem_v, m_w_q[layer], m_w_o[layer]),
                       ln_g[layer, 1], ln_b[layer, 1])
        h = layer_norm(DEEPNORM_ALPHA * h + moe_ffn(h, r_w[layer], r_b[layer], e_w1[layer], e_b1[layer],
                                                    e_w2[layer], e_b2[layer]),
                       ln_g[layer, 2], ln_b[layer, 2])
    return h
```

```python
import functools
import math

import jax
import jax.numpy as jnp
from jax import lax
from jax.experimental import pallas as pl
from jax.experimental.pallas import tpu as pltpu

F32 = jnp.float32
BF16 = jnp.bfloat16
I32 = jnp.int32

D_MODEL = 1024
N_MIXERS = 3
NORM_EPS = 1e-5
ROPE_THETA = 10000.0
MIN_FORGET = 1e-6
MASK_VALUE = -1e30

A_HEADS = 8
A_DIM = 128
A_CHUNK = 64
A_SUB = 8

B_GROUPS = ((128, 1), (512, 4), (2048, 16))
B_HEAD_DIM = 64
B_HEADS = 16
B_BLOCK = 128

C_GROUP_CH = 16
C_GROUPS = 64
C_STATE = 64
C_CHUNK = 16

M_HEADS = 4
M_HEAD_DIM = 256

N_EXPERTS = 32
TOP_K = 4
SWIGLU_ALPHA = 1.702
SWIGLU_LIMIT = 7.0
MOE_ROWS = 512

LANES = 128
VMEM_LIMIT = 48 * 1024 * 1024


def _cparams(*sem):
    return pltpu.CompilerParams(dimension_semantics=sem, vmem_limit_bytes=VMEM_LIMIT)


def _layer_norm(z, g, b):
    mu = jnp.mean(z, axis=-1, keepdims=True)
    zc = z - mu
    var = jnp.mean(zc * zc, axis=-1, keepdims=True)
    return zc * lax.rsqrt(var + NORM_EPS) * g + b


def _sigmoid(x):
    return 1.0 / (1.0 + jnp.exp(-x))


def _dot(a, b):
    return jnp.dot(a, b, preferred_element_type=F32)


def _dot_nt(a, b):
    return lax.dot_general(a, b, (((1,), (1,)), ((), ())), preferred_element_type=F32)


def _dot_tn(a, b):
    return lax.dot_general(a, b, (((0,), (0,)), ((), ())), preferred_element_type=F32)


def _matmul_kernel(x_ref, w_ref, o_ref):
    o_ref[...] = _dot(x_ref[...].astype(BF16), w_ref[...]).astype(o_ref.dtype)


def _matmul(x, w, out_dtype, tm, tn):
    m, k = x.shape
    n = w.shape[1]
    return pl.pallas_call(
        _matmul_kernel,
        grid=(n // tn, m // tm),
        in_specs=[pl.BlockSpec((tm, k), lambda j, i: (i, 0)),
                  pl.BlockSpec((k, tn), lambda j, i: (0, j))],
        out_specs=pl.BlockSpec((tm, tn), lambda j, i: (i, j)),
        out_shape=jax.ShapeDtypeStruct((m, n), out_dtype),
        compiler_params=_cparams("parallel", "parallel"),
        name="matmul",
    )(x, w)


def _proj_res_ln_kernel(alpha, a_ref, w_ref, h_ref, g_ref, b_ref, o_ref):
    mix = _dot(a_ref[...].astype(BF16), w_ref[...])
    o_ref[...] = _layer_norm(alpha * h_ref[...] + mix, g_ref[...], b_ref[...])


def _proj_res_ln(a, w, h, g, b, alpha, tm=512):
    m, d = h.shape
    k = a.shape[1]
    row = lambda i: (i, 0)
    fix = lambda i: (0, 0)
    return pl.pallas_call(
        functools.partial(_proj_res_ln_kernel, alpha),
        grid=(m // tm,),
        in_specs=[pl.BlockSpec((tm, k), row), pl.BlockSpec((k, d), fix),
                  pl.BlockSpec((tm, d), row), pl.BlockSpec((1, d), fix), pl.BlockSpec((1, d), fix)],
        out_specs=pl.BlockSpec((tm, d), row),
        out_shape=jax.ShapeDtypeStruct((m, d), F32),
        compiler_params=_cparams("parallel"),
        name="proj_res_ln",
    )(a, w, h, g, b)


def _hgrn_in_kernel(x_ref, w_ref, lb_ref, q_ref, k_ref, v_ref, lf_ref, g_ref):
    d = x_ref.shape[1]
    xb = x_ref[...].astype(BF16)
    q = _dot(xb, w_ref[:, 0:d])
    q_ref[...] = q * _sigmoid(q)
    f = _dot(xb, w_ref[:, d:2 * d])
    lb = lb_ref[...]
    fg = lb + (1.0 - lb) * _sigmoid(f)
    lf_ref[...] = jnp.log(jnp.maximum(fg, MIN_FORGET))
    k_ref[...] = 1.0 - fg
    v_ref[...] = _dot(xb, w_ref[:, 2 * d:3 * d])
    g = _dot(xb, w_ref[:, 3 * d:4 * d])
    g_ref[...] = g * _sigmoid(g)


def _hgrn_in(x, w, lb, tm=256):
    m, d = x.shape
    row = lambda i: (i, 0)
    fix = lambda i: (0, 0)
    out = jax.ShapeDtypeStruct((m, d), F32)
    return pl.pallas_call(
        _hgrn_in_kernel,
        grid=(m // tm,),
        in_specs=[pl.BlockSpec((tm, d), row), pl.BlockSpec((d, 4 * d), fix), pl.BlockSpec((1, d), fix)],
        out_specs=[pl.BlockSpec((tm, d), row)] * 5,
        out_shape=[out] * 5,
        compiler_params=_cparams("parallel"),
        name="hgrn_in",
    )(x, w, lb)


def _rows_from(b, picks, span):
    parts = [jnp.broadcast_to(b[p:p + 1, :], (span, b.shape[1])) for p in picks]
    return parts[0] if len(parts) == 1 else jnp.concatenate(parts, axis=0)


def _hgrn_chunk_kernel(q_ref, k_ref, v_ref, lf_ref, g_ref, ng_ref, o_ref, state_ref):
    c = A_CHUNK
    n_sub = q_ref.shape[0] // c

    @pl.when(pl.program_id(1) == 0)
    def _():
        state_ref[...] = jnp.zeros_like(state_ref)

    ti = lax.broadcasted_iota(I32, (c, c), 0)
    si = lax.broadcasted_iota(I32, (c, c), 1)
    row = lax.broadcasted_iota(I32, (c, A_DIM), 0)
    halves = []
    half = c // 2
    while half >= A_SUB:
        halves.append(half)
        half //= 2
    level_masks = [((ti // (2 * hf)) == (si // (2 * hf))) & ((ti % (2 * hf)) >= hf) & ((si % (2 * hf)) < hf)
                   for hf in halves]
    diag_mask = ((ti // A_SUB) == (si // A_SUB)) & (si <= ti)

    def chunk(ci, carry):
        r0 = pl.multiple_of(ci * c, c)
        for hd in range(A_HEADS):
            cols = slice(hd * A_DIM, (hd + 1) * A_DIM)
            q = q_ref[pl.ds(r0, c), cols]
            k = k_ref[pl.ds(r0, c), cols]
            v = v_ref[pl.ds(r0, c), cols]
            b = lf_ref[pl.ds(r0, c), cols]
            sh = 1
            while sh < c:
                b = b + jnp.where(row >= sh, pltpu.roll(b, sh, axis=0), 0.0)
                sh *= 2
            scores = jnp.zeros((c, c), F32)
            for hf, mask in zip(halves, level_masks):
                ref = _rows_from(b, [blk * 2 * hf + hf - 1 for blk in range(c // (2 * hf))], 2 * hf)
                qt = (q * jnp.exp(jnp.minimum(b - ref, 0.0))).astype(BF16)
                kt = (k * jnp.exp(jnp.minimum(ref - b, 0.0))).astype(BF16)
                scores = scores + jnp.where(mask, _dot_nt(qt, kt), 0.0)
            lo = _rows_from(b, [blk * A_SUB for blk in range(c // A_SUB)], A_SUB)
            hi = _rows_from(b, [blk * A_SUB + A_SUB - 1 for blk in range(c // A_SUB)], A_SUB)
            mid = 0.5 * (lo + hi)
            qt = (q * jnp.exp(b - mid)).astype(BF16)
            kt = (k * jnp.exp(mid - b)).astype(BF16)
            scores = scores + jnp.where(diag_mask, _dot_nt(qt, kt), 0.0)

            st = state_ref[hd]
            b_end = b[c - 1:c, :]
            o = _dot(scores.astype(BF16), v.astype(BF16))
            o = o + _dot_nt((q * jnp.exp(b)).astype(BF16), st.astype(BF16))
            ke = (k * jnp.exp(b_end - b)).astype(BF16)
            state_ref[hd] = st * jnp.exp(b_end) + _dot_tn(v.astype(BF16), ke)

            o = o * lax.rsqrt(jnp.mean(o * o, axis=-1, keepdims=True) + NORM_EPS) * ng_ref[:, cols]
            o_ref[pl.ds(r0, c), cols] = (o * g_ref[pl.ds(r0, c), cols]).astype(o_ref.dtype)
        return carry

    lax.fori_loop(0, n_sub, chunk, 0)


def _hgrn_chunk(q, k, v, lf, g, ng, bsz, seq, tc=256):
    d = q.shape[1]
    n_t = seq // tc
    row = lambda b, i: (b * n_t + i, 0)
    return pl.pallas_call(
        _hgrn_chunk_kernel,
        grid=(bsz, n_t),
        in_specs=[pl.BlockSpec((tc, d), row)] * 5 + [pl.BlockSpec((1, d), lambda b, i: (0, 0))],
        out_specs=pl.BlockSpec((tc, d), row),
        out_shape=jax.ShapeDtypeStruct(q.shape, BF16),
        scratch_shapes=[pltpu.VMEM((A_HEADS, A_DIM, A_DIM), F32)],
        compiler_params=_cparams("parallel", "arbitrary"),
        name="hgrn_chunk",
    )(q, k, v, lf, g, ng)


def _hgrn2_mixer_ln(h, w_in, norm_g, w_out, lb, ln_g, ln_b, alpha, bsz, seq):
    q, k, v, lf, g = _hgrn_in(h, w_in.astype(BF16), lb.reshape(1, -1))
    o = _hgrn_chunk(q, k, v, lf, g, norm_g.reshape(1, -1), bsz, seq)
    return _proj_res_ln(o, w_out.astype(BF16), h, ln_g, ln_b, alpha)


def _rope_table_kernel(pos_ref, cos_ref, sin_ref):
    half = B_HEAD_DIM // 2
    lane = lax.broadcasted_iota(I32, cos_ref.shape, 1)
    j = (lane % half).astype(F32)
    inv_freq = jnp.exp(j * (-math.log(ROPE_THETA) / half))
    ang = pos_ref[...].astype(F32) * inv_freq
    cos_ref[...] = jnp.cos(ang)
    sin_ref[...] = jnp.where(lane < 2 * half, -1.0, 1.0) * jnp.sin(ang)


def _rope_tables(pos_col, tm=1024):
    t = pos_col.shape[0]
    out = jax.ShapeDtypeStruct((t, LANES), F32)
    return pl.pallas_call(
        _rope_table_kernel,
        grid=(t // tm,),
        in_specs=[pl.BlockSpec((tm, 1), lambda i: (i, 0))],
        out_specs=[pl.BlockSpec((tm, LANES), lambda i: (i, 0))] * 2,
        out_shape=[out, out],
        compiler_params=_cparams("parallel"),
        name="rope_tables",
    )(pos_col)


def _dil_attn_kernel(reach, q_ref, kp_ref, kc_ref, vp_ref, vc_ref, cq_ref, sq_ref, cp_ref, sp_ref,
                     o_ref, lse_ref):
    nq = q_ref.shape[0]
    i = pl.program_id(2)
    scale = B_HEAD_DIM ** -0.5
    half_lanes = LANES // 2

    def rope(x, cos, sin):
        return x * cos + pltpu.roll(x, half_lanes, axis=1) * sin

    qi = lax.broadcasted_iota(I32, (nq, 2 * nq), 0)
    ki = lax.broadcasted_iota(I32, (nq, 2 * nq), 1)
    dist = qi + nq - ki
    valid = (dist >= 0) & (dist <= reach) & ((ki >= nq) | (i > 0))
    lane = lax.broadcasted_iota(I32, (1, LANES), 1)
    head_a = ((lane // (B_HEAD_DIM // 2)) % 2) == 0
    v_head_a = lane < B_HEAD_DIM
    cq, sq = cq_ref[...], sq_ref[...]
    ck = jnp.concatenate([cp_ref[...], cq], axis=0)
    sk = jnp.concatenate([sp_ref[...], sq], axis=0)
    for pr in range(B_HEADS // 2):
        cols = slice(pr * LANES, (pr + 1) * LANES)
        q = rope(q_ref[:, cols], cq, sq) * scale
        k = rope(jnp.concatenate([kp_ref[:, cols], kc_ref[:, cols]], axis=0), ck, sk).astype(BF16)
        v = jnp.concatenate([vp_ref[:, cols], vc_ref[:, cols]], axis=0).astype(BF16)
        outs, lses = [], []
        for sel in (head_a, ~head_a):
            s = _dot_nt(jnp.where(sel, q, 0.0).astype(BF16), k)
            s = jnp.where(valid, s, MASK_VALUE)
            m = jnp.max(s, axis=-1, keepdims=True)
            p = jnp.exp(s - m)
            l = jnp.sum(p, axis=-1, keepdims=True)
            outs.append(_dot(p.astype(BF16), v) / l)
            lses.append(m + jnp.log(l))
        o_ref[:, cols] = jnp.where(v_head_a, outs[0], outs[1])
        lse_ref[:, cols] = jnp.where(v_head_a, lses[0], lses[1])


def _dil_attention(proj, cos, sin, gi, dilation, reach, bsz, seq):
    d = D_MODEL
    sub = seq // dilation
    nb = sub // B_BLOCK
    ncol = 3 * len(B_GROUPS)
    qcol = lambda b, r, i: (b, i, r * ncol + 3 * gi)
    blk = (None, B_BLOCK, d)
    prev = lambda i: jnp.maximum(i - 1, 0)
    tab = (None, B_BLOCK, LANES)
    in_specs = [
        pl.BlockSpec(blk, qcol),
        pl.BlockSpec(blk, lambda b, r, i: (b, prev(i), r * ncol + 3 * gi + 1)),
        pl.BlockSpec(blk, lambda b, r, i: (b, i, r * ncol + 3 * gi + 1)),
        pl.BlockSpec(blk, lambda b, r, i: (b, prev(i), r * ncol + 3 * gi + 2)),
        pl.BlockSpec(blk, lambda b, r, i: (b, i, r * ncol + 3 * gi + 2)),
        pl.BlockSpec(tab, lambda b, r, i: (b, i, r)),
        pl.BlockSpec(tab, lambda b, r, i: (b, i, r)),
        pl.BlockSpec(tab, lambda b, r, i: (b, prev(i), r)),
        pl.BlockSpec(tab, lambda b, r, i: (b, prev(i), r)),
    ]
    out_spec = pl.BlockSpec(blk, lambda b, r, i: (b, i, r))
    out = jax.ShapeDtypeStruct((bsz, sub, dilation * d), F32)
    cos_g = cos.reshape(bsz, sub, dilation * LANES)
    sin_g = sin.reshape(bsz, sub, dilation * LANES)
    return pl.pallas_call(
        functools.partial(_dil_attn_kernel, reach),
        grid=(bsz, dilation, nb),
        in_specs=in_specs,
        out_specs=[out_spec, out_spec],
        out_shape=[out, out],
        compiler_params=_cparams("parallel", "parallel", "arbitrary"),
        name=f"dil_attn_{gi}",
    )(proj, proj, proj, proj, proj, cos_g, sin_g, cos_g, sin_g)


def _dil_out_kernel(alpha, o0, o1, o2, l0, l1, l2, w_ref, h_ref, g_ref, b_ref, out_ref):
    l0v, l1v, l2v = l0[...], l1[...], l2[...]
    m = jnp.maximum(jnp.maximum(l0v, l1v), l2v)
    e0, e1, e2 = jnp.exp(l0v - m), jnp.exp(l1v - m), jnp.exp(l2v - m)
    o = (e0 * o0[...] + e1 * o1[...] + e2 * o2[...]) / (e0 + e1 + e2)
    mix = _dot(o.astype(BF16), w_ref[...])
    out_ref[...] = _layer_norm(alpha * h_ref[...] + mix, g_ref[...], b_ref[...])


def _dil_out(outs, lses, w, h, g, b, alpha, tm=256):
    m, d = h.shape
    row = lambda i: (i, 0)
    fix = lambda i: (0, 0)
    return pl.pallas_call(
        functools.partial(_dil_out_kernel, alpha),
        grid=(m // tm,),
        in_specs=[pl.BlockSpec((tm, d), row)] * 6 + [pl.BlockSpec((d, d), fix), pl.BlockSpec((tm, d), row),
                                                     pl.BlockSpec((1, d), fix), pl.BlockSpec((1, d), fix)],
        out_specs=pl.BlockSpec((tm, d), row),
        out_shape=jax.ShapeDtypeStruct((m, d), F32),
        compiler_params=_cparams("parallel"),
        name="dil_out",
    )(*outs, *lses, w, h, g, b)


def _rope_col_perm():
    half = B_HEAD_DIM // 2
    idx = []
    for pr in range(B_HEADS // 2):
        a, b = 2 * pr * B_HEAD_DIM, (2 * pr + 1) * B_HEAD_DIM
        idx += list(range(a, a + half)) + list(range(b, b + half))
        idx += list(range(a + half, a + 2 * half)) + list(range(b + half, b + 2 * half))
    return jnp.asarray(idx, dtype=I32)


def _dilated_mixer_ln(h, positions, w_in, w_out, ln_g, ln_b, alpha, bsz, seq):
    d = D_MODEL
    perm = _rope_col_perm()
    w = w_in.reshape(d, len(B_GROUPS), 3, d)
    w = jnp.concatenate([w[:, :, 0:2, :][..., perm], w[:, :, 2:3, :]], axis=2)
    w = w.reshape(d, 3 * len(B_GROUPS) * d).astype(BF16)
    proj = _matmul(h, w, F32, 512, 1024)
    cos, sin = _rope_tables(positions.reshape(-1, 1))
    outs, lses = [], []
    for gi, (window, dilation) in enumerate(B_GROUPS):
        pg = proj.reshape(bsz, seq // dilation, dilation * w.shape[1])
        o, l = _dil_attention(pg, cos, sin, gi, dilation, window // dilation, bsz, seq)
        outs.append(o.reshape(bsz * seq, d))
        lses.append(l.reshape(bsz * seq, d))
    return _dil_out(outs, lses, w_out.astype(BF16), h, ln_g, ln_b, alpha)


def _s5_operators(a_re, a_im, log_dt, b_re, b_im, c_re, c_im):
    hp = lax.Precision.HIGHEST
    L = C_CHUNK
    dt = jnp.exp(log_dt)[:, None]
    lam_re = jnp.exp(a_re * dt) * jnp.cos(a_im * dt)
    lam_im = jnp.exp(a_re * dt) * jnp.sin(a_im * dt)
    den = a_re * a_re + a_im * a_im
    fr = ((lam_re - 1.0) * a_re + lam_im * a_im) / den
    fi = (lam_im * a_re - (lam_re - 1.0) * a_im) / den
    bb_re = fr[..., None] * b_re - fi[..., None] * b_im
    bb_im = fr[..., None] * b_im + fi[..., None] * b_re
    tau = jnp.arange(L + 1, dtype=F32)[:, None, None]
    mag = jnp.exp(a_re * dt * tau)
    pw_re = mag * jnp.cos(a_im * dt * tau)
    pw_im = mag * jnp.sin(a_im * dt * tau)
    cl_re = c_re[None] * pw_re[:, :, None, :] - c_im[None] * pw_im[:, :, None, :]
    cl_im = c_re[None] * pw_im[:, :, None, :] + c_im[None] * pw_re[:, :, None, :]
    kern = (jnp.einsum('tgon,gni->tgoi', cl_re, bb_re, precision=hp)
            - jnp.einsum('tgon,gni->tgoi', cl_im, bb_im, precision=hp))
    t_idx = jnp.arange(L)
    lag = t_idx[None, :] - t_idx[:, None]
    kt = kern[jnp.clip(lag, 0, L)]
    kt = jnp.where((lag >= 0)[:, :, None, None, None], kt, 0.0)
    m_op = kt.transpose(2, 0, 4, 1, 3).reshape(C_GROUPS, L * C_GROUP_CH, L * C_GROUP_CH)
    rev = pw_re[L - 1 - t_idx], pw_im[L - 1 - t_idx]
    p_re = rev[0][..., None] * bb_re[None] - rev[1][..., None] * bb_im[None]
    p_im = rev[0][..., None] * bb_im[None] + rev[1][..., None] * bb_re[None]
    p_op = jnp.concatenate([p_re, p_im], axis=2).transpose(1, 0, 3, 2)
    p_op = p_op.reshape(C_GROUPS, L * C_GROUP_CH, 2 * C_STATE)
    q_re = cl_re[1:]
    q_im = -cl_im[1:]
    q_op = jnp.concatenate([q_re, q_im], axis=3).transpose(1, 3, 0, 2)
    q_op = q_op.reshape(C_GROUPS, 2 * C_STATE, L * C_GROUP_CH)
    lam_l = jnp.concatenate([pw_re[L], pw_im[L]], axis=-1)
    return m_op.astype(BF16), p_op.astype(BF16), q_op.astype(BF16), lam_l


def _s5_local_kernel(u_ref, m_ref, p_ref, y_ref, x_ref):
    u = u_ref[...]
    y_ref[...] = _dot(u, m_ref[...])
    x_ref[...] = _dot(u, p_ref[...])


def _s5_local(u, m_op, p_op, tr=1024):
    g, n, w = u.shape
    tr = min(tr, n)
    return pl.pallas_call(
        _s5_local_kernel,
        grid=(g, n // tr),
        in_specs=[pl.BlockSpec((None, tr, w), lambda gi, i: (gi, i, 0)),
                  pl.BlockSpec((None, w, w), lambda gi, i: (gi, 0, 0)),
                  pl.BlockSpec((None, w, 2 * C_STATE), lambda gi, i: (gi, 0, 0))],
        out_specs=[pl.BlockSpec((None, tr, w), lambda gi, i: (gi, i, 0)),
                   pl.BlockSpec((None, tr, 2 * C_STATE), lambda gi, i: (gi, i, 0))],
        out_shape=[jax.ShapeDtypeStruct((g, n, w), F32), jax.ShapeDtypeStruct((g, n, 2 * C_STATE), F32)],
        compiler_params=_cparams("parallel", "parallel"),
        name="s5_local",
    )(u, m_op, p_op)


def _s5_scan_kernel(xl_ref, la_ref, lb_ref, xs_ref):
    la, lb = la_ref[...], lb_ref[...]

    def step(c, x):
        xs_ref[c] = x
        return la * x + lb * pltpu.roll(x, C_STATE, axis=1) + xl_ref[c]

    lax.fori_loop(0, xl_ref.shape[0], step, jnp.zeros(xl_ref.shape[1:], F32))


def _s5_scan(xl, lam_l):
    g, nc, bsz, w = xl.shape
    lr, li = lam_l[:, :C_STATE], lam_l[:, C_STATE:]
    la = jnp.concatenate([lr, lr], axis=-1).reshape(g, 1, w)
    lb = jnp.concatenate([-li, li], axis=-1).reshape(g, 1, w)
    return pl.pallas_call(
        _s5_scan_kernel,
        grid=(g,),
        in_specs=[pl.BlockSpec((None, nc, bsz, w), lambda gi: (gi, 0, 0, 0)),
                  pl.BlockSpec((None, 1, w), lambda gi: (gi, 0, 0)),
                  pl.BlockSpec((None, 1, w), lambda gi: (gi, 0, 0))],
        out_specs=pl.BlockSpec((None, nc, bsz, w), lambda gi: (gi, 0, 0, 0)),
        out_shape=jax.ShapeDtypeStruct(xl.shape, F32),
        compiler_params=_cparams("parallel"),
        name="s5_scan",
    )(xl, la, lb)


def _s5_state_out_kernel(y_ref, xs_ref, q_ref, o_ref):
    o_ref[...] = y_ref[...] + _dot(xs_ref[...].astype(BF16), q_ref[...])


def _s5_state_out(y, xs, q_op, tr=1024):
    g, n, w = y.shape
    tr = min(tr, n)
    return pl.pallas_call(
        _s5_state_out_kernel,
        grid=(g, n // tr),
        in_specs=[pl.BlockSpec((None, tr, w), lambda gi, i: (gi, i, 0)),
                  pl.BlockSpec((None, tr, 2 * C_STATE), lambda gi, i: (gi, i, 0)),
                  pl.BlockSpec((None, 2 * C_STATE, w), lambda gi, i: (gi, 0, 0))],
        out_specs=pl.BlockSpec((None, tr, w), lambda gi, i: (gi, i, 0)),
        out_shape=jax.ShapeDtypeStruct((g, n, w), F32),
        compiler_params=_cparams("parallel", "parallel"),
        name="s5_state_out",
    )(y, xs, q_op)


def _s5_out_kernel(alpha, y_ref, h_ref, dskip_ref, w_ref, g_ref, b_ref, o_ref):
    d = h_ref.shape[1]
    h = h_ref[...]
    z = y_ref[...] + dskip_ref[...] * h
    z = 0.5 * z * (1.0 + jnp.tanh(math.sqrt(2.0 / math.pi) * (z + 0.044715 * (z * z * z))))
    zb = z.astype(BF16)
    val = _dot(zb, w_ref[:, 0:d])
    gate = _dot(zb, w_ref[:, d:2 * d])
    o_ref[...] = _layer_norm(alpha * h + val * _sigmoid(gate), g_ref[...], b_ref[...])


def _s5_out(y, h, d_skip, w_glu, g, b, alpha, tm=256):
    m, d = h.shape
    row = lambda i: (i, 0)
    fix = lambda i: (0, 0)
    return pl.pallas_call(
        functools.partial(_s5_out_kernel, alpha),
        grid=(m // tm,),
        in_specs=[pl.BlockSpec((tm, d), row), pl.BlockSpec((tm, d), row), pl.BlockSpec((1, d), fix),
                  pl.BlockSpec((d, 2 * d), fix), pl.BlockSpec((1, d), fix), pl.BlockSpec((1, d), fix)],
        out_specs=pl.BlockSpec((tm, d), row),
        out_shape=jax.ShapeDtypeStruct((m, d), F32),
        compiler_params=_cparams("parallel"),
        name="s5_out",
    )(y, h, d_skip, w_glu, g, b)


def _s5_mixer_ln(h, a_re, a_im, log_dt, b_re, b_im, c_re, c_im, d_skip, w_glu, ln_g, ln_b, alpha, bsz, seq):
    L = C_CHUNK
    nc = seq // L
    m_op, p_op, q_op, lam_l = _s5_operators(a_re, a_im, log_dt, b_re, b_im, c_re, c_im)
    u = h.astype(BF16).reshape(bsz, nc, L, C_GROUPS, C_GROUP_CH)
    u = u.transpose(3, 1, 0, 2, 4).reshape(C_GROUPS, nc * bsz, L * C_GROUP_CH)
    y_loc, x_loc = _s5_local(u, m_op, p_op)
    xs = _s5_scan(x_loc.reshape(C_GROUPS, nc, bsz, 2 * C_STATE), lam_l)
    y = _s5_state_out(y_loc, xs.reshape(C_GROUPS, nc * bsz, 2 * C_STATE), q_op)
    y = y.reshape(C_GROUPS, nc, bsz, L, C_GROUP_CH).transpose(2, 1, 3, 0, 4).reshape(bsz * seq, D_MODEL)
    return _s5_out(y, h, d_skip.reshape(1, -1), w_glu.astype(BF16), ln_g, ln_b, alpha)


def _cross_kernel(alpha, h_ref, kv_ref, wq_ref, wo_ref, g_ref, b_ref, wr_ref, br_ref,
                  o_ref, idx_ref, gate_ref):
    d = h_ref.shape[1]
    h = h_ref[...]
    q = (_dot(h.astype(BF16), wq_ref[...]) * (M_HEAD_DIM ** -0.5)).astype(BF16)
    heads = []
    for hd in range(M_HEADS):
        cols = slice(hd * M_HEAD_DIM, (hd + 1) * M_HEAD_DIM)
        s = _dot_nt(q[:, cols], kv_ref[:, cols])
        p = jnp.exp(s - jnp.max(s, axis=-1, keepdims=True))
        l = jnp.sum(p, axis=-1, keepdims=True)
        heads.append((_dot(p.astype(BF16), kv_ref[:, d + hd * M_HEAD_DIM:d + (hd + 1) * M_HEAD_DIM]) / l)
                     .astype(BF16))
    att = _dot(jnp.concatenate(heads, axis=-1), wo_ref[...])
    h2 = _layer_norm(alpha * h + att, g_ref[...], b_ref[...])
    o_ref[...] = h2
    logits = (_dot(h2.astype(BF16), wr_ref[...]) + br_ref[...]).T[0:N_EXPERTS, :]
    eidx = lax.broadcasted_iota(I32, logits.shape, 0)
    vals, idxs = [], []
    for _ in range(TOP_K):
        m = jnp.max(logits, axis=0, keepdims=True)
        pick = jnp.min(jnp.where(logits == m, eidx, N_EXPERTS), axis=0, keepdims=True)
        vals.append(m)
        idxs.append(pick)
        logits = jnp.where(eidx == pick, -jnp.inf, logits)
    es = [jnp.exp(v - vals[0]) for v in vals]
    tot = es[0] + es[1] + es[2] + es[3]
    idx_ref[...] = jnp.concatenate(idxs, axis=0)
    gate_ref[...] = jnp.concatenate([e / tot for e in es], axis=0)


def _cross_attention_router(h, mem_kv, wq, wo, g, b, wr, br, alpha, bsz, seq, tm=512):
    d = D_MODEL
    n_t = seq // tm
    mlen = mem_kv.shape[1]
    row = lambda bi, i: (bi * n_t + i, 0)
    fix = lambda bi, i: (0, 0)
    col = lambda bi, i: (bi, 0, i)
    return pl.pallas_call(
        functools.partial(_cross_kernel, alpha),
        grid=(bsz, n_t),
        in_specs=[pl.BlockSpec((tm, d), row),
                  pl.BlockSpec((None, mlen, 2 * d), lambda bi, i: (bi, 0, 0)),
                  pl.BlockSpec((d, d), fix), pl.BlockSpec((d, d), fix),
                  pl.BlockSpec((1, d), fix), pl.BlockSpec((1, d), fix),
                  pl.BlockSpec((d, LANES), fix), pl.BlockSpec((1, LANES), fix)],
        out_specs=[pl.BlockSpec((tm, d), row),
                   pl.BlockSpec((None, TOP_K, tm), col), pl.BlockSpec((None, TOP_K, tm), col)],
        out_shape=[jax.ShapeDtypeStruct((bsz * seq, d), F32),
                   jax.ShapeDtypeStruct((bsz, TOP_K, seq), I32),
                   jax.ShapeDtypeStruct((bsz, TOP_K, seq), F32)],
        compiler_params=_cparams("parallel", "parallel"),
        name="cross_attn_router",
    )(h, mem_kv, wq, wo, g, b, wr, br)


def _rank_kernel(idx_ref, rank_ref, count_ref, run_ref):
    first = (pl.program_id(0) == 0) & (pl.program_id(1) == 0)

    @pl.when(first)
    def _():
        run_ref[...] = jnp.zeros_like(run_ref)

    tn = idx_ref.shape[1]
    idx = idx_ref[...]
    eidx = lax.broadcasted_iota(I32, (N_EXPERTS, tn), 0)
    hits = [eidx == idx[k:k + 1, :] for k in range(TOP_K)]
    onehot = jnp.zeros((N_EXPERTS, tn), F32)
    for hit in hits:
        onehot = onehot + jnp.where(hit, 1.0, 0.0)
    earlier = lax.broadcasted_iota(I32, (tn, tn), 0) < lax.broadcasted_iota(I32, (tn, tn), 1)
    before = _dot(onehot.astype(BF16), jnp.where(earlier, 1.0, 0.0).astype(BF16)) + run_ref[:, 0:1]
    ranks = [jnp.sum(jnp.where(hit, before, 0.0), axis=0, keepdims=True) for hit in hits]
    rank_ref[...] = jnp.concatenate(ranks, axis=0).astype(I32)
    run = run_ref[...] + jnp.sum(onehot, axis=1, keepdims=True)
    run_ref[...] = run
    count_ref[...] = run.astype(I32)


def _expert_ranks(idx, tn=512):
    bsz, _, seq = idx.shape
    col = lambda bi, i: (bi, 0, i)
    return pl.pallas_call(
        _rank_kernel,
        grid=(bsz, seq // tn),
        in_specs=[pl.BlockSpec((None, TOP_K, tn), col)],
        out_specs=[pl.BlockSpec((None, TOP_K, tn), col),
                   pl.BlockSpec((N_EXPERTS, LANES), lambda bi, i: (0, 0))],
        out_shape=[jax.ShapeDtypeStruct(idx.shape, I32), jax.ShapeDtypeStruct((N_EXPERTS, LANES), I32)],
        scratch_shapes=[pltpu.VMEM((N_EXPERTS, LANES), F32)],
        compiler_params=_cparams("arbitrary", "arbitrary"),
        name="expert_ranks",
    )(idx)


def _dispatch_kernel(slot_ref, x_ref, xs_in_ref, xs_ref, sem):
    del xs_in_ref
    tm = x_ref.shape[0]

    def row_copy(j):
        return pltpu.make_async_copy(x_ref.at[pl.ds(j % tm, 1)], xs_ref.at[pl.ds(slot_ref[0, j], 1)], sem)

    def start(j, c):
        row_copy(j).start()
        return c

    def wait(j, c):
        row_copy(j).wait()
        return c

    lax.fori_loop(0, TOP_K * tm, start, 0)
    lax.fori_loop(0, TOP_K * tm, wait, 0)


def _dispatch(x, slots, n_rows, tm=256):
    t, d = x.shape
    zeros = jnp.zeros((n_rows, d), x.dtype)
    return pl.pallas_call(
        _dispatch_kernel,
        grid=(t // tm,),
        in_specs=[pl.BlockSpec((None, 1, TOP_K * tm), lambda i: (i, 0, 0), memory_space=pltpu.SMEM),
                  pl.BlockSpec((tm, d), lambda i: (i, 0)),
                  pl.BlockSpec(memory_space=pl.ANY)],
        out_specs=pl.BlockSpec(memory_space=pl.ANY),
        out_shape=jax.ShapeDtypeStruct(zeros.shape, zeros.dtype),
        scratch_shapes=[pltpu.SemaphoreType.DMA(())],
        input_output_aliases={2: 0},
        compiler_params=_cparams("arbitrary"),
        name="moe_dispatch",
    )(slots, x, zeros)


def _expert_ffn_kernel(be_ref, nu_ref, x_ref, w1_ref, b1_ref, w2_ref, b2_ref, y_ref):
    d = x_ref.shape[1]

    @pl.when(pl.program_id(0) < nu_ref[0])
    def _():
        xb = x_ref[...].astype(BF16)
        glu = jnp.minimum(_dot(xb, w1_ref[:, 0:d]) + b1_ref[:, 0:d], SWIGLU_LIMIT)
        lin = jnp.clip(_dot(xb, w1_ref[:, d:2 * d]) + b1_ref[:, d:2 * d], -SWIGLU_LIMIT, SWIGLU_LIMIT)
        act = glu * _sigmoid(SWIGLU_ALPHA * glu) * (lin + 1.0)
        y_ref[...] = _dot(act.astype(BF16), w2_ref[...]) + b2_ref[...]

    @pl.when(pl.program_id(0) >= nu_ref[0])
    def _():
        y_ref[...] = jnp.zeros_like(y_ref)


def _expert_ffn(xs, block_e, n_used, w1, b1, w2, b2):
    n_rows, d = xs.shape
    nb = n_rows // MOE_ROWS
    wsel = lambda i, be, nu: (be[i], 0, 0)
    grid_spec = pltpu.PrefetchScalarGridSpec(
        num_scalar_prefetch=2,
        grid=(nb,),
        in_specs=[pl.BlockSpec((MOE_ROWS, d), lambda i, be, nu: (i, 0)),
                  pl.BlockSpec((None, d, 2 * d), wsel), pl.BlockSpec((None, 1, 2 * d), wsel),
                  pl.BlockSpec((None, d, d), wsel), pl.BlockSpec((None, 1, d), wsel)],
        out_specs=pl.BlockSpec((MOE_ROWS, d), lambda i, be, nu: (i, 0)),
    )
    return pl.pallas_call(
        _expert_ffn_kernel,
        grid_spec=grid_spec,
        out_shape=jax.ShapeDtypeStruct((n_rows, d), F32),
        compiler_params=_cparams("arbitrary"),
        name="expert_ffn",
    )(block_e, n_used, xs, w1, b1, w2, b2)


def _combine_kernel(alpha, slot_ref, gate_ref, h_ref, g_ref, b_ref, ys_ref, o_ref, buf, sem):
    tm = h_ref.shape[0]

    def row_copy(j):
        return pltpu.make_async_copy(ys_ref.at[pl.ds(slot_ref[0, j], 1)],
                                     buf.at[j // tm, pl.ds(j % tm, 1)], sem)

    def start(j, c):
        row_copy(j).start()
        return c

    def wait(j, c):
        row_copy(j).wait()
        return c

    lax.fori_loop(0, TOP_K * tm, start, 0)
    lax.fori_loop(0, TOP_K * tm, wait, 0)
    gate = gate_ref[...]
    y = buf[0] * gate[:, 0:1]
    for k in range(1, TOP_K):
        y = y + buf[k] * gate[:, k:k + 1]
    o_ref[...] = _layer_norm(alpha * h_ref[...] + y, g_ref[...], b_ref[...])


def _combine_ln(ys, slots, gates, h, g, b, alpha, tm=128):
    t, d = h.shape
    row = lambda i: (i, 0)
    fix = lambda i: (0, 0)
    return pl.pallas_call(
        functools.partial(_combine_kernel, alpha),
        grid=(t // tm,),
        in_specs=[pl.BlockSpec((None, 1, TOP_K * tm), lambda i: (i, 0, 0), memory_space=pltpu.SMEM),
                  pl.BlockSpec((tm, TOP_K), row), pl.BlockSpec((tm, d), row),
                  pl.BlockSpec((1, d), fix), pl.BlockSpec((1, d), fix),
                  pl.BlockSpec(memory_space=pl.ANY)],
        out_specs=pl.BlockSpec((tm, d), row),
        out_shape=jax.ShapeDtypeStruct((t, d), F32),
        scratch_shapes=[pltpu.VMEM((TOP_K, tm, d), F32), pltpu.SemaphoreType.DMA(())],
        compiler_params=_cparams("arbitrary"),
        name="moe_combine_ln",
    )(slots, gates, h, g, b, ys)


def _tile_slots(slot, tm):
    bsz, _, seq = slot.shape
    s = slot.reshape(bsz, TOP_K, seq // tm, tm).transpose(0, 2, 1, 3)
    return s.reshape(bsz * (seq // tm), 1, TOP_K * tm)


def _moe_ln(h2, idx, gates, w1, b1, w2, b2, ln_g, ln_b, alpha, tm_d=256, tm_c=128):
    t, d = h2.shape
    rank, counts = _expert_ranks(idx)
    counts = counts[:, 0]
    padded = (counts + MOE_ROWS - 1) // MOE_ROWS * MOE_ROWS
    pad_end = jnp.cumsum(padded)
    base = pad_end - padded
    slot = base[idx] + rank
    nb = t * TOP_K // MOE_ROWS + N_EXPERTS
    block_e = jnp.minimum(jnp.searchsorted(pad_end, jnp.arange(nb, dtype=I32) * MOE_ROWS, side='right'),
                          N_EXPERTS - 1).astype(I32)
    n_used = (pad_end[-1:] // MOE_ROWS).astype(I32)
    xs = _dispatch(h2, _tile_slots(slot, tm_d), nb * MOE_ROWS, tm_d)
    ys = _expert_ffn(xs, block_e, n_used, w1.astype(BF16), b1[:, None, :], w2.astype(BF16), b2[:, None, :])
    gates_t = gates.transpose(0, 2, 1).reshape(t, TOP_K)
    return _combine_ln(ys, _tile_slots(slot, tm_c), gates_t, h2, ln_g[None], ln_b[None], alpha, tm_c)


def kernel(x, mem, positions, ln_g, ln_b, a_w_in, a_lower_bounds, a_norm_g, a_w_out, b_w_in, b_w_out,
           c_a_re, c_a_im, c_log_dt, c_b_re, c_b_im, c_c_re, c_c_im, c_d, c_w_glu, m_w_kv, m_w_q, m_w_o,
           r_w, r_b, e_w1, e_b1, e_w2, e_b2):
    bsz, seq, d = x.shape
    depth = ln_g.shape[0]
    alpha = (2 * depth) ** 0.25
    t = bsz * seq
    lb = jax.nn.softmax(a_lower_bounds.astype(F32), axis=0)
    lb = jnp.cumsum(lb, axis=0) - lb[0]
    mem_kv = _matmul(mem.reshape(-1, d), m_w_kv.astype(BF16), BF16, 512, 1024)
    mem_kv = mem_kv.reshape(bsz, mem.shape[1], 2 * d)
    h = x.reshape(t, d)
    for layer in range(depth):
        kind, j = layer % N_MIXERS, layer // N_MIXERS
        g1, b1 = ln_g[layer, 0][None], ln_b[layer, 0][None]
        if kind == 0:
            h = _hgrn2_mixer_ln(h, a_w_in[j], a_norm_g[j], a_w_out[j], lb[layer], g1, b1, alpha, bsz, seq)
        elif kind == 1:
            h = _dilated_mixer_ln(h, positions, b_w_in[j], b_w_out[j], g1, b1, alpha, bsz, seq)
        else:
            h = _s5_mixer_ln(h, c_a_re[j], c_a_im[j], c_log_dt[j], c_b_re[j], c_b_im[j], c_c_re[j],
                             c_c_im[j], c_d[j], c_w_glu[j], g1, b1, alpha, bsz, seq)
        wr = jnp.pad(r_w[layer], ((0, 0), (0, LANES - N_EXPERTS))).astype(BF16)
        br = jnp.pad(r_b[layer], (0, LANES - N_EXPERTS))[None]
        h, idx, gates = _cross_attention_router(h, mem_kv, m_w_q[layer].astype(BF16), m_w_o[layer].astype(BF16),
                                                ln_g[layer, 1][None], ln_b[layer, 1][None], wr, br,
                                                alpha, bsz, seq)
        h = _moe_ln(h, idx, gates, e_w1[layer], e_b1[layer], e_w2[layer], e_b2[layer],
                    ln_g[layer, 2], ln_b[layer, 2], alpha)
    return h.reshape(bsz, seq, d)
```

```python
import functools
import math

import jax
import jax.numpy as jnp
from jax import lax
from jax.experimental import pallas as pl
from jax.experimental.pallas import tpu as pltpu

F32 = jnp.float32
BF16 = jnp.bfloat16
I32 = jnp.int32

D_MODEL = 1024
N_MIXERS = 3
NORM_EPS = 1e-5
ROPE_THETA = 10000.0
MIN_FORGET = 1e-6
MASK_VALUE = -1e30

A_HEADS = 8
A_DIM = 128
A_CHUNK = 64
A_SUB = 8

B_GROUPS = ((128, 1), (512, 4), (2048, 16))
B_HEAD_DIM = 64
B_HEADS = 16
B_BLOCK = 128

C_GROUP_CH = 16
C_GROUPS = 64
C_STATE = 64
C_CHUNK = 16

M_HEADS = 4
M_HEAD_DIM = 256

N_EXPERTS = 32
TOP_K = 4
SWIGLU_ALPHA = 1.702
SWIGLU_LIMIT = 7.0
MOE_ROWS = 512
DMA_UNROLL = 8

LANES = 128
VMEM_LIMIT = 48 * 1024 * 1024


def _cparams(*sem):
    return pltpu.CompilerParams(dimension_semantics=sem, vmem_limit_bytes=VMEM_LIMIT)


def _layer_norm(z, g, b):
    mu = jnp.mean(z, axis=-1, keepdims=True)
    zc = z - mu
    var = jnp.mean(zc * zc, axis=-1, keepdims=True)
    return zc * lax.rsqrt(var + NORM_EPS) * g + b


def _sigmoid(x):
    return 1.0 / (1.0 + jnp.exp(-x))


def _dot(a, b):
    return jnp.dot(a, b, preferred_element_type=F32)


def _dot_nt(a, b):
    return lax.dot_general(a, b, (((1,), (1,)), ((), ())), preferred_element_type=F32)


def _dot_tn(a, b):
    return lax.dot_general(a, b, (((0,), (0,)), ((), ())), preferred_element_type=F32)


def _matmul_kernel(x_ref, w_ref, o_ref):
    o_ref[...] = _dot(x_ref[...].astype(BF16), w_ref[...]).astype(o_ref.dtype)


def _matmul(x, w, out_dtype, tm, tn):
    m, k = x.shape
    n = w.shape[1]
    return pl.pallas_call(
        _matmul_kernel,
        grid=(n // tn, m // tm),
        in_specs=[pl.BlockSpec((tm, k), lambda j, i: (i, 0)),
                  pl.BlockSpec((k, tn), lambda j, i: (0, j))],
        out_specs=pl.BlockSpec((tm, tn), lambda j, i: (i, j)),
        out_shape=jax.ShapeDtypeStruct((m, n), out_dtype),
        compiler_params=_cparams("parallel", "parallel"),
        name="matmul",
    )(x, w)


def _proj_res_ln_kernel(alpha, a_ref, w_ref, h_ref, g_ref, b_ref, o_ref):
    mix = _dot(a_ref[...].astype(BF16), w_ref[...])
    o_ref[...] = _layer_norm(alpha * h_ref[...] + mix, g_ref[...], b_ref[...])


def _proj_res_ln(a, w, h, g, b, alpha, tm=512):
    m, d = h.shape
    k = a.shape[1]
    row = lambda i: (i, 0)
    fix = lambda i: (0, 0)
    return pl.pallas_call(
        functools.partial(_proj_res_ln_kernel, alpha),
        grid=(m // tm,),
        in_specs=[pl.BlockSpec((tm, k), row), pl.BlockSpec((k, d), fix),
                  pl.BlockSpec((tm, d), row), pl.BlockSpec((1, d), fix), pl.BlockSpec((1, d), fix)],
        out_specs=pl.BlockSpec((tm, d), row),
        out_shape=jax.ShapeDtypeStruct((m, d), F32),
        compiler_params=_cparams("parallel"),
        name="proj_res_ln",
    )(a, w, h, g, b)


def _hgrn_in_kernel(x_ref, w_ref, lb_ref, q_ref, k_ref, v_ref, lf_ref, g_ref):
    d = x_ref.shape[1]
    xb = x_ref[...].astype(BF16)
    q = _dot(xb, w_ref[:, 0:d])
    q_ref[...] = q * _sigmoid(q)
    f = _dot(xb, w_ref[:, d:2 * d])
    lb = lb_ref[...]
    fg = lb + (1.0 - lb) * _sigmoid(f)
    lf_ref[...] = jnp.log(jnp.maximum(fg, MIN_FORGET))
    k_ref[...] = 1.0 - fg
    v_ref[...] = _dot(xb, w_ref[:, 2 * d:3 * d])
    g = _dot(xb, w_ref[:, 3 * d:4 * d])
    g_ref[...] = g * _sigmoid(g)


def _hgrn_in(x, w, lb, tm=256):
    m, d = x.shape
    row = lambda i: (i, 0)
    fix = lambda i: (0, 0)
    out = jax.ShapeDtypeStruct((m, d), F32)
    return pl.pallas_call(
        _hgrn_in_kernel,
        grid=(m // tm,),
        in_specs=[pl.BlockSpec((tm, d), row), pl.BlockSpec((d, 4 * d), fix), pl.BlockSpec((1, d), fix)],
        out_specs=[pl.BlockSpec((tm, d), row)] * 5,
        out_shape=[out] * 5,
        compiler_params=_cparams("parallel"),
        name="hgrn_in",
    )(x, w, lb)


def _rows_from(b, picks, span):
    parts = [jnp.broadcast_to(b[p:p + 1, :], (span, b.shape[1])) for p in picks]
    return parts[0] if len(parts) == 1 else jnp.concatenate(parts, axis=0)


def _hgrn_chunk_kernel(q_ref, k_ref, v_ref, lf_ref, g_ref, ng_ref, o_ref, state_ref):
    c = A_CHUNK
    n_sub = q_ref.shape[0] // c

    @pl.when(pl.program_id(1) == 0)
    def _():
        state_ref[...] = jnp.zeros_like(state_ref)

    ti = lax.broadcasted_iota(I32, (c, c), 0)
    si = lax.broadcasted_iota(I32, (c, c), 1)
    row = lax.broadcasted_iota(I32, (c, A_DIM), 0)
    halves = []
    half = c // 2
    while half >= A_SUB:
        halves.append(half)
        half //= 2
    level_masks = [((ti // (2 * hf)) == (si // (2 * hf))) & ((ti % (2 * hf)) >= hf) & ((si % (2 * hf)) < hf)
                   for hf in halves]
    diag_mask = ((ti // A_SUB) == (si // A_SUB)) & (si <= ti)

    def chunk(ci, carry):
        r0 = pl.multiple_of(ci * c, c)
        for hd in range(A_HEADS):
            cols = slice(hd * A_DIM, (hd + 1) * A_DIM)
            q = q_ref[pl.ds(r0, c), cols]
            k = k_ref[pl.ds(r0, c), cols]
            v = v_ref[pl.ds(r0, c), cols]
            b = lf_ref[pl.ds(r0, c), cols]
            sh = 1
            while sh < c:
                b = b + jnp.where(row >= sh, pltpu.roll(b, sh, axis=0), 0.0)
                sh *= 2
            scores = jnp.zeros((c, c), F32)
            for hf, mask in zip(halves, level_masks):
                ref = _rows_from(b, [blk * 2 * hf + hf - 1 for blk in range(c // (2 * hf))], 2 * hf)
                qt = (q * jnp.exp(jnp.minimum(b - ref, 0.0))).astype(BF16)
                kt = (k * jnp.exp(jnp.minimum(ref - b, 0.0))).astype(BF16)
                scores = scores + jnp.where(mask, _dot_nt(qt, kt), 0.0)
            lo = _rows_from(b, [blk * A_SUB for blk in range(c // A_SUB)], A_SUB)
            hi = _rows_from(b, [blk * A_SUB + A_SUB - 1 for blk in range(c // A_SUB)], A_SUB)
            mid = 0.5 * (lo + hi)
            qt = (q * jnp.exp(b - mid)).astype(BF16)
            kt = (k * jnp.exp(mid - b)).astype(BF16)
            scores = scores + jnp.where(diag_mask, _dot_nt(qt, kt), 0.0)

            st = state_ref[hd]
            b_end = b[c - 1:c, :]
            o = _dot(scores.astype(BF16), v.astype(BF16))
            o = o + _dot_nt((q * jnp.exp(b)).astype(BF16), st.astype(BF16))
            ke = (k * jnp.exp(b_end - b)).astype(BF16)
            state_ref[hd] = st * jnp.exp(b_end) + _dot_tn(v.astype(BF16), ke)

            o = o * lax.rsqrt(jnp.mean(o * o, axis=-1, keepdims=True) + NORM_EPS) * ng_ref[:, cols]
            o_ref[pl.ds(r0, c), cols] = (o * g_ref[pl.ds(r0, c), cols]).astype(o_ref.dtype)
        return carry

    lax.fori_loop(0, n_sub, chunk, 0)


def _hgrn_chunk(q, k, v, lf, g, ng, bsz, seq, tc=256):
    d = q.shape[1]
    n_t = seq // tc
    row = lambda b, i: (b * n_t + i, 0)
    return pl.pallas_call(
        _hgrn_chunk_kernel,
        grid=(bsz, n_t),
        in_specs=[pl.BlockSpec((tc, d), row)] * 5 + [pl.BlockSpec((1, d), lambda b, i: (0, 0))],
        out_specs=pl.BlockSpec((tc, d), row),
        out_shape=jax.ShapeDtypeStruct(q.shape, BF16),
        scratch_shapes=[pltpu.VMEM((A_HEADS, A_DIM, A_DIM), F32)],
        compiler_params=_cparams("parallel", "arbitrary"),
        name="hgrn_chunk",
    )(q, k, v, lf, g, ng)


def _hgrn2_mixer_ln(h, w_in, norm_g, w_out, lb, ln_g, ln_b, alpha, bsz, seq):
    q, k, v, lf, g = _hgrn_in(h, w_in.astype(BF16), lb.reshape(1, -1))
    o = _hgrn_chunk(q, k, v, lf, g, norm_g.reshape(1, -1), bsz, seq)
    return _proj_res_ln(o, w_out.astype(BF16), h, ln_g, ln_b, alpha)


def _rope_table_kernel(pos_ref, cos_ref, sin_ref):
    half = B_HEAD_DIM // 2
    lane = lax.broadcasted_iota(I32, cos_ref.shape, 1)
    j = (lane % half).astype(F32)
    inv_freq = jnp.exp(j * (-math.log(ROPE_THETA) / half))
    ang = pos_ref[...].astype(F32) * inv_freq
    cos_ref[...] = jnp.cos(ang)
    sin_ref[...] = jnp.where(lane < 2 * half, -1.0, 1.0) * jnp.sin(ang)


def _rope_tables(pos_col, tm=1024):
    t = pos_col.shape[0]
    out = jax.ShapeDtypeStruct((t, LANES), F32)
    return pl.pallas_call(
        _rope_table_kernel,
        grid=(t // tm,),
        in_specs=[pl.BlockSpec((tm, 1), lambda i: (i, 0))],
        out_specs=[pl.BlockSpec((tm, LANES), lambda i: (i, 0))] * 2,
        out_shape=[out, out],
        compiler_params=_cparams("parallel"),
        name="rope_tables",
    )(pos_col)


def _dil_attn_kernel(reach, q_ref, kp_ref, kc_ref, vp_ref, vc_ref, cq_ref, sq_ref, cp_ref, sp_ref,
                     o_ref, lse_ref):
    nq = q_ref.shape[0]
    i = pl.program_id(2)
    scale = B_HEAD_DIM ** -0.5
    half_lanes = LANES // 2

    def rope(x, cos, sin):
        return x * cos + pltpu.roll(x, half_lanes, axis=1) * sin

    qi = lax.broadcasted_iota(I32, (nq, 2 * nq), 0)
    ki = lax.broadcasted_iota(I32, (nq, 2 * nq), 1)
    dist = qi + nq - ki
    valid = (dist >= 0) & (dist <= reach) & ((ki >= nq) | (i > 0))
    lane = lax.broadcasted_iota(I32, (1, LANES), 1)
    head_a = ((lane // (B_HEAD_DIM // 2)) % 2) == 0
    v_head_a = lane < B_HEAD_DIM
    cq, sq = cq_ref[...], sq_ref[...]
    ck = jnp.concatenate([cp_ref[...], cq], axis=0)
    sk = jnp.concatenate([sp_ref[...], sq], axis=0)
    for pr in range(B_HEADS // 2):
        cols = slice(pr * LANES, (pr + 1) * LANES)
        q = rope(q_ref[:, cols], cq, sq) * scale
        k = rope(jnp.concatenate([kp_ref[:, cols], kc_ref[:, cols]], axis=0), ck, sk).astype(BF16)
        v = jnp.concatenate([vp_ref[:, cols], vc_ref[:, cols]], axis=0).astype(BF16)
        outs, lses = [], []
        for sel in (head_a, ~head_a):
            s = _dot_nt(jnp.where(sel, q, 0.0).astype(BF16), k)
            s = jnp.where(valid, s, MASK_VALUE)
            m = jnp.max(s, axis=-1, keepdims=True)
            p = jnp.exp(s - m)
            l = jnp.sum(p, axis=-1, keepdims=True)
            outs.append(_dot(p.astype(BF16), v) / l)
            lses.append(m + jnp.log(l))
        o_ref[:, cols] = jnp.where(v_head_a, outs[0], outs[1])
        lse_ref[:, cols] = jnp.where(v_head_a, lses[0], lses[1])


def _dil_attention(proj, cos, sin, dilation, reach, bsz, seq):
    d = D_MODEL
    sub = seq // dilation
    nb = sub // B_BLOCK
    blk = (None, None, B_BLOCK, d)
    tab = (None, None, B_BLOCK, LANES)
    prev = lambda i: jnp.maximum(i - 1, 0)
    in_specs = [
        pl.BlockSpec(blk, lambda b, r, i: (b, r, i, 0)),
        pl.BlockSpec(blk, lambda b, r, i: (b, r, prev(i), 1)),
        pl.BlockSpec(blk, lambda b, r, i: (b, r, i, 1)),
        pl.BlockSpec(blk, lambda b, r, i: (b, r, prev(i), 2)),
        pl.BlockSpec(blk, lambda b, r, i: (b, r, i, 2)),
        pl.BlockSpec(tab, lambda b, r, i: (b, r, i, 0)),
        pl.BlockSpec(tab, lambda b, r, i: (b, r, i, 0)),
        pl.BlockSpec(tab, lambda b, r, i: (b, r, prev(i), 0)),
        pl.BlockSpec(tab, lambda b, r, i: (b, r, prev(i), 0)),
    ]
    out_spec = pl.BlockSpec(blk, lambda b, r, i: (b, r, i, 0))
    out = jax.ShapeDtypeStruct((bsz, dilation, sub, d), F32)
    return pl.pallas_call(
        functools.partial(_dil_attn_kernel, reach),
        grid=(bsz, dilation, nb),
        in_specs=in_specs,
        out_specs=[out_spec, out_spec],
        out_shape=[out, out],
        compiler_params=_cparams("parallel", "parallel", "arbitrary"),
        name=f"dil_attn_{dilation}",
    )(proj, proj, proj, proj, proj, cos, sin, cos, sin)


def _dil_out_kernel(alpha, o0, o1, o2, l0, l1, l2, w_ref, h_ref, g_ref, b_ref, out_ref):
    l0v, l1v, l2v = l0[...], l1[...], l2[...]
    m = jnp.maximum(jnp.maximum(l0v, l1v), l2v)
    e0, e1, e2 = jnp.exp(l0v - m), jnp.exp(l1v - m), jnp.exp(l2v - m)
    o = (e0 * o0[...] + e1 * o1[...] + e2 * o2[...]) / (e0 + e1 + e2)
    mix = _dot(o.astype(BF16), w_ref[...])
    out_ref[...] = _layer_norm(alpha * h_ref[...] + mix, g_ref[...], b_ref[...])


def _dil_out(outs, lses, w, h, g, b, alpha, tm=256):
    m, d = h.shape
    row = lambda i: (i, 0)
    fix = lambda i: (0, 0)
    return pl.pallas_call(
        functools.partial(_dil_out_kernel, alpha),
        grid=(m // tm,),
        in_specs=[pl.BlockSpec((tm, d), row)] * 6 + [pl.BlockSpec((d, d), fix), pl.BlockSpec((tm, d), row),
                                                     pl.BlockSpec((1, d), fix), pl.BlockSpec((1, d), fix)],
        out_specs=pl.BlockSpec((tm, d), row),
        out_shape=jax.ShapeDtypeStruct((m, d), F32),
        compiler_params=_cparams("parallel"),
        name="dil_out",
    )(*outs, *lses, w, h, g, b)


def _rope_col_perm():
    half = B_HEAD_DIM // 2
    idx = []
    for pr in range(B_HEADS // 2):
        a, b = 2 * pr * B_HEAD_DIM, (2 * pr + 1) * B_HEAD_DIM
        idx += list(range(a, a + half)) + list(range(b, b + half))
        idx += list(range(a + half, a + 2 * half)) + list(range(b + half, b + 2 * half))
    return jnp.asarray(idx, dtype=I32)


def _dilated_mixer_ln(h, positions, w_in, w_out, ln_g, ln_b, alpha, bsz, seq):
    d = D_MODEL
    perm = _rope_col_perm()
    w = w_in.reshape(d, len(B_GROUPS), 3, d)
    w = jnp.concatenate([w[:, :, 0:2, :][..., perm], w[:, :, 2:3, :]], axis=2)
    w = w.reshape(d, len(B_GROUPS), 3 * d).astype(BF16)
    hb = h.astype(BF16)
    outs, lses = [], []
    for gi, (window, dilation) in enumerate(B_GROUPS):
        sub = seq // dilation

        def by_residue(a):
            return a.reshape(bsz, sub, dilation, -1).transpose(0, 2, 1, 3).reshape(bsz * seq, -1)

        def by_token(a):
            return a.transpose(0, 2, 1, 3).reshape(bsz * seq, -1)

        proj = _matmul(by_residue(hb), w[:, gi, :], F32, 512, 1024)
        cos, sin = _rope_tables(by_residue(positions.reshape(-1, 1)))
        tab = (bsz, dilation, sub, LANES)
        o, l = _dil_attention(proj.reshape(bsz, dilation, sub, 3 * d), cos.reshape(tab), sin.reshape(tab),
                              dilation, window // dilation, bsz, seq)
        outs.append(by_token(o))
        lses.append(by_token(l))
    return _dil_out(outs, lses, w_out.astype(BF16), h, ln_g, ln_b, alpha)


def _s5_operators(a_re, a_im, log_dt, b_re, b_im, c_re, c_im):
    hp = lax.Precision.HIGHEST
    L = C_CHUNK
    dt = jnp.exp(log_dt)[:, None]
    lam_re = jnp.exp(a_re * dt) * jnp.cos(a_im * dt)
    lam_im = jnp.exp(a_re * dt) * jnp.sin(a_im * dt)
    den = a_re * a_re + a_im * a_im
    fr = ((lam_re - 1.0) * a_re + lam_im * a_im) / den
    fi = (lam_im * a_re - (lam_re - 1.0) * a_im) / den
    bb_re = fr[..., None] * b_re - fi[..., None] * b_im
    bb_im = fr[..., None] * b_im + fi[..., None] * b_re
    tau = jnp.arange(L + 1, dtype=F32)[:, None, None]
    mag = jnp.exp(a_re * dt * tau)
    pw_re = mag * jnp.cos(a_im * dt * tau)
    pw_im = mag * jnp.sin(a_im * dt * tau)
    cl_re = c_re[None] * pw_re[:, :, None, :] - c_im[None] * pw_im[:, :, None, :]
    cl_im = c_re[None] * pw_im[:, :, None, :] + c_im[None] * pw_re[:, :, None, :]
    kern = (jnp.einsum('tgon,gni->tgoi', cl_re, bb_re, precision=hp)
            - jnp.einsum('tgon,gni->tgoi', cl_im, bb_im, precision=hp))
    t_idx = jnp.arange(L)
    lag = t_idx[None, :] - t_idx[:, None]
    kt = kern[jnp.clip(lag, 0, L)]
    kt = jnp.where((lag >= 0)[:, :, None, None, None], kt, 0.0)
    m_op = kt.transpose(2, 0, 4, 1, 3).reshape(C_GROUPS, L * C_GROUP_CH, L * C_GROUP_CH)
    rev = pw_re[L - 1 - t_idx], pw_im[L - 1 - t_idx]
    p_re = rev[0][..., None] * bb_re[None] - rev[1][..., None] * bb_im[None]
    p_im = rev[0][..., None] * bb_im[None] + rev[1][..., None] * bb_re[None]
    p_op = jnp.concatenate([p_re, p_im], axis=2).transpose(1, 0, 3, 2)
    p_op = p_op.reshape(C_GROUPS, L * C_GROUP_CH, 2 * C_STATE)
    q_re = cl_re[1:]
    q_im = -cl_im[1:]
    q_op = jnp.concatenate([q_re, q_im], axis=3).transpose(1, 3, 0, 2)
    q_op = q_op.reshape(C_GROUPS, 2 * C_STATE, L * C_GROUP_CH)
    lam_l = jnp.concatenate([pw_re[L], pw_im[L]], axis=-1)
    return m_op.astype(BF16), p_op.astype(BF16), q_op.astype(BF16), lam_l


def _s5_local_kernel(u_ref, m_ref, p_ref, y_ref, x_ref):
    u = u_ref[...]
    y_ref[...] = _dot(u, m_ref[...])
    x_ref[...] = _dot(u, p_ref[...])


def _s5_local(u, m_op, p_op, tr=1024):
    g, n, w = u.shape
    tr = min(tr, n)
    return pl.pallas_call(
        _s5_local_kernel,
        grid=(g, n // tr),
        in_specs=[pl.BlockSpec((None, tr, w), lambda gi, i: (gi, i, 0)),
                  pl.BlockSpec((None, w, w), lambda gi, i: (gi, 0, 0)),
                  pl.BlockSpec((None, w, 2 * C_STATE), lambda gi, i: (gi, 0, 0))],
        out_specs=[pl.BlockSpec((None, tr, w), lambda gi, i: (gi, i, 0)),
                   pl.BlockSpec((None, tr, 2 * C_STATE), lambda gi, i: (gi, i, 0))],
        out_shape=[jax.ShapeDtypeStruct((g, n, w), F32), jax.ShapeDtypeStruct((g, n, 2 * C_STATE), F32)],
        compiler_params=_cparams("parallel", "parallel"),
        name="s5_local",
    )(u, m_op, p_op)


def _s5_scan_kernel(xl_ref, la_ref, lb_ref, xs_ref):
    ng = xl_ref.shape[0]
    la = [la_ref[j] for j in range(ng)]
    lb = [lb_ref[j] for j in range(ng)]

    def step(c, xs):
        out = []
        for j in range(ng):
            xs_ref[j, c] = xs[j]
            out.append(la[j] * xs[j] + lb[j] * pltpu.roll(xs[j], C_STATE, axis=1) + xl_ref[j, c])
        return tuple(out)

    zero = jnp.zeros(xl_ref.shape[2:], F32)
    lax.fori_loop(0, xl_ref.shape[1], step, (zero,) * ng)


def _s5_scan(xl, lam_l, ng=4):
    g, nc, bsz, w = xl.shape
    lr, li = lam_l[:, :C_STATE], lam_l[:, C_STATE:]
    la = jnp.concatenate([lr, lr], axis=-1).reshape(g, 1, w)
    lb = jnp.concatenate([-li, li], axis=-1).reshape(g, 1, w)
    return pl.pallas_call(
        _s5_scan_kernel,
        grid=(g // ng,),
        in_specs=[pl.BlockSpec((ng, nc, bsz, w), lambda gi: (gi, 0, 0, 0)),
                  pl.BlockSpec((ng, 1, w), lambda gi: (gi, 0, 0)),
                  pl.BlockSpec((ng, 1, w), lambda gi: (gi, 0, 0))],
        out_specs=pl.BlockSpec((ng, nc, bsz, w), lambda gi: (gi, 0, 0, 0)),
        out_shape=jax.ShapeDtypeStruct(xl.shape, F32),
        compiler_params=_cparams("parallel"),
        name="s5_scan",
    )(xl, la, lb)


def _s5_state_out_kernel(y_ref, xs_ref, q_ref, o_ref):
    o_ref[...] = y_ref[...] + _dot(xs_ref[...].astype(BF16), q_ref[...])


def _s5_state_out(y, xs, q_op, tr=1024):
    g, n, w = y.shape
    tr = min(tr, n)
    return pl.pallas_call(
        _s5_state_out_kernel,
        grid=(g, n // tr),
        in_specs=[pl.BlockSpec((None, tr, w), lambda gi, i: (gi, i, 0)),
                  pl.BlockSpec((None, tr, 2 * C_STATE), lambda gi, i: (gi, i, 0)),
                  pl.BlockSpec((None, 2 * C_STATE, w), lambda gi, i: (gi, 0, 0))],
        out_specs=pl.BlockSpec((None, tr, w), lambda gi, i: (gi, i, 0)),
        out_shape=jax.ShapeDtypeStruct((g, n, w), F32),
        compiler_params=_cparams("parallel", "parallel"),
        name="s5_state_out",
    )(y, xs, q_op)


def _s5_out_kernel(alpha, y_ref, h_ref, dskip_ref, w_ref, g_ref, b_ref, o_ref):
    d = h_ref.shape[1]
    h = h_ref[...]
    z = y_ref[...] + dskip_ref[...] * h
    z = 0.5 * z * (1.0 + jnp.tanh(math.sqrt(2.0 / math.pi) * (z + 0.044715 * (z * z * z))))
    zb = z.astype(BF16)
    val = _dot(zb, w_ref[:, 0:d])
    gate = _dot(zb, w_ref[:, d:2 * d])
    o_ref[...] = _layer_norm(alpha * h + val * _sigmoid(gate), g_ref[...], b_ref[...])


def _s5_out(y, h, d_skip, w_glu, g, b, alpha, tm=256):
    m, d = h.shape
    row = lambda i: (i, 0)
    fix = lambda i: (0, 0)
    return pl.pallas_call(
        functools.partial(_s5_out_kernel, alpha),
        grid=(m // tm,),
        in_specs=[pl.BlockSpec((tm, d), row), pl.BlockSpec((tm, d), row), pl.BlockSpec((1, d), fix),
                  pl.BlockSpec((d, 2 * d), fix), pl.BlockSpec((1, d), fix), pl.BlockSpec((1, d), fix)],
        out_specs=pl.BlockSpec((tm, d), row),
        out_shape=jax.ShapeDtypeStruct((m, d), F32),
        compiler_params=_cparams("parallel"),
        name="s5_out",
    )(y, h, d_skip, w_glu, g, b)


def _s5_mixer_ln(h, a_re, a_im, log_dt, b_re, b_im, c_re, c_im, d_skip, w_glu, ln_g, ln_b, alpha, bsz, seq):
    L = C_CHUNK
    nc = seq // L
    m_op, p_op, q_op, lam_l = _s5_operators(a_re, a_im, log_dt, b_re, b_im, c_re, c_im)
    u = h.astype(BF16).reshape(bsz, nc, L, C_GROUPS, C_GROUP_CH)
    u = u.transpose(3, 1, 0, 2, 4).reshape(C_GROUPS, nc * bsz, L * C_GROUP_CH)
    y_loc, x_loc = _s5_local(u, m_op, p_op)
    xs = _s5_scan(x_loc.reshape(C_GROUPS, nc, bsz, 2 * C_STATE), lam_l)
    y = _s5_state_out(y_loc, xs.reshape(C_GROUPS, nc * bsz, 2 * C_STATE), q_op)
    y = y.reshape(C_GROUPS, nc, bsz, L, C_GROUP_CH).transpose(2, 1, 3, 0, 4).reshape(bsz * seq, D_MODEL)
    return _s5_out(y, h, d_skip.reshape(1, -1), w_glu.astype(BF16), ln_g, ln_b, alpha)


def _cross_kernel(alpha, h_ref, kv_ref, wq_ref, wo_ref, g_ref, b_ref, wr_ref, br_ref,
                  o_ref, idx_ref, gate_ref):
    d = h_ref.shape[1]
    h = h_ref[...]
    q = (_dot(h.astype(BF16), wq_ref[...]) * (M_HEAD_DIM ** -0.5)).astype(BF16)
    heads = []
    for hd in range(M_HEADS):
        cols = slice(hd * M_HEAD_DIM, (hd + 1) * M_HEAD_DIM)
        s = _dot_nt(q[:, cols], kv_ref[:, cols])
        p = jnp.exp(s - jnp.max(s, axis=-1, keepdims=True))
        l = jnp.sum(p, axis=-1, keepdims=True)
        heads.append((_dot(p.astype(BF16), kv_ref[:, d + hd * M_HEAD_DIM:d + (hd + 1) * M_HEAD_DIM]) / l)
                     .astype(BF16))
    att = _dot(jnp.concatenate(heads, axis=-1), wo_ref[...])
    h2 = _layer_norm(alpha * h + att, g_ref[...], b_ref[...])
    o_ref[...] = h2
    logits = (_dot(h2.astype(BF16), wr_ref[...]) + br_ref[...]).T[0:N_EXPERTS, :]
    eidx = lax.broadcasted_iota(I32, logits.shape, 0)
    vals, idxs = [], []
    for _ in range(TOP_K):
        m = jnp.max(logits, axis=0, keepdims=True)
        pick = jnp.min(jnp.where(logits == m, eidx, N_EXPERTS), axis=0, keepdims=True)
        vals.append(m)
        idxs.append(pick)
        logits = jnp.where(eidx == pick, -jnp.inf, logits)
    es = [jnp.exp(v - vals[0]) for v in vals]
    tot = es[0] + es[1] + es[2] + es[3]
    idx_ref[...] = jnp.concatenate(idxs, axis=0)
    gate_ref[...] = jnp.concatenate([e / tot for e in es], axis=0)


def _cross_attention_router(h, mem_kv, wq, wo, g, b, wr, br, alpha, bsz, seq, tm=512):
    d = D_MODEL
    n_t = seq // tm
    mlen = mem_kv.shape[1]
    row = lambda bi, i: (bi * n_t + i, 0)
    fix = lambda bi, i: (0, 0)
    col = lambda bi, i: (bi, 0, i)
    return pl.pallas_call(
        functools.partial(_cross_kernel, alpha),
        grid=(bsz, n_t),
        in_specs=[pl.BlockSpec((tm, d), row),
                  pl.BlockSpec((None, mlen, 2 * d), lambda bi, i: (bi, 0, 0)),
                  pl.BlockSpec((d, d), fix), pl.BlockSpec((d, d), fix),
                  pl.BlockSpec((1, d), fix), pl.BlockSpec((1, d), fix),
                  pl.BlockSpec((d, LANES), fix), pl.BlockSpec((1, LANES), fix)],
        out_specs=[pl.BlockSpec((tm, d), row),
                   pl.BlockSpec((None, TOP_K, tm), col), pl.BlockSpec((None, TOP_K, tm), col)],
        out_shape=[jax.ShapeDtypeStruct((bsz * seq, d), F32),
                   jax.ShapeDtypeStruct((bsz, TOP_K, seq), I32),
                   jax.ShapeDtypeStruct((bsz, TOP_K, seq), F32)],
        compiler_params=_cparams("parallel", "parallel"),
        name="cross_attn_router",
    )(h, mem_kv, wq, wo, g, b, wr, br)


def _slot_kernel(idx_ref, slot_ref, count_ref, run_ref, base_ref):
    phase = pl.program_id(0)
    first = (pl.program_id(1) == 0) & (pl.program_id(2) == 0)

    @pl.when(first & (phase == 0))
    def _():
        run_ref[...] = jnp.zeros_like(run_ref)

    @pl.when(first & (phase == 1))
    def _():
        counts = run_ref[...]
        count_ref[...] = counts.astype(I32)
        padded = jnp.ceil(counts * (1.0 / MOE_ROWS)) * MOE_ROWS
        row = lax.broadcasted_iota(I32, padded.shape, 0)
        ends = padded
        sh = 1
        while sh < N_EXPERTS:
            ends = ends + jnp.where(row >= sh, pltpu.roll(ends, sh, axis=0), 0.0)
            sh *= 2
        base_ref[...] = ends - padded
        run_ref[...] = jnp.zeros_like(run_ref)

    tn = idx_ref.shape[1]
    idx = idx_ref[...]
    eidx = lax.broadcasted_iota(I32, (N_EXPERTS, tn), 0)
    hits = [eidx == idx[k:k + 1, :] for k in range(TOP_K)]
    onehot = jnp.zeros((N_EXPERTS, tn), F32)
    for hit in hits:
        onehot = onehot + jnp.where(hit, 1.0, 0.0)

    @pl.when(phase == 1)
    def _():
        earlier = lax.broadcasted_iota(I32, (tn, tn), 0) < lax.broadcasted_iota(I32, (tn, tn), 1)
        before = _dot(onehot.astype(BF16), jnp.where(earlier, 1.0, 0.0).astype(BF16))
        before = before + run_ref[:, 0:1] + base_ref[:, 0:1]
        slots = [jnp.sum(jnp.where(hit, before, 0.0), axis=0, keepdims=True) for hit in hits]
        slot_ref[...] = jnp.concatenate(slots, axis=0).astype(I32)

    run_ref[...] = run_ref[...] + jnp.sum(onehot, axis=1, keepdims=True)


def _expert_slots(idx, tn=512):
    bsz, _, seq = idx.shape
    return pl.pallas_call(
        _slot_kernel,
        grid=(2, bsz, seq // tn),
        in_specs=[pl.BlockSpec((None, TOP_K, tn), lambda p, bi, i: (bi, 0, i))],
        out_specs=[pl.BlockSpec((None, TOP_K, tn), lambda p, bi, i: (bi * p, 0, i * p)),
                   pl.BlockSpec((N_EXPERTS, LANES), lambda p, bi, i: (0, 0))],
        out_shape=[jax.ShapeDtypeStruct(idx.shape, I32), jax.ShapeDtypeStruct((N_EXPERTS, LANES), I32)],
        scratch_shapes=[pltpu.VMEM((N_EXPERTS, LANES), F32), pltpu.VMEM((N_EXPERTS, LANES), F32)],
        compiler_params=_cparams("arbitrary", "arbitrary", "arbitrary"),
        name="expert_slots",
    )(idx)


def _dispatch_kernel(slot_ref, x_ref, xs_in_ref, xs_ref, sem):
    del xs_in_ref
    tm = x_ref.shape[0]

    for k in range(TOP_K):
        def start(t, c, k=k):
            pltpu.make_async_copy(x_ref.at[pl.ds(t, 1)], xs_ref.at[pl.ds(slot_ref[0, k * tm + t], 1)],
                                  sem).start()
            return c

        lax.fori_loop(0, tm, start, 0, unroll=DMA_UNROLL)
    for k in range(TOP_K):
        pltpu.make_async_copy(x_ref, xs_ref.at[pl.ds(0, tm)], sem).wait()


def _dispatch(x, slots, n_rows, tm=512):
    t, d = x.shape
    zeros = jnp.zeros((n_rows, d), x.dtype)
    return pl.pallas_call(
        _dispatch_kernel,
        grid=(t // tm,),
        in_specs=[pl.BlockSpec((None, 1, TOP_K * tm), lambda i: (i, 0, 0), memory_space=pltpu.SMEM),
                  pl.BlockSpec((tm, d), lambda i: (i, 0)),
                  pl.BlockSpec(memory_space=pl.ANY)],
        out_specs=pl.BlockSpec(memory_space=pl.ANY),
        out_shape=jax.ShapeDtypeStruct(zeros.shape, zeros.dtype),
        scratch_shapes=[pltpu.SemaphoreType.DMA(())],
        input_output_aliases={2: 0},
        compiler_params=_cparams("arbitrary"),
        name="moe_dispatch",
    )(slots, x, zeros)


def _expert_ffn_kernel(be_ref, nu_ref, x_ref, w1_ref, b1_ref, w2_ref, b2_ref, y_ref):
    d = x_ref.shape[1]

    @pl.when(pl.program_id(0) < nu_ref[0])
    def _():
        xb = x_ref[...].astype(BF16)
        glu = jnp.minimum(_dot(xb, w1_ref[:, 0:d]) + b1_ref[:, 0:d], SWIGLU_LIMIT)
        lin = jnp.clip(_dot(xb, w1_ref[:, d:2 * d]) + b1_ref[:, d:2 * d], -SWIGLU_LIMIT, SWIGLU_LIMIT)
        act = glu * _sigmoid(SWIGLU_ALPHA * glu) * (lin + 1.0)
        y_ref[...] = _dot(act.astype(BF16), w2_ref[...]) + b2_ref[...]

    @pl.when(pl.program_id(0) >= nu_ref[0])
    def _():
        y_ref[...] = jnp.zeros_like(y_ref)


def _expert_ffn(xs, block_e, n_used, w1, b1, w2, b2):
    n_rows, d = xs.shape
    nb = n_rows // MOE_ROWS
    wsel = lambda i, be, nu: (be[i], 0, 0)
    grid_spec = pltpu.PrefetchScalarGridSpec(
        num_scalar_prefetch=2,
        grid=(nb,),
        in_specs=[pl.BlockSpec((MOE_ROWS, d), lambda i, be, nu: (i, 0)),
                  pl.BlockSpec((None, d, 2 * d), wsel), pl.BlockSpec((None, 1, 2 * d), wsel),
                  pl.BlockSpec((None, d, d), wsel), pl.BlockSpec((None, 1, d), wsel)],
        out_specs=pl.BlockSpec((MOE_ROWS, d), lambda i, be, nu: (i, 0)),
    )
    return pl.pallas_call(
        _expert_ffn_kernel,
        grid_spec=grid_spec,
        out_shape=jax.ShapeDtypeStruct((n_rows, d), F32),
        compiler_params=_cparams("arbitrary"),
        name="expert_ffn",
    )(block_e, n_used, xs, w1, b1, w2, b2)


def _combine_kernel(alpha, slot_ref, nslot_ref, gate_ref, h_ref, g_ref, b_ref, ys_ref, o_ref, buf, sem):
    tm = h_ref.shape[0]
    i = pl.program_id(0)

    def gather(s_ref, half):
        for k in range(TOP_K):
            def start(t, c, k=k):
                pltpu.make_async_copy(ys_ref.at[pl.ds(s_ref[0, k * tm + t], 1)],
                                      buf.at[half, k, pl.ds(t, 1)], sem.at[half]).start()
                return c

            lax.fori_loop(0, tm, start, 0, unroll=DMA_UNROLL)

    @pl.when(i == 0)
    def _():
        gather(slot_ref, 0)

    @pl.when(i + 1 < pl.num_programs(0))
    def _():
        gather(nslot_ref, (i + 1) % 2)

    cur = i % 2
    for k in range(TOP_K):
        pltpu.make_async_copy(ys_ref.at[pl.ds(0, tm)], buf.at[cur, k], sem.at[cur]).wait()
    gate = gate_ref[...]
    y = buf[cur, 0] * gate[:, 0:1]
    for k in range(1, TOP_K):
        y = y + buf[cur, k] * gate[:, k:k + 1]
    o_ref[...] = _layer_norm(alpha * h_ref[...] + y, g_ref[...], b_ref[...])


def _combine_ln(ys, slots, gates, h, g, b, alpha, tm=256):
    t, d = h.shape
    n_t = t // tm
    row = lambda i: (i, 0)
    fix = lambda i: (0, 0)
    slot_block = (None, 1, TOP_K * tm)
    return pl.pallas_call(
        functools.partial(_combine_kernel, alpha),
        grid=(n_t,),
        in_specs=[pl.BlockSpec(slot_block, lambda i: (i, 0, 0), memory_space=pltpu.SMEM),
                  pl.BlockSpec(slot_block, lambda i: (jnp.minimum(i + 1, n_t - 1), 0, 0),
                               memory_space=pltpu.SMEM),
                  pl.BlockSpec((tm, TOP_K), row), pl.BlockSpec((tm, d), row),
                  pl.BlockSpec((1, d), fix), pl.BlockSpec((1, d), fix),
                  pl.BlockSpec(memory_space=pl.ANY)],
        out_specs=pl.BlockSpec((tm, d), row),
        out_shape=jax.ShapeDtypeStruct((t, d), F32),
        scratch_shapes=[pltpu.VMEM((2, TOP_K, tm, d), F32), pltpu.SemaphoreType.DMA((2,))],
        compiler_params=_cparams("arbitrary"),
        name="moe_combine_ln",
    )(slots, slots, gates, h, g, b, ys)


def _tile_slots(slot, tm):
    bsz, _, seq = slot.shape
    s = slot.reshape(bsz, TOP_K, seq // tm, tm).transpose(0, 2, 1, 3)
    return s.reshape(bsz * (seq // tm), 1, TOP_K * tm)


def _moe_ln(h2, idx, gates, w1, b1, w2, b2, ln_g, ln_b, alpha, tm_d=512, tm_c=256):
    t, d = h2.shape
    slot, counts = _expert_slots(idx)
    blocks_per_e = (counts[:, 0] + MOE_ROWS - 1) // MOE_ROWS
    block_end = jnp.cumsum(blocks_per_e)
    nb = t * TOP_K // MOE_ROWS + N_EXPERTS
    block_e = jnp.sum((block_end[None, :] <= jnp.arange(nb, dtype=I32)[:, None]).astype(I32), axis=1)
    block_e = jnp.minimum(block_e, N_EXPERTS - 1)
    n_used = block_end[-1:].astype(I32)
    xs = _dispatch(h2, _tile_slots(slot, tm_d), nb * MOE_ROWS, tm_d)
    ys = _expert_ffn(xs, block_e, n_used, w1.astype(BF16), b1[:, None, :], w2.astype(BF16), b2[:, None, :])
    gates_t = gates.transpose(0, 2, 1).reshape(t, TOP_K)
    return _combine_ln(ys, _tile_slots(slot, tm_c), gates_t, h2, ln_g[None], ln_b[None], alpha, tm_c)


def kernel(x, mem, positions, ln_g, ln_b, a_w_in, a_lower_bounds, a_norm_g, a_w_out, b_w_in, b_w_out,
           c_a_re, c_a_im, c_log_dt, c_b_re, c_b_im, c_c_re, c_c_im, c_d, c_w_glu, m_w_kv, m_w_q, m_w_o,
           r_w, r_b, e_w1, e_b1, e_w2, e_b2):
    bsz, seq, d = x.shape
    depth = ln_g.shape[0]
    alpha = (2 * depth) ** 0.25
    t = bsz * seq
    lb = jax.nn.softmax(a_lower_bounds.astype(F32), axis=0)
    lb = jnp.cumsum(lb, axis=0) - lb[0]
    mem_kv = _matmul(mem.reshape(-1, d), m_w_kv.astype(BF16), BF16, 512, 1024)
    mem_kv = mem_kv.reshape(bsz, mem.shape[1], 2 * d)
    h = x.reshape(t, d)
    for layer in range(depth):
        kind, j = layer % N_MIXERS, layer // N_MIXERS
        g1, b1 = ln_g[layer, 0][None], ln_b[layer, 0][None]
        if kind == 0:
            h = _hgrn2_mixer_ln(h, a_w_in[j], a_norm_g[j], a_w_out[j], lb[layer], g1, b1, alpha, bsz, seq)
        elif kind == 1:
            h = _dilated_mixer_ln(h, positions, b_w_in[j], b_w_out[j], g1, b1, alpha, bsz, seq)
        else:
            h = _s5_mixer_ln(h, c_a_re[j], c_a_im[j], c_log_dt[j], c_b_re[j], c_b_im[j], c_c_re[j],
                             c_c_im[j], c_d[j], c_w_glu[j], g1, b1, alpha, bsz, seq)
        wr = jnp.pad(r_w[layer], ((0, 0), (0, LANES - N_EXPERTS))).astype(BF16)
        br = jnp.pad(r_b[layer], (0, LANES - N_EXPERTS))[None]
        h, idx, gates = _cross_attention_router(h, mem_kv, m_w_q[layer].astype(BF16), m_w_o[layer].astype(BF16),
                                                ln_g[layer, 1][None], ln_b[layer, 1][None], wr, br,
                                                alpha, bsz, seq)
        h = _moe_ln(h, idx, gates, e_w1[layer], e_b1[layer], e_w2[layer], e_b2[layer],
                    ln_g[layer, 2], ln_b[layer, 2], alpha)
    return h.reshape(bsz, seq, d)
```

```python
import functools
import math

import jax
import jax.numpy as jnp
from jax import lax
from jax.experimental import pallas as pl
from jax.experimental.pallas import tpu as pltpu

F32 = jnp.float32
BF16 = jnp.bfloat16
I32 = jnp.int32

D_MODEL = 1024
N_MIXERS = 3
NORM_EPS = 1e-5
ROPE_THETA = 10000.0
MIN_FORGET = 1e-6
MASK_VALUE = -1e30

A_HEADS = 8
A_DIM = 128
A_CHUNK = 64
A_SUB = 8

B_GROUPS = ((128, 1), (512, 4), (2048, 16))
B_HEAD_DIM = 64
B_HEADS = 16
B_BLOCK = 128

C_GROUP_CH = 16
C_GROUPS = 64
C_STATE = 64
C_CHUNK = 16

M_HEADS = 4
M_HEAD_DIM = 256

N_EXPERTS = 32
TOP_K = 4
SWIGLU_ALPHA = 1.702
SWIGLU_LIMIT = 7.0
MOE_ROWS = 512
DMA_UNROLL = 8

LANES = 128
VMEM_LIMIT = 48 * 1024 * 1024


def _cparams(*sem):
    return pltpu.CompilerParams(dimension_semantics=sem, vmem_limit_bytes=VMEM_LIMIT)


def _layer_norm(z, g, b):
    mu = jnp.mean(z, axis=-1, keepdims=True)
    zc = z - mu
    var = jnp.mean(zc * zc, axis=-1, keepdims=True)
    return zc * lax.rsqrt(var + NORM_EPS) * g + b


def _sigmoid(x):
    return 1.0 / (1.0 + jnp.exp(-x))


def _dot(a, b):
    return jnp.dot(a, b, preferred_element_type=F32)


def _dot_nt(a, b):
    return lax.dot_general(a, b, (((1,), (1,)), ((), ())), preferred_element_type=F32)


def _dot_tn(a, b):
    return lax.dot_general(a, b, (((0,), (0,)), ((), ())), preferred_element_type=F32)


def _matmul_kernel(x_ref, w_ref, o_ref):
    o_ref[...] = _dot(x_ref[...].astype(BF16), w_ref[...]).astype(o_ref.dtype)


def _matmul(x, w, out_dtype, tm, tn):
    m, k = x.shape
    n = w.shape[1]
    return pl.pallas_call(
        _matmul_kernel,
        grid=(n // tn, m // tm),
        in_specs=[pl.BlockSpec((tm, k), lambda j, i: (i, 0)),
                  pl.BlockSpec((k, tn), lambda j, i: (0, j))],
        out_specs=pl.BlockSpec((tm, tn), lambda j, i: (i, j)),
        out_shape=jax.ShapeDtypeStruct((m, n), out_dtype),
        compiler_params=_cparams("parallel", "parallel"),
        name="matmul",
    )(x, w)


def _proj_res_ln_kernel(alpha, a_ref, w_ref, h_ref, g_ref, b_ref, o_ref):
    mix = _dot(a_ref[...].astype(BF16), w_ref[...])
    o_ref[...] = _layer_norm(alpha * h_ref[...] + mix, g_ref[...], b_ref[...])


def _proj_res_ln(a, w, h, g, b, alpha, tm=512):
    m, d = h.shape
    k = a.shape[1]
    row = lambda i: (i, 0)
    fix = lambda i: (0, 0)
    return pl.pallas_call(
        functools.partial(_proj_res_ln_kernel, alpha),
        grid=(m // tm,),
        in_specs=[pl.BlockSpec((tm, k), row), pl.BlockSpec((k, d), fix),
                  pl.BlockSpec((tm, d), row), pl.BlockSpec((1, d), fix), pl.BlockSpec((1, d), fix)],
        out_specs=pl.BlockSpec((tm, d), row),
        out_shape=jax.ShapeDtypeStruct((m, d), F32),
        compiler_params=_cparams("parallel"),
        name="proj_res_ln",
    )(a, w, h, g, b)


def _hgrn_in_kernel(x_ref, w_ref, lb_ref, q_ref, k_ref, v_ref, lf_ref, g_ref):
    d = x_ref.shape[1]
    xb = x_ref[...].astype(BF16)
    q = _dot(xb, w_ref[:, 0:d])
    q_ref[...] = q * _sigmoid(q)
    f = _dot(xb, w_ref[:, d:2 * d])
    lb = lb_ref[...]
    fg = lb + (1.0 - lb) * _sigmoid(f)
    lf_ref[...] = jnp.log(jnp.maximum(fg, MIN_FORGET))
    k_ref[...] = 1.0 - fg
    v_ref[...] = _dot(xb, w_ref[:, 2 * d:3 * d])
    g = _dot(xb, w_ref[:, 3 * d:4 * d])
    g_ref[...] = g * _sigmoid(g)


def _hgrn_in(x, w, lb, tm=256):
    m, d = x.shape
    row = lambda i: (i, 0)
    fix = lambda i: (0, 0)
    out = jax.ShapeDtypeStruct((m, d), F32)
    return pl.pallas_call(
        _hgrn_in_kernel,
        grid=(m // tm,),
        in_specs=[pl.BlockSpec((tm, d), row), pl.BlockSpec((d, 4 * d), fix), pl.BlockSpec((1, d), fix)],
        out_specs=[pl.BlockSpec((tm, d), row)] * 5,
        out_shape=[out] * 5,
        compiler_params=_cparams("parallel"),
        name="hgrn_in",
    )(x, w, lb)


def _rows_from(b, picks, span):
    parts = [jnp.broadcast_to(b[p:p + 1, :], (span, b.shape[1])) for p in picks]
    return parts[0] if len(parts) == 1 else jnp.concatenate(parts, axis=0)


def _hgrn_chunk_kernel(q_ref, k_ref, v_ref, lf_ref, g_ref, ng_ref, o_ref, state_ref):
    c = A_CHUNK
    n_sub = q_ref.shape[0] // c

    @pl.when(pl.program_id(1) == 0)
    def _():
        state_ref[...] = jnp.zeros_like(state_ref)

    ti = lax.broadcasted_iota(I32, (c, c), 0)
    si = lax.broadcasted_iota(I32, (c, c), 1)
    row = lax.broadcasted_iota(I32, (c, A_DIM), 0)
    halves = []
    half = c // 2
    while half >= A_SUB:
        halves.append(half)
        half //= 2
    level_masks = [((ti // (2 * hf)) == (si // (2 * hf))) & ((ti % (2 * hf)) >= hf) & ((si % (2 * hf)) < hf)
                   for hf in halves]
    diag_mask = ((ti // A_SUB) == (si // A_SUB)) & (si <= ti)

    def chunk(ci, carry):
        r0 = pl.multiple_of(ci * c, c)
        for hd in range(A_HEADS):
            cols = slice(hd * A_DIM, (hd + 1) * A_DIM)
            q = q_ref[pl.ds(r0, c), cols]
            k = k_ref[pl.ds(r0, c), cols]
            v = v_ref[pl.ds(r0, c), cols]
            b = lf_ref[pl.ds(r0, c), cols]
            sh = 1
            while sh < c:
                b = b + jnp.where(row >= sh, pltpu.roll(b, sh, axis=0), 0.0)
                sh *= 2
            scores = jnp.zeros((c, c), F32)
            for hf, mask in zip(halves, level_masks):
                ref = _rows_from(b, [blk * 2 * hf + hf - 1 for blk in range(c // (2 * hf))], 2 * hf)
                qt = (q * jnp.exp(jnp.minimum(b - ref, 0.0))).astype(BF16)
                kt = (k * jnp.exp(jnp.minimum(ref - b, 0.0))).astype(BF16)
                scores = scores + jnp.where(mask, _dot_nt(qt, kt), 0.0)
            lo = _rows_from(b, [blk * A_SUB for blk in range(c // A_SUB)], A_SUB)
            hi = _rows_from(b, [blk * A_SUB + A_SUB - 1 for blk in range(c // A_SUB)], A_SUB)
            mid = 0.5 * (lo + hi)
            qt = (q * jnp.exp(b - mid)).astype(BF16)
            kt = (k * jnp.exp(mid - b)).astype(BF16)
            scores = scores + jnp.where(diag_mask, _dot_nt(qt, kt), 0.0)

            st = state_ref[hd]
            b_end = b[c - 1:c, :]
            o = _dot(scores.astype(BF16), v.astype(BF16))
            o = o + _dot_nt((q * jnp.exp(b)).astype(BF16), st.astype(BF16))
            ke = (k * jnp.exp(b_end - b)).astype(BF16)
            state_ref[hd] = st * jnp.exp(b_end) + _dot_tn(v.astype(BF16), ke)

            o = o * lax.rsqrt(jnp.mean(o * o, axis=-1, keepdims=True) + NORM_EPS) * ng_ref[:, cols]
            o_ref[pl.ds(r0, c), cols] = (o * g_ref[pl.ds(r0, c), cols]).astype(o_ref.dtype)
        return carry

    lax.fori_loop(0, n_sub, chunk, 0)


def _hgrn_chunk(q, k, v, lf, g, ng, bsz, seq, tc=256):
    d = q.shape[1]
    n_t = seq // tc
    row = lambda b, i: (b * n_t + i, 0)
    return pl.pallas_call(
        _hgrn_chunk_kernel,
        grid=(bsz, n_t),
        in_specs=[pl.BlockSpec((tc, d), row)] * 5 + [pl.BlockSpec((1, d), lambda b, i: (0, 0))],
        out_specs=pl.BlockSpec((tc, d), row),
        out_shape=jax.ShapeDtypeStruct(q.shape, BF16),
        scratch_shapes=[pltpu.VMEM((A_HEADS, A_DIM, A_DIM), F32)],
        compiler_params=_cparams("parallel", "arbitrary"),
        name="hgrn_chunk",
    )(q, k, v, lf, g, ng)


def _hgrn2_mixer_ln(h, w_in, norm_g, w_out, lb, ln_g, ln_b, alpha, bsz, seq):
    q, k, v, lf, g = _hgrn_in(h, w_in.astype(BF16), lb.reshape(1, -1))
    o = _hgrn_chunk(q, k, v, lf, g, norm_g.reshape(1, -1), bsz, seq)
    return _proj_res_ln(o, w_out.astype(BF16), h, ln_g, ln_b, alpha)


def _rope_table_kernel(pos_ref, cos_ref, sin_ref):
    half = B_HEAD_DIM // 2
    lane = lax.broadcasted_iota(I32, cos_ref.shape, 1)
    j = (lane % half).astype(F32)
    inv_freq = jnp.exp(j * (-math.log(ROPE_THETA) / half))
    ang = pos_ref[...].astype(F32) * inv_freq
    cos_ref[...] = jnp.cos(ang)
    sin_ref[...] = jnp.where(lane < 2 * half, -1.0, 1.0) * jnp.sin(ang)


def _rope_tables(pos_col, tm=1024):
    t = pos_col.shape[0]
    out = jax.ShapeDtypeStruct((t, LANES), F32)
    return pl.pallas_call(
        _rope_table_kernel,
        grid=(t // tm,),
        in_specs=[pl.BlockSpec((tm, 1), lambda i: (i, 0))],
        out_specs=[pl.BlockSpec((tm, LANES), lambda i: (i, 0))] * 2,
        out_shape=[out, out],
        compiler_params=_cparams("parallel"),
        name="rope_tables",
    )(pos_col)


def _dil_attn_kernel(reach, q_ref, kp_ref, kc_ref, vp_ref, vc_ref, cq_ref, sq_ref, cp_ref, sp_ref,
                     o_ref, lse_ref):
    nq = q_ref.shape[0]
    i = pl.program_id(2)
    scale = B_HEAD_DIM ** -0.5
    half_lanes = LANES // 2

    def rope(x, cos, sin):
        return x * cos + pltpu.roll(x, half_lanes, axis=1) * sin

    qi = lax.broadcasted_iota(I32, (nq, 2 * nq), 0)
    ki = lax.broadcasted_iota(I32, (nq, 2 * nq), 1)
    dist = qi + nq - ki
    valid = (dist >= 0) & (dist <= reach) & ((ki >= nq) | (i > 0))
    lane = lax.broadcasted_iota(I32, (1, LANES), 1)
    head_a = ((lane // (B_HEAD_DIM // 2)) % 2) == 0
    v_head_a = lane < B_HEAD_DIM
    cq, sq = cq_ref[...], sq_ref[...]
    ck = jnp.concatenate([cp_ref[...], cq], axis=0)
    sk = jnp.concatenate([sp_ref[...], sq], axis=0)
    for pr in range(B_HEADS // 2):
        cols = slice(pr * LANES, (pr + 1) * LANES)
        q = rope(q_ref[:, cols], cq, sq) * scale
        k = rope(jnp.concatenate([kp_ref[:, cols], kc_ref[:, cols]], axis=0), ck, sk).astype(BF16)
        v = jnp.concatenate([vp_ref[:, cols], vc_ref[:, cols]], axis=0).astype(BF16)
        outs, lses = [], []
        for sel in (head_a, ~head_a):
            s = _dot_nt(jnp.where(sel, q, 0.0).astype(BF16), k)
            s = jnp.where(valid, s, MASK_VALUE)
            m = jnp.max(s, axis=-1, keepdims=True)
            p = jnp.exp(s - m)
            l = jnp.sum(p, axis=-1, keepdims=True)
            outs.append(_dot(p.astype(BF16), v) / l)
            lses.append(m + jnp.log(l))
        o_ref[:, cols] = jnp.where(v_head_a, outs[0], outs[1])
        lse_ref[:, cols] = jnp.where(v_head_a, lses[0], lses[1])


def _residue_perm(tm, dilation):
    per = tm // dilation
    j = jnp.arange(tm)
    src = (j % per) * dilation + j // per
    return (src[:, None] == jnp.arange(tm)[None, :]).astype(BF16)


def _regroup_rows(p, x):
    hi = x.astype(BF16)
    rest = x - hi.astype(F32)
    mid = rest.astype(BF16)
    lo = (rest - mid.astype(F32)).astype(BF16)
    return _dot(p, hi) + _dot(p, mid) + _dot(p, lo)


def _dil_proj_kernel(x_ref, p_ref, w_ref, o_ref):
    xs = _dot(p_ref[...], x_ref[...].astype(BF16)).astype(BF16)
    o_ref[...] = _dot(xs, w_ref[...]).reshape(o_ref.shape)


def _dil_proj(h, w, dilation, bsz, seq, tm=512):
    d, n = w.shape
    n_t = seq // tm
    return pl.pallas_call(
        _dil_proj_kernel,
        grid=(bsz * n_t,),
        in_specs=[pl.BlockSpec((tm, d), lambda i: (i, 0)), pl.BlockSpec((tm, tm), lambda i: (0, 0)),
                  pl.BlockSpec((d, n), lambda i: (0, 0))],
        out_specs=pl.BlockSpec((None, dilation, tm // dilation, n), lambda i: (i // n_t, 0, i % n_t, 0)),
        out_shape=jax.ShapeDtypeStruct((bsz, dilation, seq // dilation, n), F32),
        compiler_params=_cparams("parallel"),
        name=f"dil_proj_{dilation}",
    )(h, _residue_perm(tm, dilation), w)


def _dil_attention(proj, cos, sin, dilation, reach, bsz, seq):
    d = D_MODEL
    sub = seq // dilation
    nb = sub // B_BLOCK
    blk = (None, None, B_BLOCK, d)
    tab = (None, None, B_BLOCK, LANES)
    prev = lambda i: jnp.maximum(i - 1, 0)
    in_specs = [
        pl.BlockSpec(blk, lambda b, r, i: (b, r, i, 0)),
        pl.BlockSpec(blk, lambda b, r, i: (b, r, prev(i), 1)),
        pl.BlockSpec(blk, lambda b, r, i: (b, r, i, 1)),
        pl.BlockSpec(blk, lambda b, r, i: (b, r, prev(i), 2)),
        pl.BlockSpec(blk, lambda b, r, i: (b, r, i, 2)),
        pl.BlockSpec(tab, lambda b, r, i: (b, r, i, 0)),
        pl.BlockSpec(tab, lambda b, r, i: (b, r, i, 0)),
        pl.BlockSpec(tab, lambda b, r, i: (b, r, prev(i), 0)),
        pl.BlockSpec(tab, lambda b, r, i: (b, r, prev(i), 0)),
    ]
    out_spec = pl.BlockSpec(blk, lambda b, r, i: (b, r, i, 0))
    out = jax.ShapeDtypeStruct((bsz, dilation, sub, d), F32)
    return pl.pallas_call(
        functools.partial(_dil_attn_kernel, reach),
        grid=(bsz, dilation, nb),
        in_specs=in_specs,
        out_specs=[out_spec, out_spec],
        out_shape=[out, out],
        compiler_params=_cparams("parallel", "parallel", "arbitrary"),
        name=f"dil_attn_{dilation}",
    )(proj, proj, proj, proj, proj, cos, sin, cos, sin)


def _dil_out_kernel(alpha, o0, o1, o2, l0, l1, l2, p1_ref, p2_ref, w_ref, h_ref, g_ref, b_ref, out_ref):
    tm, d = h_ref.shape
    o1v = _regroup_rows(p1_ref[...], o1[...].reshape(tm, d))
    l1v = _regroup_rows(p1_ref[...], l1[...].reshape(tm, d))
    o2v = _regroup_rows(p2_ref[...], o2[...].reshape(tm, d))
    l2v = _regroup_rows(p2_ref[...], l2[...].reshape(tm, d))
    l0v = l0[...]
    m = jnp.maximum(jnp.maximum(l0v, l1v), l2v)
    e0, e1, e2 = jnp.exp(l0v - m), jnp.exp(l1v - m), jnp.exp(l2v - m)
    o = (e0 * o0[...] + e1 * o1v + e2 * o2v) / (e0 + e1 + e2)
    mix = _dot(o.astype(BF16), w_ref[...])
    out_ref[...] = _layer_norm(alpha * h_ref[...] + mix, g_ref[...], b_ref[...])


def _dil_out(outs, lses, w, h, g, b, alpha, bsz, seq, tm=256):
    m, d = h.shape
    n_t = seq // tm
    row = lambda i: (i, 0)
    fix = lambda i: (0, 0)

    def grouped(a):
        dil = a.shape[1]
        return pl.BlockSpec((None, dil, tm // dil, d), lambda i: (i // n_t, 0, i % n_t, 0))

    perms = [_residue_perm(tm, a.shape[1]).T for a in outs[1:]]
    return pl.pallas_call(
        functools.partial(_dil_out_kernel, alpha),
        grid=(m // tm,),
        in_specs=[pl.BlockSpec((tm, d), row), grouped(outs[1]), grouped(outs[2]),
                  pl.BlockSpec((tm, d), row), grouped(lses[1]), grouped(lses[2]),
                  pl.BlockSpec((tm, tm), fix), pl.BlockSpec((tm, tm), fix),
                  pl.BlockSpec((d, d), fix), pl.BlockSpec((tm, d), row),
                  pl.BlockSpec((1, d), fix), pl.BlockSpec((1, d), fix)],
        out_specs=pl.BlockSpec((tm, d), row),
        out_shape=jax.ShapeDtypeStruct((m, d), F32),
        compiler_params=_cparams("parallel"),
        name="dil_out",
    )(outs[0].reshape(m, d), outs[1], outs[2], lses[0].reshape(m, d), lses[1], lses[2], *perms, w, h, g, b)


def _rope_col_perm():
    half = B_HEAD_DIM // 2
    idx = []
    for pr in range(B_HEADS // 2):
        a, b = 2 * pr * B_HEAD_DIM, (2 * pr + 1) * B_HEAD_DIM
        idx += list(range(a, a + half)) + list(range(b, b + half))
        idx += list(range(a + half, a + 2 * half)) + list(range(b + half, b + 2 * half))
    return jnp.asarray(idx, dtype=I32)


def _dilated_mixer_ln(h, positions, w_in, w_out, ln_g, ln_b, alpha, bsz, seq):
    d = D_MODEL
    perm = _rope_col_perm()
    w = w_in.reshape(d, len(B_GROUPS), 3, d)
    w = jnp.concatenate([w[:, :, 0:2, :][..., perm], w[:, :, 2:3, :]], axis=2)
    w = w.reshape(d, len(B_GROUPS), 3 * d).astype(BF16)
    outs, lses = [], []
    for gi, (window, dilation) in enumerate(B_GROUPS):
        sub = seq // dilation
        if dilation == 1:
            proj = _matmul(h, w[:, gi, :], F32, 512, 1024).reshape(bsz, 1, seq, 3 * d)
        else:
            proj = _dil_proj(h, w[:, gi, :], dilation, bsz, seq)
        pos =positions.reshape(bsz, sub, dilation).transpose(0, 2, 1).reshape(-1, 1)
        cos, sin = _rope_tables(pos)
        tab = (bsz, dilation, sub, LANES)
        o, l = _dil_attention(proj, cos.reshape(tab), sin.reshape(tab), dilation, window // dilation, bsz, seq)
        outs.append(o)
        lses.append(l)
    return _dil_out(outs, lses, w_out.astype(BF16), h, ln_g, ln_b, alpha, bsz, seq)


def _s5_operators(a_re, a_im, log_dt, b_re, b_im, c_re, c_im):
    hp = lax.Precision.HIGHEST
    L = C_CHUNK
    dt = jnp.exp(log_dt)[:, None]
    lam_re = jnp.exp(a_re * dt) * jnp.cos(a_im * dt)
    lam_im = jnp.exp(a_re * dt) * jnp.sin(a_im * dt)
    den = a_re * a_re + a_im * a_im
    fr = ((lam_re - 1.0) * a_re + lam_im * a_im) / den
    fi = (lam_im * a_re - (lam_re - 1.0) * a_im) / den
    bb_re = fr[..., None] * b_re - fi[..., None] * b_im
    bb_im = fr[..., None] * b_im + fi[..., None] * b_re
    tau = jnp.arange(L + 1, dtype=F32)[:, None, None]
    mag = jnp.exp(a_re * dt * tau)
    pw_re = mag * jnp.cos(a_im * dt * tau)
    pw_im = mag * jnp.sin(a_im * dt * tau)
    cl_re = c_re[None] * pw_re[:, :, None, :] - c_im[None] * pw_im[:, :, None, :]
    cl_im = c_re[None] * pw_im[:, :, None, :] + c_im[None] * pw_re[:, :, None, :]
    kern = (jnp.einsum('tgon,gni->tgoi', cl_re, bb_re, precision=hp)
            - jnp.einsum('tgon,gni->tgoi', cl_im, bb_im, precision=hp))
    t_idx = jnp.arange(L)
    lag = t_idx[None, :] - t_idx[:, None]
    kt = kern[jnp.clip(lag, 0, L)]
    kt = jnp.where((lag >= 0)[:, :, None, None, None], kt, 0.0)
    m_op = kt.transpose(2, 0, 4, 1, 3).reshape(C_GROUPS, L * C_GROUP_CH, L * C_GROUP_CH)
    rev = pw_re[L - 1 - t_idx], pw_im[L - 1 - t_idx]
    p_re = rev[0][..., None] * bb_re[None] - rev[1][..., None] * bb_im[None]
    p_im = rev[0][..., None] * bb_im[None] + rev[1][..., None] * bb_re[None]
    p_op = jnp.concatenate([p_re, p_im], axis=2).transpose(1, 0, 3, 2)
    p_op = p_op.reshape(C_GROUPS, L * C_GROUP_CH, 2 * C_STATE)
    q_re = cl_re[1:]
    q_im = -cl_im[1:]
    q_op = jnp.concatenate([q_re, q_im], axis=3).transpose(1, 3, 0, 2)
    q_op = q_op.reshape(C_GROUPS, 2 * C_STATE, L * C_GROUP_CH)
    lam_l = jnp.concatenate([pw_re[L], pw_im[L]], axis=-1)
    return m_op.astype(BF16), p_op.astype(BF16), q_op.astype(BF16), lam_l


def _s5_local_kernel(u_ref, m_ref, p_ref, y_ref, x_ref):
    u = u_ref[...]
    y_ref[...] = _dot(u, m_ref[...])
    x_ref[...] = _dot(u, p_ref[...])


def _s5_local(u, m_op, p_op, tr=1024):
    g, n, w = u.shape
    tr = min(tr, n)
    return pl.pallas_call(
        _s5_local_kernel,
        grid=(g, n // tr),
        in_specs=[pl.BlockSpec((None, tr, w), lambda gi, i: (gi, i, 0)),
                  pl.BlockSpec((None, w, w), lambda gi, i: (gi, 0, 0)),
                  pl.BlockSpec((None, w, 2 * C_STATE), lambda gi, i: (gi, 0, 0))],
        out_specs=[pl.BlockSpec((None, tr, w), lambda gi, i: (gi, i, 0)),
                   pl.BlockSpec((None, tr, 2 * C_STATE), lambda gi, i: (gi, i, 0))],
        out_shape=[jax.ShapeDtypeStruct((g, n, w), F32), jax.ShapeDtypeStruct((g, n, 2 * C_STATE), F32)],
        compiler_params=_cparams("parallel", "parallel"),
        name="s5_local",
    )(u, m_op, p_op)


def _s5_scan_kernel(xl_ref, la_ref, lb_ref, xs_ref):
    ng = xl_ref.shape[0]
    la = [la_ref[j] for j in range(ng)]
    lb = [lb_ref[j] for j in range(ng)]

    def step(c, xs):
        out = []
        for j in range(ng):
            xs_ref[j, c] = xs[j]
            out.append(la[j] * xs[j] + lb[j] * pltpu.roll(xs[j], C_STATE, axis=1) + xl_ref[j, c])
        return tuple(out)

    zero = jnp.zeros(xl_ref.shape[2:], F32)
    lax.fori_loop(0, xl_ref.shape[1], step, (zero,) * ng)


def _s5_scan(xl, lam_l, ng=4):
    g, nc, bsz, w = xl.shape
    lr, li = lam_l[:, :C_STATE], lam_l[:, C_STATE:]
    la = jnp.concatenate([lr, lr], axis=-1).reshape(g, 1, w)
    lb = jnp.concatenate([-li, li], axis=-1).reshape(g, 1, w)
    return pl.pallas_call(
        _s5_scan_kernel,
        grid=(g // ng,),
        in_specs=[pl.BlockSpec((ng, nc, bsz, w), lambda gi: (gi, 0, 0, 0)),
                  pl.BlockSpec((ng, 1, w), lambda gi: (gi, 0, 0)),
                  pl.BlockSpec((ng, 1, w), lambda gi: (gi, 0, 0))],
        out_specs=pl.BlockSpec((ng, nc, bsz, w), lambda gi: (gi, 0, 0, 0)),
        out_shape=jax.ShapeDtypeStruct(xl.shape, F32),
        compiler_params=_cparams("parallel"),
        name="s5_scan",
    )(xl, la, lb)


def _s5_state_out_kernel(y_ref, xs_ref, q_ref, o_ref):
    o_ref[...] = y_ref[...] + _dot(xs_ref[...].astype(BF16), q_ref[...])


def _s5_state_out(y, xs, q_op, tr=1024):
    g, n, w = y.shape
    tr = min(tr, n)
    return pl.pallas_call(
        _s5_state_out_kernel,
        grid=(g, n // tr),
        in_specs=[pl.BlockSpec((None, tr, w), lambda gi, i: (gi, i, 0)),
                  pl.BlockSpec((None, tr, 2 * C_STATE), lambda gi, i: (gi, i, 0)),
                  pl.BlockSpec((None, 2 * C_STATE, w), lambda gi, i: (gi, 0, 0))],
        out_specs=pl.BlockSpec((None, tr, w), lambda gi, i: (gi, i, 0)),
        out_shape=jax.ShapeDtypeStruct((g, n, w), F32),
        compiler_params=_cparams("parallel", "parallel"),
        name="s5_state_out",
    )(y, xs, q_op)


def _s5_out_kernel(alpha, y_ref, h_ref, dskip_ref, w_ref, g_ref, b_ref, o_ref):
    d = h_ref.shape[1]
    h = h_ref[...]
    z = y_ref[...] + dskip_ref[...] * h
    z = 0.5 * z * (1.0 + jnp.tanh(math.sqrt(2.0 / math.pi) * (z + 0.044715 * (z * z * z))))
    zb = z.astype(BF16)
    val = _dot(zb, w_ref[:, 0:d])
    gate = _dot(zb, w_ref[:, d:2 * d])
    o_ref[...] = _layer_norm(alpha * h + val * _sigmoid(gate), g_ref[...], b_ref[...])


def _s5_out(y, h, d_skip, w_glu, g, b, alpha, tm=256):
    m, d = h.shape
    row = lambda i: (i, 0)
    fix = lambda i: (0, 0)
    return pl.pallas_call(
        functools.partial(_s5_out_kernel, alpha),
        grid=(m // tm,),
        in_specs=[pl.BlockSpec((tm, d), row), pl.BlockSpec((tm, d), row), pl.BlockSpec((1, d), fix),
                  pl.BlockSpec((d, 2 * d), fix), pl.BlockSpec((1, d), fix), pl.BlockSpec((1, d), fix)],
        out_specs=pl.BlockSpec((tm, d), row),
        out_shape=jax.ShapeDtypeStruct((m, d), F32),
        compiler_params=_cparams("parallel"),
        name="s5_out",
    )(y, h, d_skip, w_glu, g, b)


def _s5_mixer_ln(h, a_re, a_im, log_dt, b_re, b_im, c_re, c_im, d_skip, w_glu, ln_g, ln_b, alpha, bsz, seq):
    L = C_CHUNK
    nc = seq // L
    m_op, p_op, q_op, lam_l = _s5_operators(a_re, a_im, log_dt, b_re, b_im, c_re, c_im)
    u = h.astype(BF16).reshape(bsz, nc, L, C_GROUPS, C_GROUP_CH)
    u = u.transpose(3, 1, 0, 2, 4).reshape(C_GROUPS, nc * bsz, L * C_GROUP_CH)
    y_loc, x_loc = _s5_local(u, m_op, p_op)
    xs = _s5_scan(x_loc.reshape(C_GROUPS, nc, bsz, 2 * C_STATE), lam_l)
    y = _s5_state_out(y_loc, xs.reshape(C_GROUPS, nc * bsz, 2 * C_STATE), q_op)
    y = y.reshape(C_GROUPS, nc, bsz, L, C_GROUP_CH).transpose(2, 1, 3, 0, 4).reshape(bsz * seq, D_MODEL)
    return _s5_out(y, h, d_skip.reshape(1, -1), w_glu.astype(BF16), ln_g, ln_b, alpha)


def _cross_kernel(alpha, h_ref, kv_ref, wq_ref, wo_ref, g_ref, b_ref, wr_ref, br_ref,
                  o_ref, idx_ref, gate_ref):
    d = h_ref.shape[1]
    h = h_ref[...]
    q = (_dot(h.astype(BF16), wq_ref[...]) * (M_HEAD_DIM ** -0.5)).astype(BF16)
    heads = []
    for hd in range(M_HEADS):
        cols = slice(hd * M_HEAD_DIM, (hd + 1) * M_HEAD_DIM)
        s = _dot_nt(q[:, cols], kv_ref[:, cols])
        p = jnp.exp(s - jnp.max(s, axis=-1, keepdims=True))
        l = jnp.sum(p, axis=-1, keepdims=True)
        heads.append((_dot(p.astype(BF16), kv_ref[:, d + hd * M_HEAD_DIM:d + (hd + 1) * M_HEAD_DIM]) / l)
                     .astype(BF16))
    att = _dot(jnp.concatenate(heads, axis=-1), wo_ref[...])
    h2 = _layer_norm(alpha * h + att, g_ref[...], b_ref[...])
    o_ref[...] = h2
    logits = (_dot(h2.astype(BF16), wr_ref[...]) + br_ref[...]).T[0:N_EXPERTS, :]
    eidx = lax.broadcasted_iota(I32, logits.shape, 0)
    vals, idxs = [], []
    for _ in range(TOP_K):
        m = jnp.max(logits, axis=0, keepdims=True)
        pick = jnp.min(jnp.where(logits == m, eidx, N_EXPERTS), axis=0, keepdims=True)
        vals.append(m)
        idxs.append(pick)
        logits = jnp.where(eidx == pick, -jnp.inf, logits)
    es = [jnp.exp(v - vals[0]) for v in vals]
    tot = es[0] + es[1] + es[2] + es[3]
    idx_ref[...] = jnp.concatenate(idxs, axis=0)
    gate_ref[...] = jnp.concatenate([e / tot for e in es], axis=0)


def _cross_attention_router(h, mem_kv, wq, wo, g, b, wr, br, alpha, bsz, seq, tm=512):
    d = D_MODEL
    n_t = seq // tm
    mlen = mem_kv.shape[1]
    row = lambda bi, i: (bi * n_t + i, 0)
    fix = lambda bi, i: (0, 0)
    col = lambda bi, i: (bi, 0, i)
    return pl.pallas_call(
        functools.partial(_cross_kernel, alpha),
        grid=(bsz, n_t),
        in_specs=[pl.BlockSpec((tm, d), row),
                  pl.BlockSpec((None, mlen, 2 * d), lambda bi, i: (bi, 0, 0)),
                  pl.BlockSpec((d, d), fix), pl.BlockSpec((d, d), fix),
                  pl.BlockSpec((1, d), fix), pl.BlockSpec((1, d), fix),
                  pl.BlockSpec((d, LANES), fix), pl.BlockSpec((1, LANES), fix)],
        out_specs=[pl.BlockSpec((tm, d), row),
                   pl.BlockSpec((None, TOP_K, tm), col), pl.BlockSpec((None, TOP_K, tm), col)],
        out_shape=[jax.ShapeDtypeStruct((bsz * seq, d), F32),
                   jax.ShapeDtypeStruct((bsz, TOP_K, seq), I32),
                   jax.ShapeDtypeStruct((bsz, TOP_K, seq), F32)],
        compiler_params=_cparams("parallel", "parallel"),
        name="cross_attn_router",
    )(h, mem_kv, wq, wo, g, b, wr, br)


def _slot_kernel(idx_ref, slot_ref, count_ref, run_ref, base_ref):
    phase = pl.program_id(0)
    first = (pl.program_id(1) == 0) & (pl.program_id(2) == 0)

    @pl.when(first & (phase == 0))
    def _():
        run_ref[...] = jnp.zeros_like(run_ref)

    @pl.when(first & (phase == 1))
    def _():
        counts = run_ref[...]
        count_ref[...] = counts.astype(I32)
        padded = jnp.ceil(counts * (1.0 / MOE_ROWS)) * MOE_ROWS
        row = lax.broadcasted_iota(I32, padded.shape, 0)
        ends = padded
        sh = 1
        while sh < N_EXPERTS:
            ends = ends + jnp.where(row >= sh, pltpu.roll(ends, sh, axis=0), 0.0)
            sh *= 2
        base_ref[...] = ends - padded
        run_ref[...] = jnp.zeros_like(run_ref)

    tn = idx_ref.shape[1]
    idx = idx_ref[...]
    eidx = lax.broadcasted_iota(I32, (N_EXPERTS, tn), 0)
    hits = [eidx == idx[k:k + 1, :] for k in range(TOP_K)]
    onehot = jnp.zeros((N_EXPERTS, tn), F32)
    for hit in hits:
        onehot = onehot + jnp.where(hit, 1.0, 0.0)

    @pl.when(phase == 1)
    def _():
        earlier = lax.broadcasted_iota(I32, (tn, tn), 0) < lax.broadcasted_iota(I32, (tn, tn), 1)
        before = _dot(onehot.astype(BF16), jnp.where(earlier, 1.0, 0.0).astype(BF16))
        before = before + run_ref[:, 0:1] + base_ref[:, 0:1]
        slots = [jnp.sum(jnp.where(hit, before, 0.0), axis=0, keepdims=True) for hit in hits]
        slot_ref[...] = jnp.concatenate(slots, axis=0).astype(I32)

    run_ref[...] = run_ref[...] + jnp.sum(onehot, axis=1, keepdims=True)


def _expert_slots(idx, tn=512):
    bsz, _, seq = idx.shape
    return pl.pallas_call(
        _slot_kernel,
        grid=(2, bsz, seq // tn),
        in_specs=[pl.BlockSpec((None, TOP_K, tn), lambda p, bi, i: (bi, 0, i))],
        out_specs=[pl.BlockSpec((None, TOP_K, tn), lambda p, bi, i: (bi * p, 0, i * p)),
                   pl.BlockSpec((N_EXPERTS, LANES), lambda p, bi, i: (0, 0))],
        out_shape=[jax.ShapeDtypeStruct(idx.shape, I32), jax.ShapeDtypeStruct((N_EXPERTS, LANES), I32)],
        scratch_shapes=[pltpu.VMEM((N_EXPERTS, LANES), F32), pltpu.VMEM((N_EXPERTS, LANES), F32)],
        compiler_params=_cparams("arbitrary", "arbitrary", "arbitrary"),
        name="expert_slots",
    )(idx)


def _dispatch_kernel(tail_ref, slot_ref, x_ref, xs_ref, zero_ref, sem, zsem):
    tm = x_ref.shape[0]

    @pl.when(pl.program_id(0) == 0)
    def _():
        zero_ref[...] = jnp.zeros_like(zero_ref)

        def clear(e):
            row = pl.multiple_of(jnp.maximum(tail_ref[e], 0), MOE_ROWS)
            return pltpu.make_async_copy(zero_ref, xs_ref.at[pl.ds(row, MOE_ROWS)], zsem)

        for e in range(N_EXPERTS):
            @pl.when(tail_ref[e] >= 0)
            def _(e=e):
                clear(e).start()
        for e in range(N_EXPERTS):
            @pl.when(tail_ref[e] >= 0)
            def _(e=e):
                clear(e).wait()

    for k in range(TOP_K):
        def start(t, c, k=k):
            pltpu.make_async_copy(x_ref.at[pl.ds(t, 1)], xs_ref.at[pl.ds(slot_ref[0, k * tm + t], 1)],
                                  sem).start()
            return c

        lax.fori_loop(0, tm, start, 0, unroll=DMA_UNROLL)
    for k in range(TOP_K):
        pltpu.make_async_copy(x_ref, xs_ref.at[pl.ds(0, tm)], sem).wait()


def _dispatch(x, slots, tail_rows, n_rows, tm=512):
    t, d = x.shape
    grid_spec = pltpu.PrefetchScalarGridSpec(
        num_scalar_prefetch=1,
        grid=(t // tm,),
        in_specs=[pl.BlockSpec((None, 1, TOP_K * tm), lambda i, tail: (i, 0, 0), memory_space=pltpu.SMEM),
                  pl.BlockSpec((tm, d), lambda i, tail: (i, 0))],
        out_specs=pl.BlockSpec(memory_space=pl.ANY),
        scratch_shapes=[pltpu.VMEM((MOE_ROWS, d), x.dtype), pltpu.SemaphoreType.DMA(()),
                        pltpu.SemaphoreType.DMA(())],
    )
    return pl.pallas_call(
        _dispatch_kernel,
        grid_spec=grid_spec,
        out_shape=jax.ShapeDtypeStruct((n_rows, d), x.dtype),
        compiler_params=_cparams("arbitrary"),
        name="moe_dispatch",
    )(tail_rows, slots, x)


def _expert_ffn_kernel(be_ref, nu_ref, x_ref, w1_ref, b1_ref, w2_ref, b2_ref, y_ref, w1b_ref, w2b_ref):
    d = x_ref.shape[1]
    i = pl.program_id(0)

    @pl.when((i == 0) | (be_ref[i] != be_ref[jnp.maximum(i - 1, 0)]))
    def _():
        w1b_ref[...] = w1_ref[...].astype(BF16)
        w2b_ref[...] = w2_ref[...].astype(BF16)

    @pl.when(i < nu_ref[0])
    def _():
        xb = x_ref[...].astype(BF16)
        glu = jnp.minimum(_dot(xb, w1b_ref[:, 0:d]) + b1_ref[:, 0:d], SWIGLU_LIMIT)
        lin = jnp.clip(_dot(xb, w1b_ref[:, d:2 * d]) + b1_ref[:, d:2 * d], -SWIGLU_LIMIT, SWIGLU_LIMIT)
        act = glu * _sigmoid(SWIGLU_ALPHA * glu) * (lin + 1.0)
        y_ref[...] = _dot(act.astype(BF16), w2b_ref[...]) + b2_ref[...]

    @pl.when(i >= nu_ref[0])
    def _():
        y_ref[...] = jnp.zeros_like(y_ref)


def _expert_ffn(xs, block_e, n_used, w1, b1, w2, b2):
    n_rows, d = xs.shape
    nb = n_rows // MOE_ROWS
    wsel = lambda i, be, nu: (be[i], 0, 0)
    xsel = lambda i, be, nu: (jnp.minimum(i, nu[0] - 1), 0)
    grid_spec = pltpu.PrefetchScalarGridSpec(
        num_scalar_prefetch=2,
        grid=(nb,),
        in_specs=[pl.BlockSpec((MOE_ROWS, d), xsel),
                  pl.BlockSpec((None, d, 2 * d), wsel), pl.BlockSpec((None, 1, 2 * d), wsel),
                  pl.BlockSpec((None, d, d), wsel), pl.BlockSpec((None, 1, d), wsel)],
        out_specs=pl.BlockSpec((MOE_ROWS, d), lambda i, be, nu: (i, 0)),
        scratch_shapes=[pltpu.VMEM((d, 2 * d), BF16), pltpu.VMEM((d, d), BF16)],
    )
    return pl.pallas_call(
        _expert_ffn_kernel,
        grid_spec=grid_spec,
        out_shape=jax.ShapeDtypeStruct((n_rows, d), F32),
        compiler_params=_cparams("arbitrary"),
        name="expert_ffn",
    )(block_e, n_used, xs, w1, b1, w2, b2)


def _combine_kernel(alpha, slot_ref, nslot_ref, gate_ref, h_ref, g_ref, b_ref, ys_ref, o_ref, buf, sem):
    tm = h_ref.shape[0]
    i = pl.program_id(0)

    def gather(s_ref, half):
        for k in range(TOP_K):
            def start(t, c, k=k):
                pltpu.make_async_copy(ys_ref.at[pl.ds(s_ref[0, k * tm + t], 1)],
                                      buf.at[half, k, pl.ds(t, 1)], sem.at[half]).start()
                return c

            lax.fori_loop(0, tm, start, 0, unroll=DMA_UNROLL)

    @pl.when(i == 0)
    def _():
        gather(slot_ref, 0)

    @pl.when(i + 1 < pl.num_programs(0))
    def _():
        gather(nslot_ref, (i + 1) % 2)

    cur = i % 2
    for k in range(TOP_K):
        pltpu.make_async_copy(ys_ref.at[pl.ds(0, tm)], buf.at[cur, k], sem.at[cur]).wait()
    gate = gate_ref[...]
    y = buf[cur, 0] * gate[:, 0:1]
    for k in range(1, TOP_K):
        y = y + buf[cur, k] * gate[:, k:k + 1]
    o_ref[...] = _layer_norm(alpha * h_ref[...] + y, g_ref[...], b_ref[...])


def _combine_ln(ys, slots, gates, h, g, b, alpha, tm=256):
    t, d = h.shape
    n_t = t // tm
    row = lambda i: (i, 0)
    fix = lambda i: (0, 0)
    slot_block = (None, 1, TOP_K * tm)
    return pl.pallas_call(
        functools.partial(_combine_kernel, alpha),
        grid=(n_t,),
        in_specs=[pl.BlockSpec(slot_block, lambda i: (i, 0, 0), memory_space=pltpu.SMEM),
                  pl.BlockSpec(slot_block, lambda i: (jnp.minimum(i + 1, n_t - 1), 0, 0),
                               memory_space=pltpu.SMEM),
                  pl.BlockSpec((tm, TOP_K), row), pl.BlockSpec((tm, d), row),
                  pl.BlockSpec((1, d), fix), pl.BlockSpec((1, d), fix),
                  pl.BlockSpec(memory_space=pl.ANY)],
        out_specs=pl.BlockSpec((tm, d), row),
        out_shape=jax.ShapeDtypeStruct((t, d), F32),
        scratch_shapes=[pltpu.VMEM((2, TOP_K, tm, d), F32), pltpu.SemaphoreType.DMA((2,))],
        compiler_params=_cparams("arbitrary"),
        name="moe_combine_ln",
    )(slots, slots, gates, h, g, b, ys)


def _tile_slots(slot, tm):
    bsz, _, seq = slot.shape
    s = slot.reshape(bsz, TOP_K, seq // tm, tm).transpose(0, 2, 1, 3)
    return s.reshape(bsz * (seq // tm), 1, TOP_K * tm)


def _moe_ln(h2, idx, gates, w1, b1, w2, b2, ln_g, ln_b, alpha, tm_d=512, tm_c=256):
    t, d = h2.shape
    slot, counts = _expert_slots(idx)
    blocks_per_e = (counts[:, 0] + MOE_ROWS - 1) // MOE_ROWS
    block_end = jnp.cumsum(blocks_per_e)
    nb = t * TOP_K // MOE_ROWS + N_EXPERTS
    block_e = jnp.sum((block_end[None, :] <= jnp.arange(nb, dtype=I32)[:, None]).astype(I32), axis=1)
    block_e = jnp.minimum(block_e, N_EXPERTS - 1)
    n_used = block_end[-1:].astype(I32)
    tail_rows = jnp.where(blocks_per_e > 0, (block_end - 1) * MOE_ROWS, -1).astype(I32)
    xs = _dispatch(h2, _tile_slots(slot, tm_d), tail_rows, nb * MOE_ROWS, tm_d)
    ys = _expert_ffn(xs, block_e, n_used, w1, b1[:, None, :], w2, b2[:, None, :])
    gates_t = gates.transpose(0, 2, 1).reshape(t, TOP_K)
    return _combine_ln(ys, _tile_slots(slot, tm_c), gates_t, h2, ln_g[None], ln_b[None], alpha, tm_c)


def kernel(x, mem, positions, ln_g, ln_b, a_w_in, a_lower_bounds, a_norm_g, a_w_out, b_w_in, b_w_out,
           c_a_re, c_a_im, c_log_dt, c_b_re, c_b_im, c_c_re, c_c_im, c_d, c_w_glu, m_w_kv, m_w_q, m_w_o,
           r_w, r_b, e_w1, e_b1, e_w2, e_b2):
    bsz, seq, d = x.shape
    depth = ln_g.shape[0]
    alpha = (2 * depth) ** 0.25
    t = bsz * seq
    lb = jax.nn.softmax(a_lower_bounds.astype(F32), axis=0)
    lb = jnp.cumsum(lb, axis=0) - lb[0]
    mem_kv = _matmul(mem.reshape(-1, d), m_w_kv.astype(BF16), BF16, 512, 1024)
    mem_kv = mem_kv.reshape(bsz, mem.shape[1], 2 * d)
    h = x.reshape(t, d)
    for layer in range(depth):
        kind, j = layer % N_MIXERS, layer // N_MIXERS
        g1, b1 = ln_g[layer, 0][None], ln_b[layer, 0][None]
        if kind == 0:
            h = _hgrn2_mixer_ln(h, a_w_in[j], a_norm_g[j], a_w_out[j], lb[layer], g1, b1, alpha, bsz, seq)
        elif kind == 1:
            h = _dilated_mixer_ln(h, positions, b_w_in[j], b_w_out[j], g1, b1, alpha, bsz, seq)
        else:
            h = _s5_mixer_ln(h, c_a_re[j], c_a_im[j], c_log_dt[j], c_b_re[j], c_b_im[j], c_c_re[j],
                             c_c_im[j], c_d[j], c_w_glu[j], g1, b1, alpha, bsz, seq)
        wr = jnp.pad(r_w[layer], ((0, 0), (0, LANES - N_EXPERTS))).astype(BF16)
        br = jnp.pad(r_b[layer], (0, LANES - N_EXPERTS))[None]
        h, idx, gates = _cross_attention_router(h, mem_kv, m_w_q[layer].astype(BF16), m_w_o[layer].astype(BF16),
                                                ln_g[layer, 1][None], ln_b[layer, 1][None], wr, br,
                                                alpha, bsz, seq)
        h = _moe_ln(h, idx, gates, e_w1[layer], e_b1[layer], e_w2[layer], e_b2[layer],
                    ln_g[layer, 2], ln_b[layer, 2], alpha)
    return h.reshape(bsz, seq, d)
```

```python
import functools
import math

import jax
import jax.numpy as jnp
from jax import lax
from jax.experimental import pallas as pl
from jax.experimental.pallas import tpu as pltpu

F32 = jnp.float32
BF16 = jnp.bfloat16
I32 = jnp.int32

D_MODEL = 1024
N_MIXERS = 3
NORM_EPS = 1e-5
ROPE_THETA = 10000.0
MIN_FORGET = 1e-6
MASK_VALUE = -1e30

A_HEADS = 8
A_DIM = 128
A_CHUNK = 64
A_SUB = 8

B_GROUPS = ((128, 1), (512, 4), (2048, 16))
B_HEAD_DIM = 64
B_HEADS = 16
B_BLOCK = 128

C_GROUP_CH = 16
C_GROUPS = 64
C_STATE = 64
C_CHUNK = 16
C_TILE_GROUPS = 8

M_HEADS = 4
M_HEAD_DIM = 256

N_EXPERTS = 32
TOP_K = 4
SWIGLU_ALPHA = 1.702
SWIGLU_LIMIT = 7.0
MOE_ROWS = 512
DMA_UNROLL = 8

LANES = 128
VMEM_LIMIT = 48 * 1024 * 1024


def _cparams(*sem):
    return pltpu.CompilerParams(dimension_semantics=sem, vmem_limit_bytes=VMEM_LIMIT)


def _layer_norm(z, g, b):
    mu = jnp.mean(z, axis=-1, keepdims=True)
    zc = z - mu
    var = jnp.mean(zc * zc, axis=-1, keepdims=True)
    return zc * lax.rsqrt(var + NORM_EPS) * g + b


def _sigmoid(x):
    return 1.0 / (1.0 + jnp.exp(-x))


def _dot(a, b):
    return jnp.dot(a, b, preferred_element_type=F32)


def _dot_nt(a, b):
    return lax.dot_general(a, b, (((1,), (1,)), ((), ())), preferred_element_type=F32)


def _dot_tn(a, b):
    return lax.dot_general(a, b, (((0,), (0,)), ((), ())), preferred_element_type=F32)


def _matmul_kernel(x_ref, w_ref, o_ref):
    o_ref[...] = _dot(x_ref[...].astype(BF16), w_ref[...]).astype(o_ref.dtype)


def _matmul(x, w, out_dtype, tm, tn):
    m, k = x.shape
    n = w.shape[1]
    return pl.pallas_call(
        _matmul_kernel,
        grid=(n // tn, m // tm),
        in_specs=[pl.BlockSpec((tm, k), lambda j, i: (i, 0)),
                  pl.BlockSpec((k, tn), lambda j, i: (0, j))],
        out_specs=pl.BlockSpec((tm, tn), lambda j, i: (i, j)),
        out_shape=jax.ShapeDtypeStruct((m, n), out_dtype),
        compiler_params=_cparams("parallel", "parallel"),
        name="matmul",
    )(x, w)


def _proj_res_ln_kernel(alpha, a_ref, w_ref, h_ref, g_ref, b_ref, o_ref):
    mix = _dot(a_ref[...].astype(BF16), w_ref[...])
    o_ref[...] = _layer_norm(alpha * h_ref[...] + mix, g_ref[...], b_ref[...])


def _proj_res_ln(a, w, h, g, b, alpha, tm=512):
    m, d = h.shape
    k = a.shape[1]
    row = lambda i: (i, 0)
    fix = lambda i: (0, 0)
    return pl.pallas_call(
        functools.partial(_proj_res_ln_kernel, alpha),
        grid=(m // tm,),
        in_specs=[pl.BlockSpec((tm, k), row), pl.BlockSpec((k, d), fix),
                  pl.BlockSpec((tm, d), row), pl.BlockSpec((1, d), fix), pl.BlockSpec((1, d), fix)],
        out_specs=pl.BlockSpec((tm, d), row),
        out_shape=jax.ShapeDtypeStruct((m, d), F32),
        compiler_params=_cparams("parallel"),
        name="proj_res_ln",
    )(a, w, h, g, b)


def _hgrn_in_kernel(x_ref, w_ref, lb_ref, q_ref, k_ref, v_ref, lf_ref, g_ref):
    d = x_ref.shape[1]
    xb = x_ref[...].astype(BF16)
    q = _dot(xb, w_ref[:, 0:d])
    q_ref[...] = q * _sigmoid(q)
    f = _dot(xb, w_ref[:, d:2 * d])
    lb = lb_ref[...]
    fg = lb + (1.0 - lb) * _sigmoid(f)
    lf_ref[...] = jnp.log(jnp.maximum(fg, MIN_FORGET))
    k_ref[...] = 1.0 - fg
    v_ref[...] = _dot(xb, w_ref[:, 2 * d:3 * d])
    g = _dot(xb, w_ref[:, 3 * d:4 * d])
    g_ref[...] = g * _sigmoid(g)


def _hgrn_in(x, w, lb, tm=256):
    m, d = x.shape
    row = lambda i: (i, 0)
    fix = lambda i: (0, 0)
    out = jax.ShapeDtypeStruct((m, d), F32)
    return pl.pallas_call(
        _hgrn_in_kernel,
        grid=(m // tm,),
        in_specs=[pl.BlockSpec((tm, d), row), pl.BlockSpec((d, 4 * d), fix), pl.BlockSpec((1, d), fix)],
        out_specs=[pl.BlockSpec((tm, d), row)] * 5,
        out_shape=[out] * 5,
        compiler_params=_cparams("parallel"),
        name="hgrn_in",
    )(x, w, lb)


def _rows_from(b, picks, span):
    parts = [jnp.broadcast_to(b[p:p + 1, :], (span, b.shape[1])) for p in picks]
    return parts[0] if len(parts) == 1 else jnp.concatenate(parts, axis=0)


def _hgrn_chunk_kernel(q_ref, k_ref, v_ref, lf_ref, g_ref, ng_ref, o_ref, state_ref):
    c = A_CHUNK
    n_sub = q_ref.shape[0] // c

    @pl.when(pl.program_id(1) == 0)
    def _():
        state_ref[...] = jnp.zeros_like(state_ref)

    ti = lax.broadcasted_iota(I32, (c, c), 0)
    si = lax.broadcasted_iota(I32, (c, c), 1)
    row = lax.broadcasted_iota(I32, (c, A_DIM), 0)
    halves = []
    half = c // 2
    while half >= A_SUB:
        halves.append(half)
        half //= 2
    level_masks = [((ti // (2 * hf)) == (si // (2 * hf))) & ((ti % (2 * hf)) >= hf) & ((si % (2 * hf)) < hf)
                   for hf in halves]
    diag_mask = ((ti // A_SUB) == (si // A_SUB)) & (si <= ti)

    def chunk(ci, carry):
        r0 = pl.multiple_of(ci * c, c)
        for hd in range(A_HEADS):
            cols = slice(hd * A_DIM, (hd + 1) * A_DIM)
            q = q_ref[pl.ds(r0, c), cols]
            k = k_ref[pl.ds(r0, c), cols]
            v = v_ref[pl.ds(r0, c), cols]
            b = lf_ref[pl.ds(r0, c), cols]
            sh = 1
            while sh < c:
                b = b + jnp.where(row >= sh, pltpu.roll(b, sh, axis=0), 0.0)
                sh *= 2
            scores = jnp.zeros((c, c), F32)
            for hf, mask in zip(halves, level_masks):
                ref = _rows_from(b, [blk * 2 * hf + hf - 1 for blk in range(c // (2 * hf))], 2 * hf)
                qt = (q * jnp.exp(jnp.minimum(b - ref, 0.0))).astype(BF16)
                kt = (k * jnp.exp(jnp.minimum(ref - b, 0.0))).astype(BF16)
                scores = scores + jnp.where(mask, _dot_nt(qt, kt), 0.0)
            lo = _rows_from(b, [blk * A_SUB for blk in range(c // A_SUB)], A_SUB)
            hi = _rows_from(b, [blk * A_SUB + A_SUB - 1 for blk in range(c // A_SUB)], A_SUB)
            mid = 0.5 * (lo + hi)
            qt = (q * jnp.exp(b - mid)).astype(BF16)
            kt = (k * jnp.exp(mid - b)).astype(BF16)
            scores = scores + jnp.where(diag_mask, _dot_nt(qt, kt), 0.0)

            st = state_ref[hd]
            b_end = b[c - 1:c, :]
            o = _dot(scores.astype(BF16), v.astype(BF16))
            o = o + _dot_nt((q * jnp.exp(b)).astype(BF16), st.astype(BF16))
            ke = (k * jnp.exp(b_end - b)).astype(BF16)
            state_ref[hd] = st * jnp.exp(b_end) + _dot_tn(v.astype(BF16), ke)

            o = o * lax.rsqrt(jnp.mean(o * o, axis=-1, keepdims=True) + NORM_EPS) * ng_ref[:, cols]
            o_ref[pl.ds(r0, c), cols] = (o * g_ref[pl.ds(r0, c), cols]).astype(o_ref.dtype)
        return carry

    lax.fori_loop(0, n_sub, chunk, 0)


def _hgrn_chunk(q, k, v, lf, g, ng, bsz, seq, tc=256):
    d = q.shape[1]
    n_t = seq // tc
    row = lambda b, i: (b * n_t + i, 0)
    return pl.pallas_call(
        _hgrn_chunk_kernel,
        grid=(bsz, n_t),
        in_specs=[pl.BlockSpec((tc, d), row)] * 5 + [pl.BlockSpec((1, d), lambda b, i: (0, 0))],
        out_specs=pl.BlockSpec((tc, d), row),
        out_shape=jax.ShapeDtypeStruct(q.shape, BF16),
        scratch_shapes=[pltpu.VMEM((A_HEADS, A_DIM, A_DIM), F32)],
        compiler_params=_cparams("parallel", "arbitrary"),
        name="hgrn_chunk",
    )(q, k, v, lf, g, ng)


def _hgrn2_mixer_ln(h, w_in, norm_g, w_out, lb, ln_g, ln_b, alpha, bsz, seq):
    q, k, v, lf, g = _hgrn_in(h, w_in.astype(BF16), lb.reshape(1, -1))
    o = _hgrn_chunk(q, k, v, lf, g, norm_g.reshape(1, -1), bsz, seq)
    return _proj_res_ln(o, w_out.astype(BF16), h, ln_g, ln_b, alpha)


def _rope_table_kernel(pos_ref, cos_ref, sin_ref):
    half = B_HEAD_DIM // 2
    lane = lax.broadcasted_iota(I32, cos_ref.shape, 1)
    j = (lane % half).astype(F32)
    inv_freq = jnp.exp(j * (-math.log(ROPE_THETA) / half))
    ang = pos_ref[...].astype(F32) * inv_freq
    cos_ref[...] = jnp.cos(ang)
    sin_ref[...] = jnp.where(lane < 2 * half, -1.0, 1.0) * jnp.sin(ang)


def _rope_tables(pos_col, tm=1024):
    t = pos_col.shape[0]
    out = jax.ShapeDtypeStruct((t, LANES), F32)
    return pl.pallas_call(
        _rope_table_kernel,
        grid=(t // tm,),
        in_specs=[pl.BlockSpec((tm, 1), lambda i: (i, 0))],
        out_specs=[pl.BlockSpec((tm, LANES), lambda i: (i, 0))] * 2,
        out_shape=[out, out],
        compiler_params=_cparams("parallel"),
        name="rope_tables",
    )(pos_col)


def _dil_attn_kernel(reach, q_ref, kp_ref, kc_ref, vp_ref, vc_ref, cq_ref, sq_ref, cp_ref, sp_ref,
                     o_ref, lse_ref):
    nq = q_ref.shape[0]
    i = pl.program_id(2)
    scale = B_HEAD_DIM ** -0.5
    half_lanes = LANES // 2

    def rope(x, cos, sin):
        return x * cos + pltpu.roll(x, half_lanes, axis=1) * sin

    qi = lax.broadcasted_iota(I32, (nq, 2 * nq), 0)
    ki = lax.broadcasted_iota(I32, (nq, 2 * nq), 1)
    dist = qi + nq - ki
    valid = (dist >= 0) & (dist <= reach) & ((ki >= nq) | (i > 0))
    lane = lax.broadcasted_iota(I32, (1, LANES), 1)
    head_a = ((lane // (B_HEAD_DIM // 2)) % 2) == 0
    v_head_a = lane < B_HEAD_DIM
    cq, sq = cq_ref[...], sq_ref[...]
    ck = jnp.concatenate([cp_ref[...], cq], axis=0)
    sk = jnp.concatenate([sp_ref[...], sq], axis=0)
    for pr in range(B_HEADS // 2):
        cols = slice(pr * LANES, (pr + 1) * LANES)
        q = rope(q_ref[:, cols], cq, sq) * scale
        k = rope(jnp.concatenate([kp_ref[:, cols], kc_ref[:, cols]], axis=0), ck, sk).astype(BF16)
        v = jnp.concatenate([vp_ref[:, cols], vc_ref[:, cols]], axis=0).astype(BF16)
        outs, lses = [], []
        for sel in (head_a, ~head_a):
            s = _dot_nt(jnp.where(sel, q, 0.0).astype(BF16), k)
            s = jnp.where(valid, s, MASK_VALUE)
            m = jnp.max(s, axis=-1, keepdims=True)
            p = jnp.exp(s - m)
            l = jnp.sum(p, axis=-1, keepdims=True)
            outs.append(_dot(p.astype(BF16), v) / l)
            lses.append(m + jnp.log(l))
        o_ref[:, cols] = jnp.where(v_head_a, outs[0], outs[1])
        lse_ref[:, cols] = jnp.where(v_head_a, lses[0], lses[1])


def _residue_perm(tm, dilation):
    per = tm // dilation
    j = jnp.arange(tm)
    src = (j % per) * dilation + j // per
    return (src[:, None] == jnp.arange(tm)[None, :]).astype(BF16)


def _regroup_rows(p, x):
    hi = x.astype(BF16)
    rest = x - hi.astype(F32)
    mid = rest.astype(BF16)
    lo = (rest - mid.astype(F32)).astype(BF16)
    return _dot(p, hi) + _dot(p, mid) + _dot(p, lo)


def _dil_proj_kernel(x_ref, p_ref, w_ref, o_ref):
    xs = _dot(p_ref[...], x_ref[...].astype(BF16)).astype(BF16)
    o_ref[...] = _dot(xs, w_ref[...]).reshape(o_ref.shape)


def _dil_proj(h, w, dilation, bsz, seq, tm=512):
    d, n = w.shape
    n_t = seq // tm
    return pl.pallas_call(
        _dil_proj_kernel,
        grid=(bsz * n_t,),
        in_specs=[pl.BlockSpec((tm, d), lambda i: (i, 0)), pl.BlockSpec((tm, tm), lambda i: (0, 0)),
                  pl.BlockSpec((d, n), lambda i: (0, 0))],
        out_specs=pl.BlockSpec((None, dilation, tm // dilation, n), lambda i: (i // n_t, 0, i % n_t, 0)),
        out_shape=jax.ShapeDtypeStruct((bsz, dilation, seq // dilation, n), F32),
        compiler_params=_cparams("parallel"),
        name=f"dil_proj_{dilation}",
    )(h, _residue_perm(tm, dilation), w)


def _dil_attention(proj, cos, sin, dilation, reach, bsz, seq):
    d = D_MODEL
    sub = seq // dilation
    nb = sub // B_BLOCK
    blk = (None, None, B_BLOCK, d)
    tab = (None, None, B_BLOCK, LANES)
    prev = lambda i: jnp.maximum(i - 1, 0)
    in_specs = [
        pl.BlockSpec(blk, lambda b, r, i: (b, r, i, 0)),
        pl.BlockSpec(blk, lambda b, r, i: (b, r, prev(i), 1)),
        pl.BlockSpec(blk, lambda b, r, i: (b, r, i, 1)),
        pl.BlockSpec(blk, lambda b, r, i: (b, r, prev(i), 2)),
        pl.BlockSpec(blk, lambda b, r, i: (b, r, i, 2)),
        pl.BlockSpec(tab, lambda b, r, i: (b, r, i, 0)),
        pl.BlockSpec(tab, lambda b, r, i: (b, r, i, 0)),
        pl.BlockSpec(tab, lambda b, r, i: (b, r, prev(i), 0)),
        pl.BlockSpec(tab, lambda b, r, i: (b, r, prev(i), 0)),
    ]
    out_spec = pl.BlockSpec(blk, lambda b, r, i: (b, r, i, 0))
    out = jax.ShapeDtypeStruct((bsz, dilation, sub, d), F32)
    return pl.pallas_call(
        functools.partial(_dil_attn_kernel, reach),
        grid=(bsz, dilation, nb),
        in_specs=in_specs,
        out_specs=[out_spec, out_spec],
        out_shape=[out, out],
        compiler_params=_cparams("parallel", "parallel", "arbitrary"),
        name=f"dil_attn_{dilation}",
    )(proj, proj, proj, proj, proj, cos, sin, cos, sin)


def _dil_out_kernel(alpha, o0, o1, o2, l0, l1, l2, p1_ref, p2_ref, w_ref, h_ref, g_ref, b_ref, out_ref):
    tm, d = h_ref.shape
    o1v = _regroup_rows(p1_ref[...], o1[...].reshape(tm, d))
    l1v = _regroup_rows(p1_ref[...], l1[...].reshape(tm, d))
    o2v = _regroup_rows(p2_ref[...], o2[...].reshape(tm, d))
    l2v = _regroup_rows(p2_ref[...], l2[...].reshape(tm, d))
    l0v = l0[...]
    m = jnp.maximum(jnp.maximum(l0v, l1v), l2v)
    e0, e1, e2 = jnp.exp(l0v - m), jnp.exp(l1v - m), jnp.exp(l2v - m)
    o = (e0 * o0[...] + e1 * o1v + e2 * o2v) / (e0 + e1 + e2)
    mix = _dot(o.astype(BF16), w_ref[...])
    out_ref[...] = _layer_norm(alpha * h_ref[...] + mix, g_ref[...], b_ref[...])


def _dil_out(outs, lses, w, h, g, b, alpha, bsz, seq, tm=256):
    m, d = h.shape
    n_t = seq // tm
    row = lambda i: (i, 0)
    fix = lambda i: (0, 0)

    def grouped(a):
        dil = a.shape[1]
        return pl.BlockSpec((None, dil, tm // dil, d), lambda i: (i // n_t, 0, i % n_t, 0))

    perms = [_residue_perm(tm, a.shape[1]).T for a in outs[1:]]
    return pl.pallas_call(
        functools.partial(_dil_out_kernel, alpha),
        grid=(m // tm,),
        in_specs=[pl.BlockSpec((tm, d), row), grouped(outs[1]), grouped(outs[2]),
                  pl.BlockSpec((tm, d), row), grouped(lses[1]), grouped(lses[2]),
                  pl.BlockSpec((tm, tm), fix), pl.BlockSpec((tm, tm), fix),
                  pl.BlockSpec((d, d), fix), pl.BlockSpec((tm, d), row),
                  pl.BlockSpec((1, d), fix), pl.BlockSpec((1, d), fix)],
        out_specs=pl.BlockSpec((tm, d), row),
        out_shape=jax.ShapeDtypeStruct((m, d), F32),
        compiler_params=_cparams("parallel"),
        name="dil_out",
    )(outs[0].reshape(m, d), outs[1], outs[2], lses[0].reshape(m, d), lses[1], lses[2], *perms, w, h, g, b)


def _rope_col_perm():
    half = B_HEAD_DIM // 2
    idx = []
    for pr in range(B_HEADS // 2):
        a, b = 2 * pr * B_HEAD_DIM, (2 * pr + 1) * B_HEAD_DIM
        idx += list(range(a, a + half)) + list(range(b, b + half))
        idx += list(range(a + half, a + 2 * half)) + list(range(b + half, b + 2 * half))
    return jnp.asarray(idx, dtype=I32)


def _dilated_mixer_ln(h, positions, w_in, w_out, ln_g, ln_b, alpha, bsz, seq):
    d = D_MODEL
    perm = _rope_col_perm()
    w = w_in.reshape(d, len(B_GROUPS), 3, d)
    w = jnp.concatenate([w[:, :, 0:2, :][..., perm], w[:, :, 2:3, :]], axis=2)
    w = w.reshape(d, len(B_GROUPS), 3 * d).astype(BF16)
    outs, lses = [], []
    for gi, (window, dilation) in enumerate(B_GROUPS):
        sub = seq // dilation
        if dilation == 1:
            proj = _matmul(h, w[:, gi, :], F32, 512, 1024).reshape(bsz, 1, seq, 3 * d)
        else:
            proj = _dil_proj(h, w[:, gi, :], dilation, bsz, seq)
        pos =positions.reshape(bsz, sub, dilation).transpose(0, 2, 1).reshape(-1, 1)
        cos, sin = _rope_tables(pos)
        tab = (bsz, dilation, sub, LANES)
        o, l = _dil_attention(proj, cos.reshape(tab), sin.reshape(tab), dilation, window // dilation, bsz, seq)
        outs.append(o)
        lses.append(l)
    return _dil_out(outs, lses, w_out.astype(BF16), h, ln_g, ln_b, alpha, bsz, seq)


def _s5_operators(a_re, a_im, log_dt, b_re, b_im, c_re, c_im):
    hp = lax.Precision.HIGHEST
    L = C_CHUNK
    dt = jnp.exp(log_dt)[:, None]
    lam_re = jnp.exp(a_re * dt) * jnp.cos(a_im * dt)
    lam_im = jnp.exp(a_re * dt) * jnp.sin(a_im * dt)
    den = a_re * a_re + a_im * a_im
    fr = ((lam_re - 1.0) * a_re + lam_im * a_im) / den
    fi = (lam_im * a_re - (lam_re - 1.0) * a_im) / den
    bb_re = fr[..., None] * b_re - fi[..., None] * b_im
    bb_im = fr[..., None] * b_im + fi[..., None] * b_re
    tau = jnp.arange(L + 1, dtype=F32)[:, None, None]
    mag = jnp.exp(a_re * dt * tau)
    pw_re = mag * jnp.cos(a_im * dt * tau)
    pw_im = mag * jnp.sin(a_im * dt * tau)
    cl_re = c_re[None] * pw_re[:, :, None, :] - c_im[None] * pw_im[:, :, None, :]
    cl_im = c_re[None] * pw_im[:, :, None, :] + c_im[None] * pw_re[:, :, None, :]
    kern = (jnp.einsum('tgon,gni->tgoi', cl_re, bb_re, precision=hp)
            - jnp.einsum('tgon,gni->tgoi', cl_im, bb_im, precision=hp)).transpose(0, 1, 3, 2)
    t_idx = jnp.arange(L)
    rev = pw_re[L - 1 - t_idx], pw_im[L - 1 - t_idx]
    p_re = rev[0][..., None] * bb_re[None] - rev[1][..., None] * bb_im[None]
    p_im = rev[0][..., None] * bb_im[None] + rev[1][..., None] * bb_re[None]
    p_op = jnp.concatenate([p_re, p_im], axis=2).transpose(0, 1, 3, 2)
    q_op = jnp.concatenate([cl_re[1:], -cl_im[1:]], axis=3).transpose(0, 1, 3, 2)
    lam_l = jnp.concatenate([pw_re[L], pw_im[L]], axis=-1)
    return kern, p_op, q_op, lam_l


def _s5_tile_operators(kern, p_op, q_op, lam_l, n_chunks):
    gt = C_TILE_GROUPS
    nt = C_GROUPS // gt
    L, ch, ns = C_CHUNK, C_GROUP_CH, 2 * C_STATE
    r128 = jnp.arange(LANES)[:, None]
    rows_q = jnp.arange(gt * ns)[:, None]
    own_d = (r128 // ch == jnp.arange(LANES)[None, :] // ch).astype(F32)
    own_p = (r128 // ch == jnp.arange(gt * ns)[None, :] // ns).astype(F32)
    own_q = (rows_q // ns == jnp.arange(LANES)[None, :] // ch).astype(F32)
    lag_blk = jnp.tile(kern[:L].reshape(L, nt, LANES, ch), (1, 1, 1, gt)) * own_d
    zero = jnp.zeros_like(lag_blk[0])
    m_t = jnp.concatenate(
        [jnp.concatenate([lag_blk[t - s] if t >= s else zero for t in range(L)], axis=-1) for s in range(L)],
        axis=1)
    p_blk = jnp.tile(p_op.reshape(L, nt, LANES, ns), (1, 1, 1, gt)) * own_p
    p_t = p_blk.transpose(1, 0, 2, 3).reshape(nt, L * LANES, gt * ns)
    q_blk = jnp.tile(q_op.reshape(L, nt, gt * ns, ch), (1, 1, 1, gt)) * own_q
    q_t = jnp.concatenate([q_blk[t] for t in range(L)], axis=-1)
    m_t, p_t, q_t = m_t.astype(BF16), p_t.astype(BF16), q_t.astype(BF16)
    lr, li = lam_l[:, :C_STATE], lam_l[:, C_STATE:]
    las, lbs = [], []
    k = 1
    while k < n_chunks:
        las.append(jnp.concatenate([lr, lr], axis=-1).reshape(nt, gt * ns))
        lbs.append(jnp.concatenate([-li, li], axis=-1).reshape(nt, gt * ns))
        lr, li = lr * lr - li * li, 2.0 * lr * li
        k *= 2
    return m_t, p_t, q_t, jnp.stack(las, axis=1), jnp.stack(lbs, axis=1)


def _s5_local_kernel(u_ref, m_ref, p_ref, y_ref, x_ref):
    n = u_ref.shape[0] // C_CHUNK
    u = jnp.concatenate([u_ref[pl.ds(s, n, stride=C_CHUNK), :].astype(BF16) for s in range(C_CHUNK)], axis=1)
    y = _dot(u, m_ref[...])
    for t in range(C_CHUNK):
        y_ref[pl.ds(t, n, stride=C_CHUNK), :] = y[:, t * LANES:(t + 1) * LANES]
    x_ref[...] = _dot(u, p_ref[...])


def _s5_local(h, m_t, p_t, bsz, seq):
    nt, kf, ns = p_t.shape
    n = seq // C_CHUNK
    return pl.pallas_call(
        _s5_local_kernel,
        grid=(nt, bsz),
        in_specs=[pl.BlockSpec((seq, LANES), lambda j, b: (b, j)),
                  pl.BlockSpec((None, kf, kf), lambda j, b: (j, 0, 0)),
                  pl.BlockSpec((None, kf, ns), lambda j, b: (j, 0, 0))],
        out_specs=[pl.BlockSpec((seq, LANES), lambda j, b: (b, j)),
                   pl.BlockSpec((None, n, ns), lambda j, b: (j, b, 0))],
        out_shape=[jax.ShapeDtypeStruct(h.shape, F32), jax.ShapeDtypeStruct((nt, bsz * n, ns), F32)],
        compiler_params=_cparams("parallel", "parallel"),
        name="s5_local",
    )(h, m_t, p_t)


def _s5_state_kernel(y_ref, x_ref, q_ref, la_ref, lb_ref, o_ref):
    n = x_ref.shape[0]
    row = lax.broadcasted_iota(I32, (n, LANES), 0)
    cols = []
    for g in range(C_TILE_GROUPS):
        lanes = slice(g * LANES, (g + 1) * LANES)
        x = x_ref[:, lanes]
        k, sh = 0, 1
        while sh < n:
            v = jnp.where(row >= sh, pltpu.roll(x, sh, axis=0), 0.0)
            x = x + la_ref[k:k + 1, lanes] * v + lb_ref[k:k + 1, lanes] * pltpu.roll(v, C_STATE, axis=1)
            k += 1
            sh *= 2
        cols.append(jnp.where(row >= 1, pltpu.roll(x, 1, axis=0), 0.0).astype(BF16))
    yq = _dot(jnp.concatenate(cols, axis=1), q_ref[...])
    for t in range(C_CHUNK):
        rows = pl.ds(t, n, stride=C_CHUNK)
        o_ref[rows, :] = y_ref[rows, :] + yq[:, t * LANES:(t + 1) * LANES]


def _s5_state_out(y, x_loc, q_t, la, lb, bsz, seq):
    nt, ns, kf = q_t.shape
    n = seq // C_CHUNK
    nk = la.shape[1]
    return pl.pallas_call(
        _s5_state_kernel,
        grid=(nt, bsz),
        in_specs=[pl.BlockSpec((seq, LANES), lambda j, b: (b, j)),
                  pl.BlockSpec((None, n, ns), lambda j, b: (j, b, 0)),
                  pl.BlockSpec((None, ns, kf), lambda j, b: (j, 0, 0)),
                  pl.BlockSpec((None, nk, ns), lambda j, b: (j, 0, 0)),
                  pl.BlockSpec((None, nk, ns), lambda j, b: (j, 0, 0))],
        out_specs=pl.BlockSpec((seq, LANES), lambda j, b: (b, j)),
        out_shape=jax.ShapeDtypeStruct(y.shape, F32),
        compiler_params=_cparams("parallel", "parallel"),
        name="s5_state_out",
    )(y, x_loc, q_t, la, lb)


def _s5_out_kernel(alpha, y_ref, h_ref, dskip_ref, w_ref, g_ref, b_ref, o_ref):
    d = h_ref.shape[1]
    h = h_ref[...]
    z = y_ref[...] + dskip_ref[...] * h
    z = 0.5 * z * (1.0 + jnp.tanh(math.sqrt(2.0 / math.pi) * (z + 0.044715 * (z * z * z))))
    zb = z.astype(BF16)
    val = _dot(zb, w_ref[:, 0:d])
    gate = _dot(zb, w_ref[:, d:2 * d])
    o_ref[...] = _layer_norm(alpha * h + val * _sigmoid(gate), g_ref[...], b_ref[...])


def _s5_out(y, h, d_skip, w_glu, g, b, alpha, tm=256):
    m, d = h.shape
    row = lambda i: (i, 0)
    fix = lambda i: (0, 0)
    return pl.pallas_call(
        functools.partial(_s5_out_kernel, alpha),
        grid=(m // tm,),
        in_specs=[pl.BlockSpec((tm, d), row), pl.BlockSpec((tm, d), row), pl.BlockSpec((1, d), fix),
                  pl.BlockSpec((d, 2 * d), fix), pl.BlockSpec((1, d), fix), pl.BlockSpec((1, d), fix)],
        out_specs=pl.BlockSpec((tm, d), row),
        out_shape=jax.ShapeDtypeStruct((m, d), F32),
        compiler_params=_cparams("parallel"),
        name="s5_out",
    )(y, h, d_skip, w_glu, g, b)


def _s5_mixer_ln(h, a_re, a_im, log_dt, b_re, b_im, c_re, c_im, d_skip, w_glu, ln_g, ln_b, alpha, bsz, seq):
    ops = _s5_operators(a_re, a_im, log_dt, b_re, b_im, c_re, c_im)
    m_t, p_t, q_t, la, lb = _s5_tile_operators(*ops, seq // C_CHUNK)
    y_loc, x_loc = _s5_local(h, m_t, p_t, bsz, seq)
    y = _s5_state_out(y_loc, x_loc, q_t, la, lb, bsz, seq)
    return _s5_out(y, h, d_skip.reshape(1, -1), w_glu.astype(BF16), ln_g, ln_b, alpha)


def _cross_kernel(alpha, h_ref, kv_ref, wq_ref, wo_ref, g_ref, b_ref, wr_ref, br_ref,
                  o_ref, idx_ref, gate_ref):
    d = h_ref.shape[1]
    h = h_ref[...]
    q = (_dot(h.astype(BF16), wq_ref[...]) * (M_HEAD_DIM ** -0.5)).astype(BF16)
    heads = []
    for hd in range(M_HEADS):
        cols = slice(hd * M_HEAD_DIM, (hd + 1) * M_HEAD_DIM)
        s = _dot_nt(q[:, cols], kv_ref[:, cols])
        p = jnp.exp(s - jnp.max(s, axis=-1, keepdims=True))
        l = jnp.sum(p, axis=-1, keepdims=True)
        heads.append((_dot(p.astype(BF16), kv_ref[:, d + hd * M_HEAD_DIM:d + (hd + 1) * M_HEAD_DIM]) / l)
                     .astype(BF16))
    att = _dot(jnp.concatenate(heads, axis=-1), wo_ref[...])
    h2 = _layer_norm(alpha * h + att, g_ref[...], b_ref[...])
    o_ref[...] = h2
    logits = (_dot(h2.astype(BF16), wr_ref[...]) + br_ref[...]).T[0:N_EXPERTS, :]
    eidx = lax.broadcasted_iota(I32, logits.shape, 0)
    vals, idxs = [], []
    for _ in range(TOP_K):
        m = jnp.max(logits, axis=0, keepdims=True)
        pick = jnp.min(jnp.where(logits == m, eidx, N_EXPERTS), axis=0, keepdims=True)
        vals.append(m)
        idxs.append(pick)
        logits = jnp.where(eidx == pick, -jnp.inf, logits)
    es = [jnp.exp(v - vals[0]) for v in vals]
    tot = es[0] + es[1] + es[2] + es[3]
    idx_ref[...] = jnp.concatenate(idxs, axis=0)
    gate_ref[...] = jnp.concatenate([e / tot for e in es], axis=0)


def _cross_attention_router(h, mem_kv, wq, wo, g, b, wr, br, alpha, bsz, seq, tm=512):
    d = D_MODEL
    n_t = seq // tm
    mlen = mem_kv.shape[1]
    row = lambda bi, i: (bi * n_t + i, 0)
    fix = lambda bi, i: (0, 0)
    col = lambda bi, i: (bi, 0, i)
    return pl.pallas_call(
        functools.partial(_cross_kernel, alpha),
        grid=(bsz, n_t),
        in_specs=[pl.BlockSpec((tm, d), row),
                  pl.BlockSpec((None, mlen, 2 * d), lambda bi, i: (bi, 0, 0)),
                  pl.BlockSpec((d, d), fix), pl.BlockSpec((d, d), fix),
                  pl.BlockSpec((1, d), fix), pl.BlockSpec((1, d), fix),
                  pl.BlockSpec((d, LANES), fix), pl.BlockSpec((1, LANES), fix)],
        out_specs=[pl.BlockSpec((tm, d), row),
                   pl.BlockSpec((None, TOP_K, tm), col), pl.BlockSpec((None, TOP_K, tm), col)],
        out_shape=[jax.ShapeDtypeStruct((bsz * seq, d), F32),
                   jax.ShapeDtypeStruct((bsz, TOP_K, seq), I32),
                   jax.ShapeDtypeStruct((bsz, TOP_K, seq), F32)],
        compiler_params=_cparams("parallel", "parallel"),
        name="cross_attn_router",
    )(h, mem_kv, wq, wo, g, b, wr, br)


def _slot_kernel(idx_ref, slot_ref, count_ref, run_ref, base_ref):
    phase = pl.program_id(0)
    first = (pl.program_id(1) == 0) & (pl.program_id(2) == 0)

    @pl.when(first & (phase == 0))
    def _():
        run_ref[...] = jnp.zeros_like(run_ref)

    @pl.when(first & (phase == 1))
    def _():
        counts = run_ref[...]
        count_ref[...] = counts.astype(I32)
        padded = jnp.ceil(counts * (1.0 / MOE_ROWS)) * MOE_ROWS
        row = lax.broadcasted_iota(I32, padded.shape, 0)
        ends = padded
        sh = 1
        while sh < N_EXPERTS:
            ends = ends + jnp.where(row >= sh, pltpu.roll(ends, sh, axis=0), 0.0)
            sh *= 2
        base_ref[...] = ends - padded
        run_ref[...] = jnp.zeros_like(run_ref)

    tn = idx_ref.shape[1]
    idx = idx_ref[...]
    eidx = lax.broadcasted_iota(I32, (N_EXPERTS, tn), 0)
    hits = [eidx == idx[k:k + 1, :] for k in range(TOP_K)]
    onehot = jnp.zeros((N_EXPERTS, tn), F32)
    for hit in hits:
        onehot = onehot + jnp.where(hit, 1.0, 0.0)

    @pl.when(phase == 1)
    def _():
        earlier = lax.broadcasted_iota(I32, (tn, tn), 0) < lax.broadcasted_iota(I32, (tn, tn), 1)
        before = _dot(onehot.astype(BF16), jnp.where(earlier, 1.0, 0.0).astype(BF16))
        before = before + run_ref[:, 0:1] + base_ref[:, 0:1]
        slots = [jnp.sum(jnp.where(hit, before, 0.0), axis=0, keepdims=True) for hit in hits]
        slot_ref[...] = jnp.concatenate(slots, axis=0).astype(I32)

    run_ref[...] = run_ref[...] + jnp.sum(onehot, axis=1, keepdims=True)


def _expert_slots(idx, tn=512):
    bsz, _, seq = idx.shape
    return pl.pallas_call(
        _slot_kernel,
        grid=(2, bsz, seq // tn),
        in_specs=[pl.BlockSpec((None, TOP_K, tn), lambda p, bi, i: (bi, 0, i))],
        out_specs=[pl.BlockSpec((None, TOP_K, tn), lambda p, bi, i: (bi * p, 0, i * p)),
                   pl.BlockSpec((N_EXPERTS, LANES), lambda p, bi, i: (0, 0))],
        out_shape=[jax.ShapeDtypeStruct(idx.shape, I32), jax.ShapeDtypeStruct((N_EXPERTS, LANES), I32)],
        scratch_shapes=[pltpu.VMEM((N_EXPERTS, LANES), F32), pltpu.VMEM((N_EXPERTS, LANES), F32)],
        compiler_params=_cparams("arbitrary", "arbitrary", "arbitrary"),
        name="expert_slots",
    )(idx)


def _dispatch_kernel(tail_ref, slot_ref, x_ref, xs_ref, zero_ref, sem, zsem):
    tm = x_ref.shape[0]

    @pl.when(pl.program_id(0) == 0)
    def _():
        zero_ref[...] = jnp.zeros_like(zero_ref)

        def clear(e):
            row = pl.multiple_of(jnp.maximum(tail_ref[e], 0), MOE_ROWS)
            return pltpu.make_async_copy(zero_ref, xs_ref.at[pl.ds(row, MOE_ROWS)], zsem)

        for e in range(N_EXPERTS):
            @pl.when(tail_ref[e] >= 0)
            def _(e=e):
                clear(e).start()
        for e in range(N_EXPERTS):
            @pl.when(tail_ref[e] >= 0)
            def _(e=e):
                clear(e).wait()

    for k in range(TOP_K):
        def start(t, c, k=k):
            pltpu.make_async_copy(x_ref.at[pl.ds(t, 1)], xs_ref.at[pl.ds(slot_ref[0, k * tm + t], 1)],
                                  sem).start()
            return c

        lax.fori_loop(0, tm, start, 0, unroll=DMA_UNROLL)
    for k in range(TOP_K):
        pltpu.make_async_copy(x_ref, xs_ref.at[pl.ds(0, tm)], sem).wait()


def _dispatch(x, slots, tail_rows, n_rows, tm=512):
    t, d = x.shape
    grid_spec = pltpu.PrefetchScalarGridSpec(
        num_scalar_prefetch=1,
        grid=(t // tm,),
        in_specs=[pl.BlockSpec((None, 1, TOP_K * tm), lambda i, tail: (i, 0, 0), memory_space=pltpu.SMEM),
                  pl.BlockSpec((tm, d), lambda i, tail: (i, 0))],
        out_specs=pl.BlockSpec(memory_space=pl.ANY),
        scratch_shapes=[pltpu.VMEM((MOE_ROWS, d), x.dtype), pltpu.SemaphoreType.DMA(()),
                        pltpu.SemaphoreType.DMA(())],
    )
    return pl.pallas_call(
        _dispatch_kernel,
        grid_spec=grid_spec,
        out_shape=jax.ShapeDtypeStruct((n_rows, d), x.dtype),
        compiler_params=_cparams("arbitrary"),
        name="moe_dispatch",
    )(tail_rows, slots, x)


def _expert_ffn_kernel(be_ref, nu_ref, x_ref, w1_ref, b1_ref, w2_ref, b2_ref, y_ref, w1b_ref, w2b_ref):
    d = x_ref.shape[1]
    i = pl.program_id(0)

    @pl.when((i == 0) | (be_ref[i] != be_ref[jnp.maximum(i - 1, 0)]))
    def _():
        w1b_ref[...] = w1_ref[...].astype(BF16)
        w2b_ref[...] = w2_ref[...].astype(BF16)

    @pl.when(i < nu_ref[0])
    def _():
        xb = x_ref[...].astype(BF16)
        glu = jnp.minimum(_dot(xb, w1b_ref[:, 0:d]) + b1_ref[:, 0:d], SWIGLU_LIMIT)
        lin = jnp.clip(_dot(xb, w1b_ref[:, d:2 * d]) + b1_ref[:, d:2 * d], -SWIGLU_LIMIT, SWIGLU_LIMIT)
        act = glu * _sigmoid(SWIGLU_ALPHA * glu) * (lin + 1.0)
        y_ref[...] = _dot(act.astype(BF16), w2b_ref[...]) + b2_ref[...]

    @pl.when(i >= nu_ref[0])
    def _():
        y_ref[...] = jnp.zeros_like(y_ref)


def _expert_ffn(xs, block_e, n_used, w1_all, b1, w2_all, b2, layer):
    n_rows, d = xs.shape
    nb = n_rows // MOE_ROWS
    wsel = lambda i, be, nu: (be[i], 0, 0)
    wsel_l = lambda i, be, nu: (layer, be[i], 0, 0)
    xsel = lambda i, be, nu: (jnp.minimum(i, nu[0] - 1), 0)
    grid_spec = pltpu.PrefetchScalarGridSpec(
        num_scalar_prefetch=2,
        grid=(nb,),
        in_specs=[pl.BlockSpec((MOE_ROWS, d), xsel),
                  pl.BlockSpec((None, None, d, 2 * d), wsel_l), pl.BlockSpec((None, 1, 2 * d), wsel),
                  pl.BlockSpec((None, None, d, d), wsel_l), pl.BlockSpec((None, 1, d), wsel)],
        out_specs=pl.BlockSpec((MOE_ROWS, d), lambda i, be, nu: (i, 0)),
        scratch_shapes=[pltpu.VMEM((d, 2 * d), BF16), pltpu.VMEM((d, d), BF16)],
    )
    return pl.pallas_call(
        _expert_ffn_kernel,
        grid_spec=grid_spec,
        out_shape=jax.ShapeDtypeStruct((n_rows, d), F32),
        compiler_params=_cparams("arbitrary"),
        name="expert_ffn",
    )(block_e, n_used, xs, w1_all, b1, w2_all, b2)


def _combine_kernel(alpha, slot_ref, nslot_ref, gate_ref, h_ref, g_ref, b_ref, ys_ref, o_ref, buf, sem):
    tm = h_ref.shape[0]
    i = pl.program_id(0)

    def gather(s_ref, half):
        for k in range(TOP_K):
            def start(t, c, k=k):
                pltpu.make_async_copy(ys_ref.at[pl.ds(s_ref[0, k * tm + t], 1)],
                                      buf.at[half, k, pl.ds(t, 1)], sem.at[half]).start()
                return c

            lax.fori_loop(0, tm, start, 0, unroll=DMA_UNROLL)

    @pl.when(i == 0)
    def _():
        gather(slot_ref, 0)

    @pl.when(i + 1 < pl.num_programs(0))
    def _():
        gather(nslot_ref, (i + 1) % 2)

    cur = i % 2
    for k in range(TOP_K):
        pltpu.make_async_copy(ys_ref.at[pl.ds(0, tm)], buf.at[cur, k], sem.at[cur]).wait()
    gate = gate_ref[...]
    y = buf[cur, 0] * gate[:, 0:1]
    for k in range(1, TOP_K):
        y = y + buf[cur, k] * gate[:, k:k + 1]
    o_ref[...] = _layer_norm(alpha * h_ref[...] + y, g_ref[...], b_ref[...])


def _combine_ln(ys, slots, gates, h, g, b, alpha, tm=256):
    t, d = h.shape
    n_t = t // tm
    row = lambda i: (i, 0)
    fix = lambda i: (0, 0)
    slot_block = (None, 1, TOP_K * tm)
    return pl.pallas_call(
        functools.partial(_combine_kernel, alpha),
        grid=(n_t,),
        in_specs=[pl.BlockSpec(slot_block, lambda i: (i, 0, 0), memory_space=pltpu.SMEM),
                  pl.BlockSpec(slot_block, lambda i: (jnp.minimum(i + 1, n_t - 1), 0, 0),
                               memory_space=pltpu.SMEM),
                  pl.BlockSpec((tm, TOP_K), row), pl.BlockSpec((tm, d), row),
                  pl.BlockSpec((1, d), fix), pl.BlockSpec((1, d), fix),
                  pl.BlockSpec(memory_space=pl.ANY)],
        out_specs=pl.BlockSpec((tm, d), row),
        out_shape=jax.ShapeDtypeStruct((t, d), F32),
        scratch_shapes=[pltpu.VMEM((2, TOP_K, tm, d), F32), pltpu.SemaphoreType.DMA((2,))],
        compiler_params=_cparams("arbitrary"),
        name="moe_combine_ln",
    )(slots, slots, gates, h, g, b, ys)


def _tile_slots(slot, tm):
    bsz, _, seq = slot.shape
    s = slot.reshape(bsz, TOP_K, seq // tm, tm).transpose(0, 2, 1, 3)
    return s.reshape(bsz * (seq // tm), 1, TOP_K * tm)


def _moe_ln(h2, idx, gates, w1_all, b1, w2_all, b2, layer, ln_g, ln_b, alpha, tm_d=512, tm_c=256):
    t, d = h2.shape
    slot, counts = _expert_slots(idx)
    blocks_per_e = (counts[:, 0] + MOE_ROWS - 1) // MOE_ROWS
    block_end = jnp.cumsum(blocks_per_e)
    nb = t * TOP_K // MOE_ROWS + N_EXPERTS
    block_e = jnp.sum((block_end[None, :] <= jnp.arange(nb, dtype=I32)[:, None]).astype(I32), axis=1)
    block_e = jnp.minimum(block_e, N_EXPERTS - 1)
    n_used = block_end[-1:].astype(I32)
    tail_rows = jnp.where(blocks_per_e > 0, (block_end - 1) * MOE_ROWS, -1).astype(I32)
    xs = _dispatch(h2, _tile_slots(slot, tm_d), tail_rows, nb * MOE_ROWS, tm_d)
    ys = _expert_ffn(xs, block_e, n_used, w1_all, b1[:, None, :], w2_all, b2[:, None, :], layer)
    gates_t = gates.transpose(0, 2, 1).reshape(t, TOP_K)
    return _combine_ln(ys, _tile_slots(slot, tm_c), gates_t, h2, ln_g[None], ln_b[None], alpha, tm_c)


def kernel(x, mem, positions, ln_g, ln_b, a_w_in, a_lower_bounds, a_norm_g, a_w_out, b_w_in, b_w_out,
           c_a_re, c_a_im, c_log_dt, c_b_re, c_b_im, c_c_re, c_c_im, c_d, c_w_glu, m_w_kv, m_w_q, m_w_o,
           r_w, r_b, e_w1, e_b1, e_w2, e_b2):
    bsz, seq, d = x.shape
    depth = ln_g.shape[0]
    alpha = (2 * depth) ** 0.25
    t = bsz * seq
    lb = jax.nn.softmax(a_lower_bounds.astype(F32), axis=0)
    lb = jnp.cumsum(lb, axis=0) - lb[0]
    mem_kv = _matmul(mem.reshape(-1, d), m_w_kv.astype(BF16), BF16, 512, 1024)
    mem_kv = mem_kv.reshape(bsz, mem.shape[1], 2 * d)
    h = x.reshape(t, d)
    for layer in range(depth):
        kind, j = layer % N_MIXERS, layer // N_MIXERS
        g1, b1 = ln_g[layer, 0][None], ln_b[layer, 0][None]
        if kind == 0:
            h = _hgrn2_mixer_ln(h, a_w_in[j], a_norm_g[j], a_w_out[j], lb[layer], g1, b1, alpha, bsz, seq)
        elif kind == 1:
            h = _dilated_mixer_ln(h, positions, b_w_in[j], b_w_out[j], g1, b1, alpha, bsz, seq)
        else:
            h = _s5_mixer_ln(h, c_a_re[j], c_a_im[j], c_log_dt[j], c_b_re[j], c_b_im[j], c_c_re[j],
                             c_c_im[j], c_d[j], c_w_glu[j], g1, b1, alpha, bsz, seq)
        wr = jnp.pad(r_w[layer], ((0, 0), (0, LANES - N_EXPERTS))).astype(BF16)
        br = jnp.pad(r_b[layer], (0, LANES - N_EXPERTS))[None]
        h, idx, gates = _cross_attention_router(h, mem_kv, m_w_q[layer].astype(BF16), m_w_o[layer].astype(BF16),
                                                ln_g[layer, 1][None], ln_b[layer, 1][None], wr, br,
                                                alpha, bsz, seq)
        h = _moe_ln(h, idx, gates, e_w1, e_b1[layer], e_w2, e_b2[layer], layer,
                    ln_g[layer, 2], ln_b[layer, 2], alpha)
    return h.reshape(bsz, seq, d)
```

```python
import functools
import math

import jax
import jax.numpy as jnp
from jax import lax
from jax.experimental import pallas as pl
from jax.experimental.pallas import tpu as pltpu

F32 = jnp.float32
BF16 = jnp.bfloat16
I32 = jnp.int32

D_MODEL = 1024
N_MIXERS = 3
NORM_EPS = 1e-5
ROPE_THETA = 10000.0
MIN_FORGET = 1e-6
MASK_VALUE = -1e30

A_HEADS = 8
A_DIM = 128
A_CHUNK = 64
A_SUB = 8

B_GROUPS = ((128, 1), (512, 4), (2048, 16))
B_HEAD_DIM = 64
B_HEADS = 16
B_BLOCK = 128

C_GROUP_CH = 16
C_GROUPS = 64
C_STATE = 64
C_CHUNK = 16
C_TILE_GROUPS = 8

M_HEADS = 4
M_HEAD_DIM = 256

N_EXPERTS = 32
TOP_K = 4
SWIGLU_ALPHA = 1.702
SWIGLU_LIMIT = 7.0
MOE_ROWS = 512
SUBLANES = 8

LANES = 128
VMEM_LIMIT = 48 * 1024 * 1024


def _cparams(*sem):
    return pltpu.CompilerParams(dimension_semantics=sem, vmem_limit_bytes=VMEM_LIMIT)


def _layer_norm(z, g, b):
    mu = jnp.mean(z, axis=-1, keepdims=True)
    zc = z - mu
    var = jnp.mean(zc * zc, axis=-1, keepdims=True)
    return zc * lax.rsqrt(var + NORM_EPS) * g + b


def _sigmoid(x):
    return 1.0 / (1.0 + jnp.exp(-x))


def _dot(a, b):
    return jnp.dot(a, b, preferred_element_type=F32)


def _dot_nt(a, b):
    return lax.dot_general(a, b, (((1,), (1,)), ((), ())), preferred_element_type=F32)


def _dot_tn(a, b):
    return lax.dot_general(a, b, (((0,), (0,)), ((), ())), preferred_element_type=F32)


def _matmul_kernel(x_ref, w_ref, o_ref):
    o_ref[...] = _dot(x_ref[...].astype(BF16), w_ref[...]).astype(o_ref.dtype)


def _matmul(x, w, out_dtype, tm, tn):
    m, k = x.shape
    n = w.shape[1]
    return pl.pallas_call(
        _matmul_kernel,
        grid=(n // tn, m // tm),
        in_specs=[pl.BlockSpec((tm, k), lambda j, i: (i, 0)),
                  pl.BlockSpec((k, tn), lambda j, i: (0, j))],
        out_specs=pl.BlockSpec((tm, tn), lambda j, i: (i, j)),
        out_shape=jax.ShapeDtypeStruct((m, n), out_dtype),
        compiler_params=_cparams("parallel", "parallel"),
        name="matmul",
    )(x, w)


def _proj_res_ln_kernel(alpha, a_ref, w_ref, h_ref, g_ref, b_ref, o_ref):
    mix = _dot(a_ref[...].astype(BF16), w_ref[...])
    o_ref[...] = _layer_norm(alpha * h_ref[...] + mix, g_ref[...], b_ref[...])


def _proj_res_ln(a, w, h, g, b, alpha, tm=512):
    m, d = h.shape
    k = a.shape[1]
    row = lambda i: (i, 0)
    fix = lambda i: (0, 0)
    return pl.pallas_call(
        functools.partial(_proj_res_ln_kernel, alpha),
        grid=(m // tm,),
        in_specs=[pl.BlockSpec((tm, k), row), pl.BlockSpec((k, d), fix),
                  pl.BlockSpec((tm, d), row), pl.BlockSpec((1, d), fix), pl.BlockSpec((1, d), fix)],
        out_specs=pl.BlockSpec((tm, d), row),
        out_shape=jax.ShapeDtypeStruct((m, d), F32),
        compiler_params=_cparams("parallel"),
        name="proj_res_ln",
    )(a, w, h, g, b)


def _hgrn_in_kernel(x_ref, w_ref, lb_ref, q_ref, k_ref, v_ref, lf_ref, g_ref):
    d = x_ref.shape[1]
    xb = x_ref[...].astype(BF16)
    q = _dot(xb, w_ref[:, 0:d])
    q_ref[...] = q * _sigmoid(q)
    f = _dot(xb, w_ref[:, d:2 * d])
    lb = lb_ref[...]
    fg = lb + (1.0 - lb) * _sigmoid(f)
    lf_ref[...] = jnp.log(jnp.maximum(fg, MIN_FORGET))
    k_ref[...] = 1.0 - fg
    v_ref[...] = _dot(xb, w_ref[:, 2 * d:3 * d])
    g = _dot(xb, w_ref[:, 3 * d:4 * d])
    g_ref[...] = g * _sigmoid(g)


def _hgrn_in(x, w, lb, tm=256):
    m, d = x.shape
    row = lambda i: (i, 0)
    fix = lambda i: (0, 0)
    out = jax.ShapeDtypeStruct((m, d), F32)
    return pl.pallas_call(
        _hgrn_in_kernel,
        grid=(m // tm,),
        in_specs=[pl.BlockSpec((tm, d), row), pl.BlockSpec((d, 4 * d), fix), pl.BlockSpec((1, d), fix)],
        out_specs=[pl.BlockSpec((tm, d), row)] * 5,
        out_shape=[out] * 5,
        compiler_params=_cparams("parallel"),
        name="hgrn_in",
    )(x, w, lb)


def _rows_from(b, picks, span):
    parts = [jnp.broadcast_to(b[p:p + 1, :], (span, b.shape[1])) for p in picks]
    return parts[0] if len(parts) == 1 else jnp.concatenate(parts, axis=0)


def _hgrn_chunk_kernel(q_ref, k_ref, v_ref, lf_ref, g_ref, ng_ref, o_ref, state_ref):
    c = A_CHUNK
    n_sub = q_ref.shape[0] // c

    @pl.when(pl.program_id(1) == 0)
    def _():
        state_ref[...] = jnp.zeros_like(state_ref)

    ti = lax.broadcasted_iota(I32, (c, c), 0)
    si = lax.broadcasted_iota(I32, (c, c), 1)
    row = lax.broadcasted_iota(I32, (c, A_DIM), 0)
    halves = []
    half = c // 2
    while half >= A_SUB:
        halves.append(half)
        half //= 2
    level_masks = [((ti // (2 * hf)) == (si // (2 * hf))) & ((ti % (2 * hf)) >= hf) & ((si % (2 * hf)) < hf)
                   for hf in halves]
    diag_mask = ((ti // A_SUB) == (si // A_SUB)) & (si <= ti)

    def chunk(ci, carry):
        r0 = pl.multiple_of(ci * c, c)
        for hd in range(A_HEADS):
            cols = slice(hd * A_DIM, (hd + 1) * A_DIM)
            q = q_ref[pl.ds(r0, c), cols]
            k = k_ref[pl.ds(r0, c), cols]
            v = v_ref[pl.ds(r0, c), cols]
            b = lf_ref[pl.ds(r0, c), cols]
            sh = 1
            while sh < c:
                b = b + jnp.where(row >= sh, pltpu.roll(b, sh, axis=0), 0.0)
                sh *= 2
            scores = jnp.zeros((c, c), F32)
            for hf, mask in zip(halves, level_masks):
                ref = _rows_from(b, [blk * 2 * hf + hf - 1 for blk in range(c // (2 * hf))], 2 * hf)
                qt = (q * jnp.exp(jnp.minimum(b - ref, 0.0))).astype(BF16)
                kt = (k * jnp.exp(jnp.minimum(ref - b, 0.0))).astype(BF16)
                scores = scores + jnp.where(mask, _dot_nt(qt, kt), 0.0)
            lo = _rows_from(b, [blk * A_SUB for blk in range(c // A_SUB)], A_SUB)
            hi = _rows_from(b, [blk * A_SUB + A_SUB - 1 for blk in range(c // A_SUB)], A_SUB)
            mid = 0.5 * (lo + hi)
            qt = (q * jnp.exp(b - mid)).astype(BF16)
            kt = (k * jnp.exp(mid - b)).astype(BF16)
            scores = scores + jnp.where(diag_mask, _dot_nt(qt, kt), 0.0)

            st = state_ref[hd]
            b_end = b[c - 1:c, :]
            o = _dot(scores.astype(BF16), v.astype(BF16))
            o = o + _dot_nt((q * jnp.exp(b)).astype(BF16), st.astype(BF16))
            ke = (k * jnp.exp(b_end - b)).astype(BF16)
            state_ref[hd] = st * jnp.exp(b_end) + _dot_tn(v.astype(BF16), ke)

            o = o * lax.rsqrt(jnp.mean(o * o, axis=-1, keepdims=True) + NORM_EPS) * ng_ref[:, cols]
            o_ref[pl.ds(r0, c), cols] = (o * g_ref[pl.ds(r0, c), cols]).astype(o_ref.dtype)
        return carry

    lax.fori_loop(0, n_sub, chunk, 0)


def _hgrn_chunk(q, k, v, lf, g, ng, bsz, seq, tc=256):
    d = q.shape[1]
    n_t = seq // tc
    row = lambda b, i: (b * n_t + i, 0)
    return pl.pallas_call(
        _hgrn_chunk_kernel,
        grid=(bsz, n_t),
        in_specs=[pl.BlockSpec((tc, d), row)] * 5 + [pl.BlockSpec((1, d), lambda b, i: (0, 0))],
        out_specs=pl.BlockSpec((tc, d), row),
        out_shape=jax.ShapeDtypeStruct(q.shape, BF16),
        scratch_shapes=[pltpu.VMEM((A_HEADS, A_DIM, A_DIM), F32)],
        compiler_params=_cparams("parallel", "arbitrary"),
        name="hgrn_chunk",
    )(q, k, v, lf, g, ng)


def _hgrn2_mixer_ln(h, w_in, norm_g, w_out, lb, ln_g, ln_b, alpha, bsz, seq):
    q, k, v, lf, g = _hgrn_in(h, w_in.astype(BF16), lb.reshape(1, -1))
    o = _hgrn_chunk(q, k, v, lf, g, norm_g.reshape(1, -1), bsz, seq)
    return _proj_res_ln(o, w_out.astype(BF16), h, ln_g, ln_b, alpha)


def _rope_table_kernel(pos_ref, cos_ref, sin_ref):
    half = B_HEAD_DIM // 2
    lane = lax.broadcasted_iota(I32, cos_ref.shape, 1)
    j = (lane % half).astype(F32)
    inv_freq = jnp.exp(j * (-math.log(ROPE_THETA) / half))
    ang = pos_ref[...].astype(F32) * inv_freq
    cos_ref[...] = jnp.cos(ang)
    sin_ref[...] = jnp.where(lane < 2 * half, -1.0, 1.0) * jnp.sin(ang)


def _rope_tables(pos_col, tm=1024):
    t = pos_col.shape[0]
    out = jax.ShapeDtypeStruct((t, LANES), F32)
    return pl.pallas_call(
        _rope_table_kernel,
        grid=(t // tm,),
        in_specs=[pl.BlockSpec((tm, 1), lambda i: (i, 0))],
        out_specs=[pl.BlockSpec((tm, LANES), lambda i: (i, 0))] * 2,
        out_shape=[out, out],
        compiler_params=_cparams("parallel"),
        name="rope_tables",
    )(pos_col)


def _dil_attn_kernel(reach, q_ref, kp_ref, kc_ref, vp_ref, vc_ref, cq_ref, sq_ref, cp_ref, sp_ref,
                     o_ref, lse_ref):
    nq = q_ref.shape[0]
    i = pl.program_id(2)
    scale = B_HEAD_DIM ** -0.5
    half_lanes = LANES // 2

    def rope(x, cos, sin):
        return x * cos + pltpu.roll(x, half_lanes, axis=1) * sin

    qi = lax.broadcasted_iota(I32, (nq, 2 * nq), 0)
    ki = lax.broadcasted_iota(I32, (nq, 2 * nq), 1)
    dist = qi + nq - ki
    valid = (dist >= 0) & (dist <= reach) & ((ki >= nq) | (i > 0))
    lane = lax.broadcasted_iota(I32, (1, LANES), 1)
    head_a = ((lane // (B_HEAD_DIM // 2)) % 2) == 0
    v_head_a = lane < B_HEAD_DIM
    cq, sq = cq_ref[...], sq_ref[...]
    ck = jnp.concatenate([cp_ref[...], cq], axis=0)
    sk = jnp.concatenate([sp_ref[...], sq], axis=0)
    for pr in range(B_HEADS // 2):
        cols = slice(pr * LANES, (pr + 1) * LANES)
        q = rope(q_ref[:, cols], cq, sq) * scale
        k = rope(jnp.concatenate([kp_ref[:, cols], kc_ref[:, cols]], axis=0), ck, sk).astype(BF16)
        v = jnp.concatenate([vp_ref[:, cols], vc_ref[:, cols]], axis=0).astype(BF16)
        outs, lses = [], []
        for sel in (head_a, ~head_a):
            s = _dot_nt(jnp.where(sel, q, 0.0).astype(BF16), k)
            s = jnp.where(valid, s, MASK_VALUE)
            m = jnp.max(s, axis=-1, keepdims=True)
            p = jnp.exp(s - m)
            l = jnp.sum(p, axis=-1, keepdims=True)
            outs.append(_dot(p.astype(BF16), v) / l)
            lses.append(m + jnp.log(l))
        o_ref[:, cols] = jnp.where(v_head_a, outs[0], outs[1])
        lse_ref[:, cols] = jnp.where(v_head_a, lses[0], lses[1])


def _residue_perm(tm, dilation):
    per = tm // dilation
    j = jnp.arange(tm)
    src = (j % per) * dilation + j // per
    return (src[:, None] == jnp.arange(tm)[None, :]).astype(BF16)


def _regroup_rows(p, x):
    hi = x.astype(BF16)
    rest = x - hi.astype(F32)
    mid = rest.astype(BF16)
    lo = (rest - mid.astype(F32)).astype(BF16)
    return _dot(p, hi) + _dot(p, mid) + _dot(p, lo)


def _dil_proj_kernel(x_ref, p_ref, w_ref, o_ref):
    xs = _dot(p_ref[...], x_ref[...].astype(BF16)).astype(BF16)
    o_ref[...] = _dot(xs, w_ref[...]).reshape(o_ref.shape)


def _dil_proj(h, w, dilation, bsz, seq, tm=512):
    d, n = w.shape
    n_t = seq // tm
    return pl.pallas_call(
        _dil_proj_kernel,
        grid=(bsz * n_t,),
        in_specs=[pl.BlockSpec((tm, d), lambda i: (i, 0)), pl.BlockSpec((tm, tm), lambda i: (0, 0)),
                  pl.BlockSpec((d, n), lambda i: (0, 0))],
        out_specs=pl.BlockSpec((None, dilation, tm // dilation, n), lambda i: (i // n_t, 0, i % n_t, 0)),
        out_shape=jax.ShapeDtypeStruct((bsz, dilation, seq // dilation, n), F32),
        compiler_params=_cparams("parallel"),
        name=f"dil_proj_{dilation}",
    )(h, _residue_perm(tm, dilation), w)


def _dil_attention(proj, cos, sin, dilation, reach, bsz, seq):
    d = D_MODEL
    sub = seq // dilation
    nb = sub // B_BLOCK
    blk = (None, None, B_BLOCK, d)
    tab = (None, None, B_BLOCK, LANES)
    prev = lambda i: jnp.maximum(i - 1, 0)
    in_specs = [
        pl.BlockSpec(blk, lambda b, r, i: (b, r, i, 0)),
        pl.BlockSpec(blk, lambda b, r, i: (b, r, prev(i), 1)),
        pl.BlockSpec(blk, lambda b, r, i: (b, r, i, 1)),
        pl.BlockSpec(blk, lambda b, r, i: (b, r, prev(i), 2)),
        pl.BlockSpec(blk, lambda b, r, i: (b, r, i, 2)),
        pl.BlockSpec(tab, lambda b, r, i: (b, r, i, 0)),
        pl.BlockSpec(tab, lambda b, r, i: (b, r, i, 0)),
        pl.BlockSpec(tab, lambda b, r, i: (b, r, prev(i), 0)),
        pl.BlockSpec(tab, lambda b, r, i: (b, r, prev(i), 0)),
    ]
    out_spec = pl.BlockSpec(blk, lambda b, r, i: (b, r, i, 0))
    out = jax.ShapeDtypeStruct((bsz, dilation, sub, d), F32)
    return pl.pallas_call(
        functools.partial(_dil_attn_kernel, reach),
        grid=(bsz, dilation, nb),
        in_specs=in_specs,
        out_specs=[out_spec, out_spec],
        out_shape=[out, out],
        compiler_params=_cparams("parallel", "parallel", "arbitrary"),
        name=f"dil_attn_{dilation}",
    )(proj, proj, proj, proj, proj, cos, sin, cos, sin)


def _dil_out_kernel(alpha, o0, o1, o2, l0, l1, l2, p1_ref, p2_ref, w_ref, h_ref, g_ref, b_ref, out_ref):
    tm, d = h_ref.shape
    o1v = _regroup_rows(p1_ref[...], o1[...].reshape(tm, d))
    l1v = _regroup_rows(p1_ref[...], l1[...].reshape(tm, d))
    o2v = _regroup_rows(p2_ref[...], o2[...].reshape(tm, d))
    l2v = _regroup_rows(p2_ref[...], l2[...].reshape(tm, d))
    l0v = l0[...]
    m = jnp.maximum(jnp.maximum(l0v, l1v), l2v)
    e0, e1, e2 = jnp.exp(l0v - m), jnp.exp(l1v - m), jnp.exp(l2v - m)
    o = (e0 * o0[...] + e1 * o1v + e2 * o2v) / (e0 + e1 + e2)
    mix = _dot(o.astype(BF16), w_ref[...])
    out_ref[...] = _layer_norm(alpha * h_ref[...] + mix, g_ref[...], b_ref[...])


def _dil_out(outs, lses, w, h, g, b, alpha, bsz, seq, tm=256):
    m, d = h.shape
    n_t = seq // tm
    row = lambda i: (i, 0)
    fix = lambda i: (0, 0)

    def grouped(a):
        dil = a.shape[1]
        return pl.BlockSpec((None, dil, tm // dil, d), lambda i: (i // n_t, 0, i % n_t, 0))

    perms = [_residue_perm(tm, a.shape[1]).T for a in outs[1:]]
    return pl.pallas_call(
        functools.partial(_dil_out_kernel, alpha),
        grid=(m // tm,),
        in_specs=[pl.BlockSpec((tm, d), row), grouped(outs[1]), grouped(outs[2]),
                  pl.BlockSpec((tm, d), row), grouped(lses[1]), grouped(lses[2]),
                  pl.BlockSpec((tm, tm), fix), pl.BlockSpec((tm, tm), fix),
                  pl.BlockSpec((d, d), fix), pl.BlockSpec((tm, d), row),
                  pl.BlockSpec((1, d), fix), pl.BlockSpec((1, d), fix)],
        out_specs=pl.BlockSpec((tm, d), row),
        out_shape=jax.ShapeDtypeStruct((m, d), F32),
        compiler_params=_cparams("parallel"),
        name="dil_out",
    )(outs[0].reshape(m, d), outs[1], outs[2], lses[0].reshape(m, d), lses[1], lses[2], *perms, w, h, g, b)


def _rope_col_perm():
    half = B_HEAD_DIM // 2
    idx = []
    for pr in range(B_HEADS // 2):
        a, b = 2 * pr * B_HEAD_DIM, (2 * pr + 1) * B_HEAD_DIM
        idx += list(range(a, a + half)) + list(range(b, b + half))
        idx += list(range(a + half, a + 2 * half)) + list(range(b + half, b + 2 * half))
    return jnp.asarray(idx, dtype=I32)


def _dilated_mixer_ln(h, positions, w_in, w_out, ln_g, ln_b, alpha, bsz, seq):
    d = D_MODEL
    perm = _rope_col_perm()
    w = w_in.reshape(d, len(B_GROUPS), 3, d)
    w = jnp.concatenate([w[:, :, 0:2, :][..., perm], w[:, :, 2:3, :]], axis=2)
    w = w.reshape(d, len(B_GROUPS), 3 * d).astype(BF16)
    outs, lses = [], []
    for gi, (window, dilation) in enumerate(B_GROUPS):
        sub = seq // dilation
        if dilation == 1:
            proj = _matmul(h, w[:, gi, :], F32, 512, 1024).reshape(bsz, 1, seq, 3 * d)
        else:
            proj = _dil_proj(h, w[:, gi, :], dilation, bsz, seq)
        pos =positions.reshape(bsz, sub, dilation).transpose(0, 2, 1).reshape(-1, 1)
        cos, sin = _rope_tables(pos)
        tab = (bsz, dilation, sub, LANES)
        o, l = _dil_attention(proj, cos.reshape(tab), sin.reshape(tab), dilation, window // dilation, bsz, seq)
        outs.append(o)
        lses.append(l)
    return _dil_out(outs, lses, w_out.astype(BF16), h, ln_g, ln_b, alpha, bsz, seq)


def _s5_operators(a_re, a_im, log_dt, b_re, b_im, c_re, c_im):
    hp = lax.Precision.HIGHEST
    L = C_CHUNK
    dt = jnp.exp(log_dt)[:, None]
    lam_re = jnp.exp(a_re * dt) * jnp.cos(a_im * dt)
    lam_im = jnp.exp(a_re * dt) * jnp.sin(a_im * dt)
    den = a_re * a_re + a_im * a_im
    fr = ((lam_re - 1.0) * a_re + lam_im * a_im) / den
    fi = (lam_im * a_re - (lam_re - 1.0) * a_im) / den
    bb_re = fr[..., None] * b_re - fi[..., None] * b_im
    bb_im = fr[..., None] * b_im + fi[..., None] * b_re
    tau = jnp.arange(L + 1, dtype=F32)[:, None, None]
    mag = jnp.exp(a_re * dt * tau)
    pw_re = mag * jnp.cos(a_im * dt * tau)
    pw_im = mag * jnp.sin(a_im * dt * tau)
    cl_re = c_re[None] * pw_re[:, :, None, :] - c_im[None] * pw_im[:, :, None, :]
    cl_im = c_re[None] * pw_im[:, :, None, :] + c_im[None] * pw_re[:, :, None, :]
    kern = (jnp.einsum('tgon,gni->tgoi', cl_re, bb_re, precision=hp)
            - jnp.einsum('tgon,gni->tgoi', cl_im, bb_im, precision=hp)).transpose(0, 1, 3, 2)
    t_idx = jnp.arange(L)
    rev = pw_re[L - 1 - t_idx], pw_im[L - 1 - t_idx]
    p_re = rev[0][..., None] * bb_re[None] - rev[1][..., None] * bb_im[None]
    p_im = rev[0][..., None] * bb_im[None] + rev[1][..., None] * bb_re[None]
    p_op = jnp.concatenate([p_re, p_im], axis=2).transpose(0, 1, 3, 2)
    q_op = jnp.concatenate([cl_re[1:], -cl_im[1:]], axis=3).transpose(0, 1, 3, 2)
    lam_l = jnp.concatenate([pw_re[L], pw_im[L]], axis=-1)
    return kern, p_op, q_op, lam_l


def _s5_tile_operators(kern, p_op, q_op, lam_l, n_chunks):
    gt = C_TILE_GROUPS
    nt = C_GROUPS // gt
    L, ch, ns = C_CHUNK, C_GROUP_CH, 2 * C_STATE
    r128 = jnp.arange(LANES)[:, None]
    rows_q = jnp.arange(gt * ns)[:, None]
    own_d = (r128 // ch == jnp.arange(LANES)[None, :] // ch).astype(F32)
    own_p = (r128 // ch == jnp.arange(gt * ns)[None, :] // ns).astype(F32)
    own_q = (rows_q // ns == jnp.arange(LANES)[None, :] // ch).astype(F32)
    lag_blk = jnp.tile(kern[:L].reshape(L, nt, LANES, ch), (1, 1, 1, gt)) * own_d
    zero = jnp.zeros_like(lag_blk[0])
    m_t = jnp.concatenate(
        [jnp.concatenate([lag_blk[t - s] if t >= s else zero for t in range(L)], axis=-1) for s in range(L)],
        axis=1)
    p_blk = jnp.tile(p_op.reshape(L, nt, LANES, ns), (1, 1, 1, gt)) * own_p
    p_t = p_blk.transpose(1, 0, 2, 3).reshape(nt, L * LANES, gt * ns)
    q_blk = jnp.tile(q_op.reshape(L, nt, gt * ns, ch), (1, 1, 1, gt)) * own_q
    q_t = jnp.concatenate([q_blk[t] for t in range(L)], axis=-1)
    m_t, p_t, q_t = m_t.astype(BF16), p_t.astype(BF16), q_t.astype(BF16)
    lr, li = lam_l[:, :C_STATE], lam_l[:, C_STATE:]
    las, lbs = [], []
    k = 1
    while k < n_chunks:
        las.append(jnp.concatenate([lr, lr], axis=-1).reshape(nt, gt * ns))
        lbs.append(jnp.concatenate([-li, li], axis=-1).reshape(nt, gt * ns))
        lr, li = lr * lr - li * li, 2.0 * lr * li
        k *= 2
    return m_t, p_t, q_t, jnp.stack(las, axis=1), jnp.stack(lbs, axis=1)


def _s5_local_kernel(u_ref, m_ref, p_ref, y_ref, x_ref):
    n = u_ref.shape[0] // C_CHUNK
    u = jnp.concatenate([u_ref[pl.ds(s, n, stride=C_CHUNK), :].astype(BF16) for s in range(C_CHUNK)], axis=1)
    y = _dot(u, m_ref[...])
    for t in range(C_CHUNK):
        y_ref[pl.ds(t, n, stride=C_CHUNK), :] = y[:, t * LANES:(t + 1) * LANES]
    x_ref[...] = _dot(u, p_ref[...])


def _s5_local(h, m_t, p_t, bsz, seq):
    nt, kf, ns = p_t.shape
    n = seq // C_CHUNK
    return pl.pallas_call(
        _s5_local_kernel,
        grid=(nt, bsz),
        in_specs=[pl.BlockSpec((seq, LANES), lambda j, b: (b, j)),
                  pl.BlockSpec((None, kf, kf), lambda j, b: (j, 0, 0)),
                  pl.BlockSpec((None, kf, ns), lambda j, b: (j, 0, 0))],
        out_specs=[pl.BlockSpec((seq, LANES), lambda j, b: (b, j)),
                   pl.BlockSpec((None, n, ns), lambda j, b: (j, b, 0))],
        out_shape=[jax.ShapeDtypeStruct(h.shape, F32), jax.ShapeDtypeStruct((nt, bsz * n, ns), F32)],
        compiler_params=_cparams("parallel", "parallel"),
        name="s5_local",
    )(h, m_t, p_t)


def _s5_state_kernel(y_ref, x_ref, q_ref, la_ref, lb_ref, o_ref):
    n = x_ref.shape[0]
    row = lax.broadcasted_iota(I32, (n, LANES), 0)
    cols = []
    for g in range(C_TILE_GROUPS):
        lanes = slice(g * LANES, (g + 1) * LANES)
        x = x_ref[:, lanes]
        k, sh = 0, 1
        while sh < n:
            v = jnp.where(row >= sh, pltpu.roll(x, sh, axis=0), 0.0)
            x = x + la_ref[k:k + 1, lanes] * v + lb_ref[k:k + 1, lanes] * pltpu.roll(v, C_STATE, axis=1)
            k += 1
            sh *= 2
        cols.append(jnp.where(row >= 1, pltpu.roll(x, 1, axis=0), 0.0).astype(BF16))
    yq = _dot(jnp.concatenate(cols, axis=1), q_ref[...])
    for t in range(C_CHUNK):
        rows = pl.ds(t, n, stride=C_CHUNK)
        o_ref[rows, :] = y_ref[rows, :] + yq[:, t * LANES:(t + 1) * LANES]


def _s5_state_out(y, x_loc, q_t, la, lb, bsz, seq):
    nt, ns, kf = q_t.shape
    n = seq // C_CHUNK
    nk = la.shape[1]
    return pl.pallas_call(
        _s5_state_kernel,
        grid=(nt, bsz),
        in_specs=[pl.BlockSpec((seq, LANES), lambda j, b: (b, j)),
                  pl.BlockSpec((None, n, ns), lambda j, b: (j, b, 0)),
                  pl.BlockSpec((None, ns, kf), lambda j, b: (j, 0, 0)),
                  pl.BlockSpec((None, nk, ns), lambda j, b: (j, 0, 0)),
                  pl.BlockSpec((None, nk, ns), lambda j, b: (j, 0, 0))],
        out_specs=pl.BlockSpec((seq, LANES), lambda j, b: (b, j)),
        out_shape=jax.ShapeDtypeStruct(y.shape, F32),
        compiler_params=_cparams("parallel", "parallel"),
        name="s5_state_out",
    )(y, x_loc, q_t, la, lb)


def _s5_out_kernel(alpha, y_ref, h_ref, dskip_ref, w_ref, g_ref, b_ref, o_ref):
    d = h_ref.shape[1]
    h = h_ref[...]
    z = y_ref[...] + dskip_ref[...] * h
    z = 0.5 * z * (1.0 + jnp.tanh(math.sqrt(2.0 / math.pi) * (z + 0.044715 * (z * z * z))))
    zb = z.astype(BF16)
    val = _dot(zb, w_ref[:, 0:d])
    gate = _dot(zb, w_ref[:, d:2 * d])
    o_ref[...] = _layer_norm(alpha * h + val * _sigmoid(gate), g_ref[...], b_ref[...])


def _s5_out(y, h, d_skip, w_glu, g, b, alpha, tm=256):
    m, d = h.shape
    row = lambda i: (i, 0)
    fix = lambda i: (0, 0)
    return pl.pallas_call(
        functools.partial(_s5_out_kernel, alpha),
        grid=(m // tm,),
        in_specs=[pl.BlockSpec((tm, d), row), pl.BlockSpec((tm, d), row), pl.BlockSpec((1, d), fix),
                  pl.BlockSpec((d, 2 * d), fix), pl.BlockSpec((1, d), fix), pl.BlockSpec((1, d), fix)],
        out_specs=pl.BlockSpec((tm, d), row),
        out_shape=jax.ShapeDtypeStruct((m, d), F32),
        compiler_params=_cparams("parallel"),
        name="s5_out",
    )(y, h, d_skip, w_glu, g, b)


def _s5_mixer_ln(h, a_re, a_im, log_dt, b_re, b_im, c_re, c_im, d_skip, w_glu, ln_g, ln_b, alpha, bsz, seq):
    ops = _s5_operators(a_re, a_im, log_dt, b_re, b_im, c_re, c_im)
    m_t, p_t, q_t, la, lb = _s5_tile_operators(*ops, seq // C_CHUNK)
    y_loc, x_loc = _s5_local(h, m_t, p_t, bsz, seq)
    y = _s5_state_out(y_loc, x_loc, q_t, la, lb, bsz, seq)
    return _s5_out(y, h, d_skip.reshape(1, -1), w_glu.astype(BF16), ln_g, ln_b, alpha)


def _cross_kernel(alpha, h_ref, kv_ref, wq_ref, wo_ref, g_ref, b_ref, wr_ref, br_ref,
                  o_ref, idx_ref, gate_ref):
    d = h_ref.shape[1]
    h = h_ref[...]
    q = (_dot(h.astype(BF16), wq_ref[...]) * (M_HEAD_DIM ** -0.5)).astype(BF16)
    heads = []
    for hd in range(M_HEADS):
        cols = slice(hd * M_HEAD_DIM, (hd + 1) * M_HEAD_DIM)
        s = _dot_nt(q[:, cols], kv_ref[:, cols])
        p = jnp.exp(s - jnp.max(s, axis=-1, keepdims=True))
        l = jnp.sum(p, axis=-1, keepdims=True)
        heads.append((_dot(p.astype(BF16), kv_ref[:, d + hd * M_HEAD_DIM:d + (hd + 1) * M_HEAD_DIM]) / l)
                     .astype(BF16))
    att = _dot(jnp.concatenate(heads, axis=-1), wo_ref[...])
    h2 = _layer_norm(alpha * h + att, g_ref[...], b_ref[...])
    o_ref[...] = h2
    logits = (_dot(h2.astype(BF16), wr_ref[...]) + br_ref[...]).T[0:N_EXPERTS, :]
    eidx = lax.broadcasted_iota(I32, logits.shape, 0)
    vals, idxs = [], []
    for _ in range(TOP_K):
        m = jnp.max(logits, axis=0, keepdims=True)
        pick = jnp.min(jnp.where(logits == m, eidx, N_EXPERTS), axis=0, keepdims=True)
        vals.append(m)
        idxs.append(pick)
        logits = jnp.where(eidx == pick, -jnp.inf, logits)
    es = [jnp.exp(v - vals[0]) for v in vals]
    tot = es[0] + es[1] + es[2] + es[3]
    idx_ref[...] = jnp.concatenate(idxs, axis=0)
    gate_ref[...] = jnp.concatenate([e / tot for e in es], axis=0)


def _cross_attention_router(h, mem_kv, wq, wo, g, b, wr, br, alpha, bsz, seq, tm=512):
    d = D_MODEL
    n_t = seq // tm
    mlen = mem_kv.shape[1]
    row = lambda bi, i: (bi * n_t + i, 0)
    fix = lambda bi, i: (0, 0)
    col = lambda bi, i: (bi, 0, i)
    return pl.pallas_call(
        functools.partial(_cross_kernel, alpha),
        grid=(bsz, n_t),
        in_specs=[pl.BlockSpec((tm, d), row),
                  pl.BlockSpec((None, mlen, 2 * d), lambda bi, i: (bi, 0, 0)),
                  pl.BlockSpec((d, d), fix), pl.BlockSpec((d, d), fix),
                  pl.BlockSpec((1, d), fix), pl.BlockSpec((1, d), fix),
                  pl.BlockSpec((d, LANES), fix), pl.BlockSpec((1, LANES), fix)],
        out_specs=[pl.BlockSpec((tm, d), row),
                   pl.BlockSpec((None, TOP_K, tm), col), pl.BlockSpec((None, TOP_K, tm), col)],
        out_shape=[jax.ShapeDtypeStruct((bsz * seq, d), F32),
                   jax.ShapeDtypeStruct((bsz, TOP_K, seq), I32),
                   jax.ShapeDtypeStruct((bsz, TOP_K, seq), F32)],
        compiler_params=_cparams("parallel", "parallel"),
        name="cross_attn_router",
    )(h, mem_kv, wq, wo, g, b, wr, br)


def _slot_kernel(idx_ref, slot_ref, count_ref, run_ref, base_ref):
    phase = pl.program_id(0)
    first = (pl.program_id(1) == 0) & (pl.program_id(2) == 0)

    @pl.when(first & (phase == 0))
    def _():
        run_ref[...] = jnp.zeros_like(run_ref)

    @pl.when(first & (phase == 1))
    def _():
        counts = run_ref[...]
        count_ref[...] = counts.astype(I32)
        padded = jnp.ceil(counts * (1.0 / MOE_ROWS)) * MOE_ROWS
        row = lax.broadcasted_iota(I32, padded.shape, 0)
        ends = padded
        sh = 1
        while sh < N_EXPERTS:
            ends = ends + jnp.where(row >= sh, pltpu.roll(ends, sh, axis=0), 0.0)
            sh *= 2
        base_ref[...] = ends - padded
        run_ref[...] = jnp.zeros_like(run_ref)

    tn = idx_ref.shape[1]
    idx = idx_ref[...]
    eidx = lax.broadcasted_iota(I32, (N_EXPERTS, tn), 0)
    hits = [eidx == idx[k:k + 1, :] for k in range(TOP_K)]
    onehot = jnp.zeros((N_EXPERTS, tn), F32)
    for hit in hits:
        onehot = onehot + jnp.where(hit, 1.0, 0.0)

    @pl.when(phase == 1)
    def _():
        earlier = lax.broadcasted_iota(I32, (tn, tn), 0) < lax.broadcasted_iota(I32, (tn, tn), 1)
        before = _dot(onehot.astype(BF16), jnp.where(earlier, 1.0, 0.0).astype(BF16))
        before = before + run_ref[:, 0:1] + base_ref[:, 0:1]
        slots = [jnp.sum(jnp.where(hit, before, 0.0), axis=0, keepdims=True) for hit in hits]
        slot_ref[...] = jnp.concatenate(slots, axis=0).astype(I32)

    run_ref[...] = run_ref[...] + jnp.sum(onehot, axis=1, keepdims=True)


def _expert_slots(idx, tn=512):
    bsz, _, seq = idx.shape
    return pl.pallas_call(
        _slot_kernel,
        grid=(2, bsz, seq // tn),
        in_specs=[pl.BlockSpec((None, TOP_K, tn), lambda p, bi, i: (bi, 0, i))],
        out_specs=[pl.BlockSpec((None, TOP_K, tn), lambda p, bi, i: (bi * p, 0, i * p)),
                   pl.BlockSpec((N_EXPERTS, LANES), lambda p, bi, i: (0, 0))],
        out_shape=[jax.ShapeDtypeStruct(idx.shape, I32), jax.ShapeDtypeStruct((N_EXPERTS, LANES), I32)],
        scratch_shapes=[pltpu.VMEM((N_EXPERTS, LANES), F32), pltpu.VMEM((N_EXPERTS, LANES), F32)],
        compiler_params=_cparams("arbitrary", "arbitrary", "arbitrary"),
        name="expert_slots",
    )(idx)


def _dispatch_kernel(tail_ref, slot_ref, x_ref, xs_ref, zero_ref, sem, zsem):
    tm = x_ref.shape[0] * SUBLANES

    @pl.when(pl.program_id(0) == 0)
    def _():
        zero_ref[...] = jnp.zeros_like(zero_ref)

        def clear(e):
            row = pl.multiple_of(jnp.maximum(tail_ref[e], 0), MOE_ROWS)
            return pltpu.make_async_copy(zero_ref, xs_ref.at[pl.ds(row, MOE_ROWS)], zsem)

        for e in range(N_EXPERTS):
            @pl.when(tail_ref[e] >= 0)
            def _(e=e):
                clear(e).start()
        for e in range(N_EXPERTS):
            @pl.when(tail_ref[e] >= 0)
            def _(e=e):
                clear(e).wait()

    for k in range(TOP_K):
        def start(g, c, k=k):
            for u in range(SUBLANES):
                pltpu.make_async_copy(x_ref.at[g, pl.ds(u, 1)],
                                      xs_ref.at[pl.ds(slot_ref[0, k * tm + g * SUBLANES + u], 1)], sem).start()
            return c

        lax.fori_loop(0, tm // SUBLANES, start, 0)
    for k in range(TOP_K):
        pltpu.make_async_copy(xs_ref.at[pl.ds(0, tm)], xs_ref.at[pl.ds(0, tm)], sem).wait()


def _dispatch(x, slots, tail_rows, n_rows, tm=512):
    t, d = x.shape
    grid_spec = pltpu.PrefetchScalarGridSpec(
        num_scalar_prefetch=1,
        grid=(t // tm,),
        in_specs=[pl.BlockSpec((None, 1, TOP_K * tm), lambda i, tail: (i, 0, 0), memory_space=pltpu.SMEM),
                  pl.BlockSpec((tm // SUBLANES, SUBLANES, d), lambda i, tail: (i, 0, 0))],
        out_specs=pl.BlockSpec(memory_space=pl.ANY),
        scratch_shapes=[pltpu.VMEM((MOE_ROWS, d), x.dtype), pltpu.SemaphoreType.DMA(()),
                        pltpu.SemaphoreType.DMA(())],
    )
    return pl.pallas_call(
        _dispatch_kernel,
        grid_spec=grid_spec,
        out_shape=jax.ShapeDtypeStruct((n_rows, d), x.dtype),
        compiler_params=_cparams("arbitrary"),
        name="moe_dispatch",
    )(tail_rows, slots, x.reshape(t // SUBLANES, SUBLANES, d))


def _expert_ffn_kernel(be_ref, nu_ref, x_ref, w1_ref, b1_ref, w2_ref, b2_ref, y_ref, w1b_ref, w2b_ref):
    d = x_ref.shape[1]
    i = pl.program_id(0)

    @pl.when((i == 0) | (be_ref[i] != be_ref[jnp.maximum(i - 1, 0)]))
    def _():
        w1b_ref[...] = w1_ref[...].astype(BF16)
        w2b_ref[...] = w2_ref[...].astype(BF16)

    @pl.when(i < nu_ref[0])
    def _():
        xb = x_ref[...].astype(BF16)
        glu = jnp.minimum(_dot(xb, w1b_ref[:, 0:d]) + b1_ref[:, 0:d], SWIGLU_LIMIT)
        lin = jnp.clip(_dot(xb, w1b_ref[:, d:2 * d]) + b1_ref[:, d:2 * d], -SWIGLU_LIMIT, SWIGLU_LIMIT)
        act = glu * _sigmoid(SWIGLU_ALPHA * glu) * (lin + 1.0)
        y_ref[...] = _dot(act.astype(BF16), w2b_ref[...]) + b2_ref[...]

    @pl.when(i >= nu_ref[0])
    def _():
        y_ref[...] = jnp.zeros_like(y_ref)


def _expert_ffn(xs, block_e, n_used, w1_all, b1, w2_all, b2, layer):
    n_rows, d = xs.shape
    nb = n_rows // MOE_ROWS
    wsel = lambda i, be, nu: (be[i], 0, 0)
    wsel_l = lambda i, be, nu: (layer, be[i], 0, 0)
    xsel = lambda i, be, nu: (jnp.minimum(i, nu[0] - 1), 0)
    grid_spec = pltpu.PrefetchScalarGridSpec(
        num_scalar_prefetch=2,
        grid=(nb,),
        in_specs=[pl.BlockSpec((MOE_ROWS, d), xsel),
                  pl.BlockSpec((None, None, d, 2 * d), wsel_l), pl.BlockSpec((None, 1, 2 * d), wsel),
                  pl.BlockSpec((None, None, d, d), wsel_l), pl.BlockSpec((None, 1, d), wsel)],
        out_specs=pl.BlockSpec((MOE_ROWS, d), lambda i, be, nu: (i, 0)),
        scratch_shapes=[pltpu.VMEM((d, 2 * d), BF16), pltpu.VMEM((d, d), BF16)],
    )
    return pl.pallas_call(
        _expert_ffn_kernel,
        grid_spec=grid_spec,
        out_shape=jax.ShapeDtypeStruct((n_rows, d), F32),
        compiler_params=_cparams("arbitrary"),
        name="expert_ffn",
    )(block_e, n_used, xs, w1_all, b1, w2_all, b2)


def _combine_kernel(alpha, slot_ref, nslot_ref, gate_ref, h_ref, g_ref, b_ref, ys_ref, o_ref, buf, sem):
    tm = h_ref.shape[0]
    i = pl.program_id(0)

    def gather(s_ref, half):
        for k in range(TOP_K):
            def start(g, c, k=k):
                for u in range(SUBLANES):
                    pltpu.make_async_copy(ys_ref.at[pl.ds(s_ref[0, k * tm + g * SUBLANES + u], 1)],
                                          buf.at[half, k, g, pl.ds(u, 1)], sem.at[half]).start()
                return c

            lax.fori_loop(0, tm // SUBLANES, start, 0)

    @pl.when(i == 0)
    def _():
        gather(slot_ref, 0)

    @pl.when(i + 1 < pl.num_programs(0))
    def _():
        gather(nslot_ref, (i + 1) % 2)

    cur = i % 2
    for k in range(TOP_K):
        pltpu.make_async_copy(ys_ref.at[pl.ds(0, tm)], ys_ref.at[pl.ds(0, tm)], sem.at[cur]).wait()
    gate = gate_ref[...]
    rows = lambda k: buf[cur, k].reshape(h_ref.shape)
    y = rows(0) * gate[:, 0:1]
    for k in range(1, TOP_K):
        y = y + rows(k) * gate[:, k:k + 1]
    o_ref[...] = _layer_norm(alpha * h_ref[...] + y, g_ref[...], b_ref[...])


def _combine_ln(ys, slots, gates, h, g, b, alpha, tm=256):
    t, d = h.shape
    n_t = t // tm
    row = lambda i: (i, 0)
    fix = lambda i: (0, 0)
    slot_block = (None, 1, TOP_K * tm)
    return pl.pallas_call(
        functools.partial(_combine_kernel, alpha),
        grid=(n_t,),
        in_specs=[pl.BlockSpec(slot_block, lambda i: (i, 0, 0), memory_space=pltpu.SMEM),
                  pl.BlockSpec(slot_block, lambda i: (jnp.minimum(i + 1, n_t - 1), 0, 0),
                               memory_space=pltpu.SMEM),
                  pl.BlockSpec((tm, TOP_K), row), pl.BlockSpec((tm, d), row),
                  pl.BlockSpec((1, d), fix), pl.BlockSpec((1, d), fix),
                  pl.BlockSpec(memory_space=pl.ANY)],
        out_specs=pl.BlockSpec((tm, d), row),
        out_shape=jax.ShapeDtypeStruct((t, d), F32),
        scratch_shapes=[pltpu.VMEM((2, TOP_K, tm // SUBLANES, SUBLANES, d), F32), pltpu.SemaphoreType.DMA((2,))],
        compiler_params=_cparams("arbitrary"),
        name="moe_combine_ln",
    )(slots, slots, gates, h, g, b, ys)


def _tile_slots(slot, tm):
    bsz, _, seq = slot.shape
    s = slot.reshape(bsz, TOP_K, seq // tm, tm).transpose(0, 2, 1, 3)
    return s.reshape(bsz * (seq // tm), 1, TOP_K * tm)


def _moe_ln(h2, idx, gates, w1_all, b1, w2_all, b2, layer, ln_g, ln_b, alpha, tm_d=512, tm_c=256):
    t, d = h2.shape
    slot, counts = _expert_slots(idx)
    blocks_per_e = (counts[:, 0] + MOE_ROWS - 1) // MOE_ROWS
    block_end = jnp.cumsum(blocks_per_e)
    nb = t * TOP_K // MOE_ROWS + N_EXPERTS
    block_e = jnp.sum((block_end[None, :] <= jnp.arange(nb, dtype=I32)[:, None]).astype(I32), axis=1)
    block_e = jnp.minimum(block_e, N_EXPERTS - 1)
    n_used = block_end[-1:].astype(I32)
    tail_rows = jnp.where(blocks_per_e > 0, (block_end - 1) * MOE_ROWS, -1).astype(I32)
    xs = _dispatch(h2, _tile_slots(slot, tm_d), tail_rows, nb * MOE_ROWS, tm_d)
    ys = _expert_ffn(xs, block_e, n_used, w1_all, b1[:, None, :], w2_all, b2[:, None, :], layer)
    gates_t = gates.transpose(0, 2, 1).reshape(t, TOP_K)
    return _combine_ln(ys, _tile_slots(slot, tm_c), gates_t, h2, ln_g[None], ln_b[None], alpha, tm_c)


def kernel(x, mem, positions, ln_g, ln_b, a_w_in, a_lower_bounds, a_norm_g, a_w_out, b_w_in, b_w_out,
           c_a_re, c_a_im, c_log_dt, c_b_re, c_b_im, c_c_re, c_c_im, c_d, c_w_glu, m_w_kv, m_w_q, m_w_o,
           r_w, r_b, e_w1, e_b1, e_w2, e_b2):
    bsz, seq, d = x.shape
    depth = ln_g.shape[0]
    alpha = (2 * depth) ** 0.25
    t = bsz * seq
    lb = jax.nn.softmax(a_lower_bounds.astype(F32), axis=0)
    lb = jnp.cumsum(lb, axis=0) - lb[0]
    mem_kv = _matmul(mem.reshape(-1, d), m_w_kv.astype(BF16), BF16, 512, 1024)
    mem_kv = mem_kv.reshape(bsz, mem.shape[1], 2 * d)
    h = x.reshape(t, d)
    for layer in range(depth):
        kind, j = layer % N_MIXERS, layer // N_MIXERS
        g1, b1 = ln_g[layer, 0][None], ln_b[layer, 0][None]
        if kind == 0:
            h = _hgrn2_mixer_ln(h, a_w_in[j], a_norm_g[j], a_w_out[j], lb[layer], g1, b1, alpha, bsz, seq)
        elif kind == 1:
            h = _dilated_mixer_ln(h, positions, b_w_in[j], b_w_out[j], g1, b1, alpha, bsz, seq)
        else:
            h = _s5_mixer_ln(h, c_a_re[j], c_a_im[j], c_log_dt[j], c_b_re[j], c_b_im[j], c_c_re[j],
                             c_c_im[j], c_d[j], c_w_glu[j], g1, b1, alpha, bsz, seq)
        wr = jnp.pad(r_w[layer], ((0, 0), (0, LANES - N_EXPERTS))).astype(BF16)
        br = jnp.pad(r_b[layer], (0, LANES - N_EXPERTS))[None]
        h, idx, gates = _cross_attention_router(h, mem_kv, m_w_q[layer].astype(BF16), m_w_o[layer].astype(BF16),
                                                ln_g[layer, 1][None], ln_b[layer, 1][None], wr, br,
                                                alpha, bsz, seq)
        h = _moe_ln(h, idx, gates, e_w1, e_b1[layer], e_w2, e_b2[layer], layer,
                    ln_g[layer, 2], ln_b[layer, 2], alpha)
    return h.reshape(bsz, seq, d)
```

```python
import functools
import math

import jax
import jax.numpy as jnp
from jax import lax
from jax.experimental import pallas as pl
from jax.experimental.pallas import tpu as pltpu

F32 = jnp.float32
BF16 = jnp.bfloat16
I32 = jnp.int32

D_MODEL = 1024
N_MIXERS = 3
NORM_EPS = 1e-5
ROPE_THETA = 10000.0
MIN_FORGET = 1e-6
MASK_VALUE = -1e30

A_HEADS = 8
A_DIM = 128
A_CHUNK = 64
A_SUB = 8

B_GROUPS = ((128, 1), (512, 4), (2048, 16))
B_HEAD_DIM = 64
B_HEADS = 16
B_BLOCK = 128

C_GROUP_CH = 16
C_GROUPS = 64
C_STATE = 64
C_CHUNK = 16
C_TILE_GROUPS = 8

M_HEADS = 4
M_HEAD_DIM = 256

N_EXPERTS = 32
TOP_K = 4
SWIGLU_ALPHA = 1.702
SWIGLU_LIMIT = 7.0
MOE_ROWS = 512
SUBLANES = 8
DMA_GROUP_UNROLL = 4

LANES = 128
VMEM_LIMIT = 48 * 1024 * 1024


def _cparams(*sem):
    return pltpu.CompilerParams(dimension_semantics=sem, vmem_limit_bytes=VMEM_LIMIT)


def _layer_norm(z, g, b):
    mu = jnp.mean(z, axis=-1, keepdims=True)
    zc = z - mu
    var = jnp.mean(zc * zc, axis=-1, keepdims=True)
    return zc * lax.rsqrt(var + NORM_EPS) * g + b


def _sigmoid(x):
    return 1.0 / (1.0 + jnp.exp(-x))


def _dot(a, b):
    return jnp.dot(a, b, preferred_element_type=F32)


def _dot_nt(a, b):
    return lax.dot_general(a, b, (((1,), (1,)), ((), ())), preferred_element_type=F32)


def _dot_tn(a, b):
    return lax.dot_general(a, b, (((0,), (0,)), ((), ())), preferred_element_type=F32)


def _matmul_kernel(x_ref, w_ref, o_ref):
    o_ref[...] = _dot(x_ref[...].astype(BF16), w_ref[...]).astype(o_ref.dtype)


def _matmul(x, w, out_dtype, tm, tn):
    m, k = x.shape
    n = w.shape[1]
    return pl.pallas_call(
        _matmul_kernel,
        grid=(n // tn, m // tm),
        in_specs=[pl.BlockSpec((tm, k), lambda j, i: (i, 0)),
                  pl.BlockSpec((k, tn), lambda j, i: (0, j))],
        out_specs=pl.BlockSpec((tm, tn), lambda j, i: (i, j)),
        out_shape=jax.ShapeDtypeStruct((m, n), out_dtype),
        compiler_params=_cparams("parallel", "parallel"),
        name="matmul",
    )(x, w)


def _proj_res_ln_kernel(alpha, a_ref, w_ref, h_ref, g_ref, b_ref, o_ref):
    mix = _dot(a_ref[...].astype(BF16), w_ref[...])
    o_ref[...] = _layer_norm(alpha * h_ref[...] + mix, g_ref[...], b_ref[...])


def _proj_res_ln(a, w, h, g, b, alpha, tm=512):
    m, d = h.shape
    k = a.shape[1]
    row = lambda i: (i, 0)
    fix = lambda i: (0, 0)
    return pl.pallas_call(
        functools.partial(_proj_res_ln_kernel, alpha),
        grid=(m // tm,),
        in_specs=[pl.BlockSpec((tm, k), row), pl.BlockSpec((k, d), fix),
                  pl.BlockSpec((tm, d), row), pl.BlockSpec((1, d), fix), pl.BlockSpec((1, d), fix)],
        out_specs=pl.BlockSpec((tm, d), row),
        out_shape=jax.ShapeDtypeStruct((m, d), F32),
        compiler_params=_cparams("parallel"),
        name="proj_res_ln",
    )(a, w, h, g, b)


def _hgrn_in_kernel(x_ref, w_ref, lb_ref, q_ref, k_ref, v_ref, lf_ref, g_ref):
    d = x_ref.shape[1]
    xb = x_ref[...].astype(BF16)
    q = _dot(xb, w_ref[:, 0:d])
    q_ref[...] = q * _sigmoid(q)
    f = _dot(xb, w_ref[:, d:2 * d])
    lb = lb_ref[...]
    fg = lb + (1.0 - lb) * _sigmoid(f)
    lf_ref[...] = jnp.log(jnp.maximum(fg, MIN_FORGET))
    k_ref[...] = 1.0 - fg
    v_ref[...] = _dot(xb, w_ref[:, 2 * d:3 * d])
    g = _dot(xb, w_ref[:, 3 * d:4 * d])
    g_ref[...] = g * _sigmoid(g)


def _hgrn_in(x, w, lb, tm=256):
    m, d = x.shape
    row = lambda i: (i, 0)
    fix = lambda i: (0, 0)
    out = jax.ShapeDtypeStruct((m, d), F32)
    return pl.pallas_call(
        _hgrn_in_kernel,
        grid=(m // tm,),
        in_specs=[pl.BlockSpec((tm, d), row), pl.BlockSpec((d, 4 * d), fix), pl.BlockSpec((1, d), fix)],
        out_specs=[pl.BlockSpec((tm, d), row)] * 5,
        out_shape=[out] * 5,
        compiler_params=_cparams("parallel"),
        name="hgrn_in",
    )(x, w, lb)


def _rows_from(b, picks, span):
    parts = [jnp.broadcast_to(b[p:p + 1, :], (span, b.shape[1])) for p in picks]
    return parts[0] if len(parts) == 1 else jnp.concatenate(parts, axis=0)


def _hgrn_chunk_kernel(q_ref, k_ref, v_ref, lf_ref, g_ref, ng_ref, o_ref, state_ref):
    c = A_CHUNK
    n_sub = q_ref.shape[0] // c

    @pl.when(pl.program_id(1) == 0)
    def _():
        state_ref[...] = jnp.zeros_like(state_ref)

    ti = lax.broadcasted_iota(I32, (c, c), 0)
    si = lax.broadcasted_iota(I32, (c, c), 1)
    row = lax.broadcasted_iota(I32, (c, A_DIM), 0)
    halves = []
    half = c // 2
    while half >= A_SUB:
        halves.append(half)
        half //= 2
    level_masks = [((ti // (2 * hf)) == (si // (2 * hf))) & ((ti % (2 * hf)) >= hf) & ((si % (2 * hf)) < hf)
                   for hf in halves]
    diag_mask = ((ti // A_SUB) == (si // A_SUB)) & (si <= ti)

    def chunk(ci, carry):
        r0 = pl.multiple_of(ci * c, c)
        for hd in range(A_HEADS):
            cols = slice(hd * A_DIM, (hd + 1) * A_DIM)
            q = q_ref[pl.ds(r0, c), cols]
            k = k_ref[pl.ds(r0, c), cols]
            v = v_ref[pl.ds(r0, c), cols]
            b = lf_ref[pl.ds(r0, c), cols]
            sh = 1
            while sh < c:
                b = b + jnp.where(row >= sh, pltpu.roll(b, sh, axis=0), 0.0)
                sh *= 2
            scores = jnp.zeros((c, c), F32)
            for hf, mask in zip(halves, level_masks):
                ref = _rows_from(b, [blk * 2 * hf + hf - 1 for blk in range(c // (2 * hf))], 2 * hf)
                qt = (q * jnp.exp(jnp.minimum(b - ref, 0.0))).astype(BF16)
                kt = (k * jnp.exp(jnp.minimum(ref - b, 0.0))).astype(BF16)
                scores = scores + jnp.where(mask, _dot_nt(qt, kt), 0.0)
            lo = _rows_from(b, [blk * A_SUB for blk in range(c // A_SUB)], A_SUB)
            hi = _rows_from(b, [blk * A_SUB + A_SUB - 1 for blk in range(c // A_SUB)], A_SUB)
            mid = 0.5 * (lo + hi)
            qt = (q * jnp.exp(b - mid)).astype(BF16)
            kt = (k * jnp.exp(mid - b)).astype(BF16)
            scores = scores + jnp.where(diag_mask, _dot_nt(qt, kt), 0.0)

            st = state_ref[hd]
            b_end = b[c - 1:c, :]
            o = _dot(scores.astype(BF16), v.astype(BF16))
            o = o + _dot_nt((q * jnp.exp(b)).astype(BF16), st.astype(BF16))
            ke = (k * jnp.exp(b_end - b)).astype(BF16)
            state_ref[hd] = st * jnp.exp(b_end) + _dot_tn(v.astype(BF16), ke)

            o = o * lax.rsqrt(jnp.mean(o * o, axis=-1, keepdims=True) + NORM_EPS) * ng_ref[:, cols]
            o_ref[pl.ds(r0, c), cols] = (o * g_ref[pl.ds(r0, c), cols]).astype(o_ref.dtype)
        return carry

    lax.fori_loop(0, n_sub, chunk, 0)


def _hgrn_chunk(q, k, v, lf, g, ng, bsz, seq, tc=256):
    d = q.shape[1]
    n_t = seq // tc
    row = lambda b, i: (b * n_t + i, 0)
    return pl.pallas_call(
        _hgrn_chunk_kernel,
        grid=(bsz, n_t),
        in_specs=[pl.BlockSpec((tc, d), row)] * 5 + [pl.BlockSpec((1, d), lambda b, i: (0, 0))],
        out_specs=pl.BlockSpec((tc, d), row),
        out_shape=jax.ShapeDtypeStruct(q.shape, BF16),
        scratch_shapes=[pltpu.VMEM((A_HEADS, A_DIM, A_DIM), F32)],
        compiler_params=_cparams("parallel", "arbitrary"),
        name="hgrn_chunk",
    )(q, k, v, lf, g, ng)


def _hgrn2_mixer_ln(h, w_in, norm_g, w_out, lb, ln_g, ln_b, alpha, bsz, seq):
    q, k, v, lf, g = _hgrn_in(h, w_in.astype(BF16), lb.reshape(1, -1))
    o = _hgrn_chunk(q, k, v, lf, g, norm_g.reshape(1, -1), bsz, seq)
    return _proj_res_ln(o, w_out.astype(BF16), h, ln_g, ln_b, alpha)


def _rope_table_kernel(pos_ref, cos_ref, sin_ref):
    half = B_HEAD_DIM // 2
    lane = lax.broadcasted_iota(I32, cos_ref.shape, 1)
    j = (lane % half).astype(F32)
    inv_freq = jnp.exp(j * (-math.log(ROPE_THETA) / half))
    ang = pos_ref[...].astype(F32) * inv_freq
    cos_ref[...] = jnp.cos(ang)
    sin_ref[...] = jnp.where(lane < 2 * half, -1.0, 1.0) * jnp.sin(ang)


def _rope_tables(pos_col, tm=1024):
    t = pos_col.shape[0]
    out = jax.ShapeDtypeStruct((t, LANES), F32)
    return pl.pallas_call(
        _rope_table_kernel,
        grid=(t // tm,),
        in_specs=[pl.BlockSpec((tm, 1), lambda i: (i, 0))],
        out_specs=[pl.BlockSpec((tm, LANES), lambda i: (i, 0))] * 2,
        out_shape=[out, out],
        compiler_params=_cparams("parallel"),
        name="rope_tables",
    )(pos_col)


def _dil_attn_kernel(reach, q_ref, kp_ref, kc_ref, vp_ref, vc_ref, cq_ref, sq_ref, cp_ref, sp_ref,
                     o_ref, lse_ref):
    nq = q_ref.shape[0]
    i = pl.program_id(2)
    scale = B_HEAD_DIM ** -0.5
    half_lanes = LANES // 2

    def rope(x, cos, sin):
        return x * cos + pltpu.roll(x, half_lanes, axis=1) * sin

    qi = lax.broadcasted_iota(I32, (nq, 2 * nq), 0)
    ki = lax.broadcasted_iota(I32, (nq, 2 * nq), 1)
    dist = qi + nq - ki
    valid = (dist >= 0) & (dist <= reach) & ((ki >= nq) | (i > 0))
    lane = lax.broadcasted_iota(I32, (1, LANES), 1)
    head_a = ((lane // (B_HEAD_DIM // 2)) % 2) == 0
    v_head_a = lane < B_HEAD_DIM
    cq, sq = cq_ref[...], sq_ref[...]
    ck = jnp.concatenate([cp_ref[...], cq], axis=0)
    sk = jnp.concatenate([sp_ref[...], sq], axis=0)
    for pr in range(B_HEADS // 2):
        cols = slice(pr * LANES, (pr + 1) * LANES)
        q = rope(q_ref[:, cols], cq, sq) * scale
        k = rope(jnp.concatenate([kp_ref[:, cols], kc_ref[:, cols]], axis=0), ck, sk).astype(BF16)
        v = jnp.concatenate([vp_ref[:, cols], vc_ref[:, cols]], axis=0).astype(BF16)
        outs, lses = [], []
        for sel in (head_a, ~head_a):
            s = _dot_nt(jnp.where(sel, q, 0.0).astype(BF16), k)
            s = jnp.where(valid, s, MASK_VALUE)
            m = jnp.max(s, axis=-1, keepdims=True)
            p = jnp.exp(s - m)
            l = jnp.sum(p, axis=-1, keepdims=True)
            outs.append(_dot(p.astype(BF16), v) / l)
            lses.append(m + jnp.log(l))
        o_ref[:, cols] = jnp.where(v_head_a, outs[0], outs[1])
        lse_ref[:, cols] = jnp.where(v_head_a, lses[0], lses[1])


def _residue_perm(tm, dilation):
    per = tm // dilation
    j = jnp.arange(tm)
    src = (j % per) * dilation + j // per
    return (src[:, None] == jnp.arange(tm)[None, :]).astype(BF16)


def _regroup_rows(p, x):
    hi = x.astype(BF16)
    rest = x - hi.astype(F32)
    mid = rest.astype(BF16)
    lo = (rest - mid.astype(F32)).astype(BF16)
    return _dot(p, hi) + _dot(p, mid) + _dot(p, lo)


def _dil_proj_kernel(x_ref, p_ref, w_ref, o_ref):
    xs = _dot(p_ref[...], x_ref[...].astype(BF16)).astype(BF16)
    o_ref[...] = _dot(xs, w_ref[...]).reshape(o_ref.shape)


def _dil_proj(h, w, dilation, bsz, seq, tm=512):
    d, n = w.shape
    n_t = seq // tm
    return pl.pallas_call(
        _dil_proj_kernel,
        grid=(bsz * n_t,),
        in_specs=[pl.BlockSpec((tm, d), lambda i: (i, 0)), pl.BlockSpec((tm, tm), lambda i: (0, 0)),
                  pl.BlockSpec((d, n), lambda i: (0, 0))],
        out_specs=pl.BlockSpec((None, dilation, tm // dilation, n), lambda i: (i // n_t, 0, i % n_t, 0)),
        out_shape=jax.ShapeDtypeStruct((bsz, dilation, seq // dilation, n), F32),
        compiler_params=_cparams("parallel"),
        name=f"dil_proj_{dilation}",
    )(h, _residue_perm(tm, dilation), w)


def _dil_attention(proj, cos, sin, dilation, reach, bsz, seq):
    d = D_MODEL
    sub = seq // dilation
    nb = sub // B_BLOCK
    blk = (None, None, B_BLOCK, d)
    tab = (None, None, B_BLOCK, LANES)
    prev = lambda i: jnp.maximum(i - 1, 0)
    in_specs = [
        pl.BlockSpec(blk, lambda b, r, i: (b, r, i, 0)),
        pl.BlockSpec(blk, lambda b, r, i: (b, r, prev(i), 1)),
        pl.BlockSpec(blk, lambda b, r, i: (b, r, i, 1)),
        pl.BlockSpec(blk, lambda b, r, i: (b, r, prev(i), 2)),
        pl.BlockSpec(blk, lambda b, r, i: (b, r, i, 2)),
        pl.BlockSpec(tab, lambda b, r, i: (b, r, i, 0)),
        pl.BlockSpec(tab, lambda b, r, i: (b, r, i, 0)),
        pl.BlockSpec(tab, lambda b, r, i: (b, r, prev(i), 0)),
        pl.BlockSpec(tab, lambda b, r, i: (b, r, prev(i), 0)),
    ]
    out_spec = pl.BlockSpec(blk, lambda b, r, i: (b, r, i, 0))
    out = jax.ShapeDtypeStruct((bsz, dilation, sub, d), F32)
    return pl.pallas_call(
        functools.partial(_dil_attn_kernel, reach),
        grid=(bsz, dilation, nb),
        in_specs=in_specs,
        out_specs=[out_spec, out_spec],
        out_shape=[out, out],
        compiler_params=_cparams("parallel", "parallel", "arbitrary"),
        name=f"dil_attn_{dilation}",
    )(proj, proj, proj, proj, proj, cos, sin, cos, sin)


def _dil_out_kernel(alpha, o0, o1, o2, l0, l1, l2, p1_ref, p2_ref, w_ref, h_ref, g_ref, b_ref, out_ref):
    tm, d = h_ref.shape
    o1v = _regroup_rows(p1_ref[...], o1[...].reshape(tm, d))
    l1v = _regroup_rows(p1_ref[...], l1[...].reshape(tm, d))
    o2v = _regroup_rows(p2_ref[...], o2[...].reshape(tm, d))
    l2v = _regroup_rows(p2_ref[...], l2[...].reshape(tm, d))
    l0v = l0[...]
    m = jnp.maximum(jnp.maximum(l0v, l1v), l2v)
    e0, e1, e2 = jnp.exp(l0v - m), jnp.exp(l1v - m), jnp.exp(l2v - m)
    o = (e0 * o0[...] + e1 * o1v + e2 * o2v) / (e0 + e1 + e2)
    mix = _dot(o.astype(BF16), w_ref[...])
    out_ref[...] = _layer_norm(alpha * h_ref[...] + mix, g_ref[...], b_ref[...])


def _dil_out(outs, lses, w, h, g, b, alpha, bsz, seq, tm=256):
    m, d = h.shape
    n_t = seq // tm
    row = lambda i: (i, 0)
    fix = lambda i: (0, 0)

    def grouped(a):
        dil = a.shape[1]
        return pl.BlockSpec((None, dil, tm // dil, d), lambda i: (i // n_t, 0, i % n_t, 0))

    perms = [_residue_perm(tm, a.shape[1]).T for a in outs[1:]]
    return pl.pallas_call(
        functools.partial(_dil_out_kernel, alpha),
        grid=(m // tm,),
        in_specs=[pl.BlockSpec((tm, d), row), grouped(outs[1]), grouped(outs[2]),
                  pl.BlockSpec((tm, d), row), grouped(lses[1]), grouped(lses[2]),
                  pl.BlockSpec((tm, tm), fix), pl.BlockSpec((tm, tm), fix),
                  pl.BlockSpec((d, d), fix), pl.BlockSpec((tm, d), row),
                  pl.BlockSpec((1, d), fix), pl.BlockSpec((1, d), fix)],
        out_specs=pl.BlockSpec((tm, d), row),
        out_shape=jax.ShapeDtypeStruct((m, d), F32),
        compiler_params=_cparams("parallel"),
        name="dil_out",
    )(outs[0].reshape(m, d), outs[1], outs[2], lses[0].reshape(m, d), lses[1], lses[2], *perms, w, h, g, b)


def _rope_col_perm():
    half = B_HEAD_DIM // 2
    idx = []
    for pr in range(B_HEADS // 2):
        a, b = 2 * pr * B_HEAD_DIM, (2 * pr + 1) * B_HEAD_DIM
        idx += list(range(a, a + half)) + list(range(b, b + half))
        idx += list(range(a + half, a + 2 * half)) + list(range(b + half, b + 2 * half))
    return jnp.asarray(idx, dtype=I32)


def _dilated_mixer_ln(h, positions, w_in, w_out, ln_g, ln_b, alpha, bsz, seq):
    d = D_MODEL
    perm = _rope_col_perm()
    w = w_in.reshape(d, len(B_GROUPS), 3, d)
    w = jnp.concatenate([w[:, :, 0:2, :][..., perm], w[:, :, 2:3, :]], axis=2)
    w = w.reshape(d, len(B_GROUPS), 3 * d).astype(BF16)
    outs, lses = [], []
    for gi, (window, dilation) in enumerate(B_GROUPS):
        sub = seq // dilation
        if dilation == 1:
            proj = _matmul(h, w[:, gi, :], F32, 512, 1024).reshape(bsz, 1, seq, 3 * d)
        else:
            proj = _dil_proj(h, w[:, gi, :], dilation, bsz, seq)
        pos =positions.reshape(bsz, sub, dilation).transpose(0, 2, 1).reshape(-1, 1)
        cos, sin = _rope_tables(pos)
        tab = (bsz, dilation, sub, LANES)
        o, l = _dil_attention(proj, cos.reshape(tab), sin.reshape(tab), dilation, window // dilation, bsz, seq)
        outs.append(o)
        lses.append(l)
    return _dil_out(outs, lses, w_out.astype(BF16), h, ln_g, ln_b, alpha, bsz, seq)


def _s5_operators(a_re, a_im, log_dt, b_re, b_im, c_re, c_im):
    hp = lax.Precision.HIGHEST
    L = C_CHUNK
    dt = jnp.exp(log_dt)[:, None]
    lam_re = jnp.exp(a_re * dt) * jnp.cos(a_im * dt)
    lam_im = jnp.exp(a_re * dt) * jnp.sin(a_im * dt)
    den = a_re * a_re + a_im * a_im
    fr = ((lam_re - 1.0) * a_re + lam_im * a_im) / den
    fi = (lam_im * a_re - (lam_re - 1.0) * a_im) / den
    bb_re = fr[..., None] * b_re - fi[..., None] * b_im
    bb_im = fr[..., None] * b_im + fi[..., None] * b_re
    tau = jnp.arange(L + 1, dtype=F32)[:, None, None]
    mag = jnp.exp(a_re * dt * tau)
    pw_re = mag * jnp.cos(a_im * dt * tau)
    pw_im = mag * jnp.sin(a_im * dt * tau)
    cl_re = c_re[None] * pw_re[:, :, None, :] - c_im[None] * pw_im[:, :, None, :]
    cl_im = c_re[None] * pw_im[:, :, None, :] + c_im[None] * pw_re[:, :, None, :]
    kern = (jnp.einsum('tgon,gni->tgoi', cl_re, bb_re, precision=hp)
            - jnp.einsum('tgon,gni->tgoi', cl_im, bb_im, precision=hp)).transpose(0, 1, 3, 2)
    t_idx = jnp.arange(L)
    rev = pw_re[L - 1 - t_idx], pw_im[L - 1 - t_idx]
    p_re = rev[0][..., None] * bb_re[None] - rev[1][..., None] * bb_im[None]
    p_im = rev[0][..., None] * bb_im[None] + rev[1][..., None] * bb_re[None]
    p_op = jnp.concatenate([p_re, p_im], axis=2).transpose(0, 1, 3, 2)
    q_op = jnp.concatenate([cl_re[1:], -cl_im[1:]], axis=3).transpose(0, 1, 3, 2)
    lam_l = jnp.concatenate([pw_re[L], pw_im[L]], axis=-1)
    return kern, p_op, q_op, lam_l


def _s5_tile_operators(kern, p_op, q_op, lam_l, n_chunks):
    gt = C_TILE_GROUPS
    nt = C_GROUPS // gt
    L, ch, ns = C_CHUNK, C_GROUP_CH, 2 * C_STATE
    r128 = jnp.arange(LANES)[:, None]
    rows_q = jnp.arange(gt * ns)[:, None]
    own_d = (r128 // ch == jnp.arange(LANES)[None, :] // ch).astype(F32)
    own_p = (r128 // ch == jnp.arange(gt * ns)[None, :] // ns).astype(F32)
    own_q = (rows_q // ns == jnp.arange(LANES)[None, :] // ch).astype(F32)
    lag_blk = jnp.tile(kern[:L].reshape(L, nt, LANES, ch), (1, 1, 1, gt)) * own_d
    zero = jnp.zeros_like(lag_blk[0])
    m_t = jnp.concatenate(
        [jnp.concatenate([lag_blk[t - s] if t >= s else zero for t in range(L)], axis=-1) for s in range(L)],
        axis=1)
    p_blk = jnp.tile(p_op.reshape(L, nt, LANES, ns), (1, 1, 1, gt)) * own_p
    p_t = p_blk.transpose(1, 0, 2, 3).reshape(nt, L * LANES, gt * ns)
    q_blk = jnp.tile(q_op.reshape(L, nt, gt * ns, ch), (1, 1, 1, gt)) * own_q
    q_t = jnp.concatenate([q_blk[t] for t in range(L)], axis=-1)
    m_t, p_t, q_t = m_t.astype(BF16), p_t.astype(BF16), q_t.astype(BF16)
    lr, li = lam_l[:, :C_STATE], lam_l[:, C_STATE:]
    las, lbs = [], []
    k = 1
    while k < n_chunks:
        las.append(jnp.concatenate([lr, lr], axis=-1).reshape(nt, gt * ns))
        lbs.append(jnp.concatenate([-li, li], axis=-1).reshape(nt, gt * ns))
        lr, li = lr * lr - li * li, 2.0 * lr * li
        k *= 2
    return m_t, p_t, q_t, jnp.stack(las, axis=1), jnp.stack(lbs, axis=1)


def _s5_local_kernel(u_ref, m_ref, p_ref, y_ref, x_ref):
    n = u_ref.shape[0] // C_CHUNK
    u = jnp.concatenate([u_ref[pl.ds(s, n, stride=C_CHUNK), :].astype(BF16) for s in range(C_CHUNK)], axis=1)
    y = _dot(u, m_ref[...])
    for t in range(C_CHUNK):
        y_ref[pl.ds(t, n, stride=C_CHUNK), :] = y[:, t * LANES:(t + 1) * LANES]
    x_ref[...] = _dot(u, p_ref[...])


def _s5_local(h, m_t, p_t, bsz, seq):
    nt, kf, ns = p_t.shape
    n = seq // C_CHUNK
    return pl.pallas_call(
        _s5_local_kernel,
        grid=(nt, bsz),
        in_specs=[pl.BlockSpec((seq, LANES), lambda j, b: (b, j)),
                  pl.BlockSpec((None, kf, kf), lambda j, b: (j, 0, 0)),
                  pl.BlockSpec((None, kf, ns), lambda j, b: (j, 0, 0))],
        out_specs=[pl.BlockSpec((seq, LANES), lambda j, b: (b, j)),
                   pl.BlockSpec((None, n, ns), lambda j, b: (j, b, 0))],
        out_shape=[jax.ShapeDtypeStruct(h.shape, F32), jax.ShapeDtypeStruct((nt, bsz * n, ns), F32)],
        compiler_params=_cparams("parallel", "parallel"),
        name="s5_local",
    )(h, m_t, p_t)


def _s5_state_kernel(y_ref, x_ref, q_ref, la_ref, lb_ref, o_ref):
    n = x_ref.shape[0]
    row = lax.broadcasted_iota(I32, (n, LANES), 0)
    cols = []
    for g in range(C_TILE_GROUPS):
        lanes = slice(g * LANES, (g + 1) * LANES)
        x = x_ref[:, lanes]
        k, sh = 0, 1
        while sh < n:
            v = jnp.where(row >= sh, pltpu.roll(x, sh, axis=0), 0.0)
            x = x + la_ref[k:k + 1, lanes] * v + lb_ref[k:k + 1, lanes] * pltpu.roll(v, C_STATE, axis=1)
            k += 1
            sh *= 2
        cols.append(jnp.where(row >= 1, pltpu.roll(x, 1, axis=0), 0.0).astype(BF16))
    yq = _dot(jnp.concatenate(cols, axis=1), q_ref[...])
    for t in range(C_CHUNK):
        rows = pl.ds(t, n, stride=C_CHUNK)
        o_ref[rows, :] = y_ref[rows, :] + yq[:, t * LANES:(t + 1) * LANES]


def _s5_state_out(y, x_loc, q_t, la, lb, bsz, seq):
    nt, ns, kf = q_t.shape
    n = seq // C_CHUNK
    nk = la.shape[1]
    return pl.pallas_call(
        _s5_state_kernel,
        grid=(nt, bsz),
        in_specs=[pl.BlockSpec((seq, LANES), lambda j, b: (b, j)),
                  pl.BlockSpec((None, n, ns), lambda j, b: (j, b, 0)),
                  pl.BlockSpec((None, ns, kf), lambda j, b: (j, 0, 0)),
                  pl.BlockSpec((None, nk, ns), lambda j, b: (j, 0, 0)),
                  pl.BlockSpec((None, nk, ns), lambda j, b: (j, 0, 0))],
        out_specs=pl.BlockSpec((seq, LANES), lambda j, b: (b, j)),
        out_shape=jax.ShapeDtypeStruct(y.shape, F32),
        compiler_params=_cparams("parallel", "parallel"),
        name="s5_state_out",
    )(y, x_loc, q_t, la, lb)


def _s5_out_kernel(alpha, y_ref, h_ref, dskip_ref, w_ref, g_ref, b_ref, o_ref):
    d = h_ref.shape[1]
    h = h_ref[...]
    z = y_ref[...] + dskip_ref[...] * h
    z = 0.5 * z * (1.0 + jnp.tanh(math.sqrt(2.0 / math.pi) * (z + 0.044715 * (z * z * z))))
    zb = z.astype(BF16)
    val = _dot(zb, w_ref[:, 0:d])
    gate = _dot(zb, w_ref[:, d:2 * d])
    o_ref[...] = _layer_norm(alpha * h + val * _sigmoid(gate), g_ref[...], b_ref[...])


def _s5_out(y, h, d_skip, w_glu, g, b, alpha, tm=256):
    m, d = h.shape
    row = lambda i: (i, 0)
    fix = lambda i: (0, 0)
    return pl.pallas_call(
        functools.partial(_s5_out_kernel, alpha),
        grid=(m // tm,),
        in_specs=[pl.BlockSpec((tm, d), row), pl.BlockSpec((tm, d), row), pl.BlockSpec((1, d), fix),
                  pl.BlockSpec((d, 2 * d), fix), pl.BlockSpec((1, d), fix), pl.BlockSpec((1, d), fix)],
        out_specs=pl.BlockSpec((tm, d), row),
        out_shape=jax.ShapeDtypeStruct((m, d), F32),
        compiler_params=_cparams("parallel"),
        name="s5_out",
    )(y, h, d_skip, w_glu, g, b)


def _s5_mixer_ln(h, a_re, a_im, log_dt, b_re, b_im, c_re, c_im, d_skip, w_glu, ln_g, ln_b, alpha, bsz, seq):
    ops = _s5_operators(a_re, a_im, log_dt, b_re, b_im, c_re, c_im)
    m_t, p_t, q_t, la, lb = _s5_tile_operators(*ops, seq // C_CHUNK)
    y_loc, x_loc = _s5_local(h, m_t, p_t, bsz, seq)
    y = _s5_state_out(y_loc, x_loc, q_t, la, lb, bsz, seq)
    return _s5_out(y, h, d_skip.reshape(1, -1), w_glu.astype(BF16), ln_g, ln_b, alpha)


def _cross_kernel(alpha, h_ref, kv_ref, wq_ref, wo_ref, g_ref, b_ref, wr_ref, br_ref,
                  o_ref, idx_ref, gate_ref):
    d = h_ref.shape[1]
    h = h_ref[...]
    q = (_dot(h.astype(BF16), wq_ref[...]) * (M_HEAD_DIM ** -0.5)).astype(BF16)
    heads = []
    for hd in range(M_HEADS):
        cols = slice(hd * M_HEAD_DIM, (hd + 1) * M_HEAD_DIM)
        s = _dot_nt(q[:, cols], kv_ref[:, cols])
        p = jnp.exp(s - jnp.max(s, axis=-1, keepdims=True))
        l = jnp.sum(p, axis=-1, keepdims=True)
        heads.append((_dot(p.astype(BF16), kv_ref[:, d + hd * M_HEAD_DIM:d + (hd + 1) * M_HEAD_DIM]) / l)
                     .astype(BF16))
    att = _dot(jnp.concatenate(heads, axis=-1), wo_ref[...])
    h2 = _layer_norm(alpha * h + att, g_ref[...], b_ref[...])
    o_ref[...] = h2
    logits = (_dot(h2.astype(BF16), wr_ref[...]) + br_ref[...]).T[0:N_EXPERTS, :]
    eidx = lax.broadcasted_iota(I32, logits.shape, 0)
    vals, idxs = [], []
    for _ in range(TOP_K):
        m = jnp.max(logits, axis=0, keepdims=True)
        pick = jnp.min(jnp.where(logits == m, eidx, N_EXPERTS), axis=0, keepdims=True)
        vals.append(m)
        idxs.append(pick)
        logits = jnp.where(eidx == pick, -jnp.inf, logits)
    es = [jnp.exp(v - vals[0]) for v in vals]
    tot = es[0] + es[1] + es[2] + es[3]
    idx_ref[...] = jnp.concatenate(idxs, axis=0)
    gate_ref[...] = jnp.concatenate([e / tot for e in es], axis=0)


def _cross_attention_router(h, mem_kv, wq, wo, g, b, wr, br, alpha, bsz, seq, tm=512):
    d = D_MODEL
    n_t = seq // tm
    mlen = mem_kv.shape[1]
    row = lambda bi, i: (bi * n_t + i, 0)
    fix = lambda bi, i: (0, 0)
    col = lambda bi, i: (bi, 0, i)
    return pl.pallas_call(
        functools.partial(_cross_kernel, alpha),
        grid=(bsz, n_t),
        in_specs=[pl.BlockSpec((tm, d), row),
                  pl.BlockSpec((None, mlen, 2 * d), lambda bi, i: (bi, 0, 0)),
                  pl.BlockSpec((d, d), fix), pl.BlockSpec((d, d), fix),
                  pl.BlockSpec((1, d), fix), pl.BlockSpec((1, d), fix),
                  pl.BlockSpec((d, LANES), fix), pl.BlockSpec((1, LANES), fix)],
        out_specs=[pl.BlockSpec((tm, d), row),
                   pl.BlockSpec((None, TOP_K, tm), col), pl.BlockSpec((None, TOP_K, tm), col)],
        out_shape=[jax.ShapeDtypeStruct((bsz * seq, d), F32),
                   jax.ShapeDtypeStruct((bsz, TOP_K, seq), I32),
                   jax.ShapeDtypeStruct((bsz, TOP_K, seq), F32)],
        compiler_params=_cparams("parallel", "parallel"),
        name="cross_attn_router",
    )(h, mem_kv, wq, wo, g, b, wr, br)


def _slot_kernel(idx_ref, slot_ref, count_ref, run_ref, base_ref):
    phase = pl.program_id(0)
    first = (pl.program_id(1) == 0) & (pl.program_id(2) == 0)

    @pl.when(first & (phase == 0))
    def _():
        run_ref[...] = jnp.zeros_like(run_ref)

    @pl.when(first & (phase == 1))
    def _():
        counts = run_ref[...]
        count_ref[...] = counts.astype(I32)
        padded = jnp.ceil(counts * (1.0 / MOE_ROWS)) * MOE_ROWS
        row = lax.broadcasted_iota(I32, padded.shape, 0)
        ends = padded
        sh = 1
        while sh < N_EXPERTS:
            ends = ends + jnp.where(row >= sh, pltpu.roll(ends, sh, axis=0), 0.0)
            sh *= 2
        base_ref[...] = ends - padded
        run_ref[...] = jnp.zeros_like(run_ref)

    tn = idx_ref.shape[1]
    idx = idx_ref[...]
    eidx = lax.broadcasted_iota(I32, (N_EXPERTS, tn), 0)
    hits = [eidx == idx[k:k + 1, :] for k in range(TOP_K)]
    onehot = jnp.zeros((N_EXPERTS, tn), F32)
    for hit in hits:
        onehot = onehot + jnp.where(hit, 1.0, 0.0)

    @pl.when(phase == 1)
    def _():
        earlier = lax.broadcasted_iota(I32, (tn, tn), 0) < lax.broadcasted_iota(I32, (tn, tn), 1)
        before = _dot(onehot.astype(BF16), jnp.where(earlier, 1.0, 0.0).astype(BF16))
        before = before + run_ref[:, 0:1] + base_ref[:, 0:1]
        slots = [jnp.sum(jnp.where(hit, before, 0.0), axis=0, keepdims=True) for hit in hits]
        slot_ref[...] = jnp.concatenate(slots, axis=0).astype(I32)

    run_ref[...] = run_ref[...] + jnp.sum(onehot, axis=1, keepdims=True)


def _expert_slots(idx, tn=512):
    bsz, _, seq = idx.shape
    return pl.pallas_call(
        _slot_kernel,
        grid=(2, bsz, seq // tn),
        in_specs=[pl.BlockSpec((None, TOP_K, tn), lambda p, bi, i: (bi, 0, i))],
        out_specs=[pl.BlockSpec((None, TOP_K, tn), lambda p, bi, i: (bi * p, 0, i * p)),
                   pl.BlockSpec((N_EXPERTS, LANES), lambda p, bi, i: (0, 0))],
        out_shape=[jax.ShapeDtypeStruct(idx.shape, I32), jax.ShapeDtypeStruct((N_EXPERTS, LANES), I32)],
        scratch_shapes=[pltpu.VMEM((N_EXPERTS, LANES), F32), pltpu.VMEM((N_EXPERTS, LANES), F32)],
        compiler_params=_cparams("arbitrary", "arbitrary", "arbitrary"),
        name="expert_slots",
    )(idx)


def _dispatch_kernel(tail_ref, slot_ref, x_ref, xs_ref, zero_ref, sem, zsem):
    tm = x_ref.shape[0] * SUBLANES

    @pl.when(pl.program_id(0) == 0)
    def _():
        zero_ref[...] = jnp.zeros_like(zero_ref)

        def clear(e):
            row = pl.multiple_of(jnp.maximum(tail_ref[e], 0), MOE_ROWS)
            return pltpu.make_async_copy(zero_ref, xs_ref.at[pl.ds(row, MOE_ROWS)], zsem)

        for e in range(N_EXPERTS):
            @pl.when(tail_ref[e] >= 0)
            def _(e=e):
                clear(e).start()
        for e in range(N_EXPERTS):
            @pl.when(tail_ref[e] >= 0)
            def _(e=e):
                clear(e).wait()

    for k in range(TOP_K):
        def start(g, c, k=k):
            for u in range(SUBLANES):
                pltpu.make_async_copy(x_ref.at[g, pl.ds(u, 1)],
                                      xs_ref.at[pl.ds(slot_ref[0, k * tm + g * SUBLANES + u], 1)], sem).start()
            return c

        lax.fori_loop(0, tm // SUBLANES, start, 0, unroll=DMA_GROUP_UNROLL)
    for k in range(TOP_K):
        pltpu.make_async_copy(xs_ref.at[pl.ds(0, tm)], xs_ref.at[pl.ds(0, tm)], sem).wait()


def _dispatch(x, slots, tail_rows, n_rows, tm=512):
    t, d = x.shape
    grid_spec = pltpu.PrefetchScalarGridSpec(
        num_scalar_prefetch=1,
        grid=(t // tm,),
        in_specs=[pl.BlockSpec((None, 1, TOP_K * tm), lambda i, tail: (i, 0, 0), memory_space=pltpu.SMEM),
                  pl.BlockSpec((tm // SUBLANES, SUBLANES, d), lambda i, tail: (i, 0, 0))],
        out_specs=pl.BlockSpec(memory_space=pl.ANY),
        scratch_shapes=[pltpu.VMEM((MOE_ROWS, d), x.dtype), pltpu.SemaphoreType.DMA(()),
                        pltpu.SemaphoreType.DMA(())],
    )
    return pl.pallas_call(
        _dispatch_kernel,
        grid_spec=grid_spec,
        out_shape=jax.ShapeDtypeStruct((n_rows, d), x.dtype),
        compiler_params=_cparams("arbitrary"),
        name="moe_dispatch",
    )(tail_rows, slots, x.reshape(t // SUBLANES, SUBLANES, d))


def _expert_ffn_kernel(be_ref, nu_ref, x_ref, w1_ref, b1_ref, w2_ref, b2_ref, y_ref, w1b_ref, w2b_ref):
    d = x_ref.shape[1]
    i = pl.program_id(0)

    @pl.when((i == 0) | (be_ref[i] != be_ref[jnp.maximum(i - 1, 0)]))
    def _():
        w1b_ref[...] = w1_ref[...].astype(BF16)
        w2b_ref[...] = w2_ref[...].astype(BF16)

    @pl.when(i < nu_ref[0])
    def _():
        xb = x_ref[...].astype(BF16)
        glu = jnp.minimum(_dot(xb, w1b_ref[:, 0:d]) + b1_ref[:, 0:d], SWIGLU_LIMIT)
        lin = jnp.clip(_dot(xb, w1b_ref[:, d:2 * d]) + b1_ref[:, d:2 * d], -SWIGLU_LIMIT, SWIGLU_LIMIT)
        act = glu * _sigmoid(SWIGLU_ALPHA * glu) * (lin + 1.0)
        y_ref[...] = _dot(act.astype(BF16), w2b_ref[...]) + b2_ref[...]

    @pl.when(i >= nu_ref[0])
    def _():
        y_ref[...] = jnp.zeros_like(y_ref)


def _expert_ffn(xs, block_e, n_used, w1_all, b1, w2_all, b2, layer):
    n_rows, d = xs.shape
    nb = n_rows // MOE_ROWS
    wsel = lambda i, be, nu: (be[i], 0, 0)
    wsel_l = lambda i, be, nu: (layer, be[i], 0, 0)
    xsel = lambda i, be, nu: (jnp.minimum(i, nu[0] - 1), 0)
    grid_spec = pltpu.PrefetchScalarGridSpec(
        num_scalar_prefetch=2,
        grid=(nb,),
        in_specs=[pl.BlockSpec((MOE_ROWS, d), xsel),
                  pl.BlockSpec((None, None, d, 2 * d), wsel_l), pl.BlockSpec((None, 1, 2 * d), wsel),
                  pl.BlockSpec((None, None, d, d), wsel_l), pl.BlockSpec((None, 1, d), wsel)],
        out_specs=pl.BlockSpec((MOE_ROWS, d), lambda i, be, nu: (i, 0)),
        scratch_shapes=[pltpu.VMEM((d, 2 * d), BF16), pltpu.VMEM((d, d), BF16)],
    )
    return pl.pallas_call(
        _expert_ffn_kernel,
        grid_spec=grid_spec,
        out_shape=jax.ShapeDtypeStruct((n_rows, d), F32),
        compiler_params=_cparams("arbitrary"),
        name="expert_ffn",
    )(block_e, n_used, xs, w1_all, b1, w2_all, b2)


def _combine_kernel(alpha, slot_ref, nslot_ref, gate_ref, h_ref, g_ref, b_ref, ys_ref, o_ref, buf, sem):
    tm = h_ref.shape[0]
    i = pl.program_id(0)

    def gather(s_ref, half):
        for k in range(TOP_K):
            def start(g, c, k=k):
                for u in range(SUBLANES):
                    pltpu.make_async_copy(ys_ref.at[pl.ds(s_ref[0, k * tm + g * SUBLANES + u], 1)],
                                          buf.at[half, k, g, pl.ds(u, 1)], sem.at[half]).start()
                return c

            lax.fori_loop(0, tm // SUBLANES, start, 0, unroll=DMA_GROUP_UNROLL)

    @pl.when(i == 0)
    def _():
        gather(slot_ref, 0)

    @pl.when(i + 1 < pl.num_programs(0))
    def _():
        gather(nslot_ref, (i + 1) % 2)

    cur = i % 2
    for k in range(TOP_K):
        pltpu.make_async_copy(ys_ref.at[pl.ds(0, tm)], ys_ref.at[pl.ds(0, tm)], sem.at[cur]).wait()
    gate = gate_ref[...]
    rows = lambda k: buf[cur, k].reshape(h_ref.shape)
    y = rows(0) * gate[:, 0:1]
    for k in range(1, TOP_K):
        y = y + rows(k) * gate[:, k:k + 1]
    o_ref[...] = _layer_norm(alpha * h_ref[...] + y, g_ref[...], b_ref[...])


def _combine_ln(ys, slots, gates, h, g, b, alpha, tm=256):
    t, d = h.shape
    n_t = t // tm
    row = lambda i: (i, 0)
    fix = lambda i: (0, 0)
    slot_block = (None, 1, TOP_K * tm)
    return pl.pallas_call(
        functools.partial(_combine_kernel, alpha),
        grid=(n_t,),
        in_specs=[pl.BlockSpec(slot_block, lambda i: (i, 0, 0), memory_space=pltpu.SMEM),
                  pl.BlockSpec(slot_block, lambda i: (jnp.minimum(i + 1, n_t - 1), 0, 0),
                               memory_space=pltpu.SMEM),
                  pl.BlockSpec((tm, TOP_K), row), pl.BlockSpec((tm, d), row),
                  pl.BlockSpec((1, d), fix), pl.BlockSpec((1, d), fix),
                  pl.BlockSpec(memory_space=pl.ANY)],
        out_specs=pl.BlockSpec((tm, d), row),
        out_shape=jax.ShapeDtypeStruct((t, d), F32),
        scratch_shapes=[pltpu.VMEM((2, TOP_K, tm // SUBLANES, SUBLANES, d), F32), pltpu.SemaphoreType.DMA((2,))],
        compiler_params=_cparams("arbitrary"),
        name="moe_combine_ln",
    )(slots, slots, gates, h, g, b, ys)


def _tile_slots(slot, tm):
    bsz, _, seq = slot.shape
    s = slot.reshape(bsz, TOP_K, seq // tm, tm).transpose(0, 2, 1, 3)
    return s.reshape(bsz * (seq // tm), 1, TOP_K * tm)


def _moe_ln(h2, idx, gates, w1_all, b1, w2_all, b2, layer, ln_g, ln_b, alpha, tm_d=512, tm_c=256):
    t, d = h2.shape
    slot, counts = _expert_slots(idx)
    blocks_per_e = (counts[:, 0] + MOE_ROWS - 1) // MOE_ROWS
    block_end = jnp.cumsum(blocks_per_e)
    nb = t * TOP_K // MOE_ROWS + N_EXPERTS
    block_e = jnp.sum((block_end[None, :] <= jnp.arange(nb, dtype=I32)[:, None]).astype(I32), axis=1)
    block_e = jnp.minimum(block_e, N_EXPERTS - 1)
    n_used = block_end[-1:].astype(I32)
    tail_rows = jnp.where(blocks_per_e > 0, (block_end - 1) * MOE_ROWS, -1).astype(I32)
    xs = _dispatch(h2, _tile_slots(slot, tm_d), tail_rows, nb * MOE_ROWS, tm_d)
    ys = _expert_ffn(xs, block_e, n_used, w1_all, b1[:, None, :], w2_all, b2[:, None, :], layer)
    gates_t = gates.transpose(0, 2, 1).reshape(t, TOP_K)
    return _combine_ln(ys, _tile_slots(slot, tm_c), gates_t, h2, ln_g[None], ln_b[None], alpha, tm_c)


def kernel(x, mem, positions, ln_g, ln_b, a_w_in, a_lower_bounds, a_norm_g, a_w_out, b_w_in, b_w_out,
           c_a_re, c_a_im, c_log_dt, c_b_re, c_b_im, c_c_re, c_c_im, c_d, c_w_glu, m_w_kv, m_w_q, m_w_o,
           r_w, r_b, e_w1, e_b1, e_w2, e_b2):
    bsz, seq, d = x.shape
    depth = ln_g.shape[0]
    alpha = (2 * depth) ** 0.25
    t = bsz * seq
    lb = jax.nn.softmax(a_lower_bounds.astype(F32), axis=0)
    lb = jnp.cumsum(lb, axis=0) - lb[0]
    mem_kv = _matmul(mem.reshape(-1, d), m_w_kv.astype(BF16), BF16, 512, 1024)
    mem_kv = mem_kv.reshape(bsz, mem.shape[1], 2 * d)
    h = x.reshape(t, d)
    for layer in range(depth):
        kind, j = layer % N_MIXERS, layer // N_MIXERS
        g1, b1 = ln_g[layer, 0][None], ln_b[layer, 0][None]
        if kind == 0:
            h = _hgrn2_mixer_ln(h, a_w_in[j], a_norm_g[j], a_w_out[j], lb[layer], g1, b1, alpha, bsz, seq)
        elif kind == 1:
            h = _dilated_mixer_ln(h, positions, b_w_in[j], b_w_out[j], g1, b1, alpha, bsz, seq)
        else:
            h = _s5_mixer_ln(h, c_a_re[j], c_a_im[j], c_log_dt[j], c_b_re[j], c_b_im[j], c_c_re[j],
                             c_c_im[j], c_d[j], c_w_glu[j], g1, b1, alpha, bsz, seq)
        wr = jnp.pad(r_w[layer], ((0, 0), (0, LANES - N_EXPERTS))).astype(BF16)
        br = jnp.pad(r_b[layer], (0, LANES - N_EXPERTS))[None]
        h, idx, gates = _cross_attention_router(h, mem_kv, m_w_q[layer].astype(BF16), m_w_o[layer].astype(BF16),
                                                ln_g[layer, 1][None], ln_b[layer, 1][None], wr, br,
                                                alpha, bsz, seq)
        h = _moe_ln(h, idx, gates, e_w1, e_b1[layer], e_w2, e_b2[layer], layer,
                    ln_g[layer, 2], ln_b[layer, 2], alpha)
    return h.reshape(bsz, seq, d)
```

```python
import functools
import math

import jax
import jax.numpy as jnp
from jax import lax
from jax.experimental import pallas as pl
from jax.experimental.pallas import tpu as pltpu

F32 = jnp.float32
BF16 = jnp.bfloat16
I32 = jnp.int32

D_MODEL = 1024
N_MIXERS = 3
NORM_EPS = 1e-5
ROPE_THETA = 10000.0
MIN_FORGET = 1e-6
MASK_VALUE = -1e30

A_HEADS = 8
A_DIM = 128
A_CHUNK = 64
A_SUB = 8

B_GROUPS = ((128, 1), (512, 4), (2048, 16))
B_HEAD_DIM = 64
B_HEADS = 16
B_BLOCK = 128

C_GROUP_CH = 16
C_GROUPS = 64
C_STATE = 64
C_CHUNK = 16
C_TILE_GROUPS = 8

M_HEADS = 4
M_HEAD_DIM = 256

N_EXPERTS = 32
TOP_K = 4
SWIGLU_ALPHA = 1.702
SWIGLU_LIMIT = 7.0
MOE_ROWS = 512
SUBLANES = 8
DMA_GROUP_UNROLL = 4

LANES = 128
VMEM_LIMIT = 48 * 1024 * 1024


def _cparams(*sem):
    return pltpu.CompilerParams(dimension_semantics=sem, vmem_limit_bytes=VMEM_LIMIT)


def _layer_norm(z, g, b):
    mu = jnp.mean(z, axis=-1, keepdims=True)
    zc = z - mu
    var = jnp.mean(zc * zc, axis=-1, keepdims=True)
    return zc * lax.rsqrt(var + NORM_EPS) * g + b


def _sigmoid(x):
    return 1.0 / (1.0 + jnp.exp(-x))


def _dot(a, b):
    return jnp.dot(a, b, preferred_element_type=F32)


def _dot_nt(a, b):
    return lax.dot_general(a, b, (((1,), (1,)), ((), ())), preferred_element_type=F32)


def _dot_tn(a, b):
    return lax.dot_general(a, b, (((0,), (0,)), ((), ())), preferred_element_type=F32)


def _matmul_kernel(x_ref, w_ref, o_ref):
    o_ref[...] = _dot(x_ref[...].astype(BF16), w_ref[...]).astype(o_ref.dtype)


def _matmul(x, w, out_dtype, tm, tn):
    m, k = x.shape
    n = w.shape[1]
    return pl.pallas_call(
        _matmul_kernel,
        grid=(n // tn, m // tm),
        in_specs=[pl.BlockSpec((tm, k), lambda j, i: (i, 0)),
                  pl.BlockSpec((k, tn), lambda j, i: (0, j))],
        out_specs=pl.BlockSpec((tm, tn), lambda j, i: (i, j)),
        out_shape=jax.ShapeDtypeStruct((m, n), out_dtype),
        compiler_params=_cparams("parallel", "parallel"),
        name="matmul",
    )(x, w)


def _proj_res_ln_kernel(alpha, a_ref, w_ref, h_ref, g_ref, b_ref, o_ref):
    mix = _dot(a_ref[...].astype(BF16), w_ref[...])
    o_ref[...] = _layer_norm(alpha * h_ref[...] + mix, g_ref[...], b_ref[...])


def _proj_res_ln(a, w, h, g, b, alpha, tm=512):
    m, d = h.shape
    k = a.shape[1]
    row = lambda i: (i, 0)
    fix = lambda i: (0, 0)
    return pl.pallas_call(
        functools.partial(_proj_res_ln_kernel, alpha),
        grid=(m // tm,),
        in_specs=[pl.BlockSpec((tm, k), row), pl.BlockSpec((k, d), fix),
                  pl.BlockSpec((tm, d), row), pl.BlockSpec((1, d), fix), pl.BlockSpec((1, d), fix)],
        out_specs=pl.BlockSpec((tm, d), row),
        out_shape=jax.ShapeDtypeStruct((m, d), F32),
        compiler_params=_cparams("parallel"),
        name="proj_res_ln",
    )(a, w, h, g, b)


def _hgrn_in_kernel(x_ref, w_ref, lb_ref, q_ref, k_ref, v_ref, lf_ref, g_ref):
    d = x_ref.shape[1]
    xb = x_ref[...].astype(BF16)
    q = _dot(xb, w_ref[:, 0:d])
    q_ref[...] = q * _sigmoid(q)
    f = _dot(xb, w_ref[:, d:2 * d])
    lb = lb_ref[...]
    fg = lb + (1.0 - lb) * _sigmoid(f)
    lf_ref[...] = jnp.log(jnp.maximum(fg, MIN_FORGET))
    k_ref[...] = 1.0 - fg
    v_ref[...] = _dot(xb, w_ref[:, 2 * d:3 * d])
    g = _dot(xb, w_ref[:, 3 * d:4 * d])
    g_ref[...] = g * _sigmoid(g)


def _hgrn_in(x, w, lb, tm=256):
    m, d = x.shape
    row = lambda i: (i, 0)
    fix = lambda i: (0, 0)
    out = jax.ShapeDtypeStruct((m, d), F32)
    return pl.pallas_call(
        _hgrn_in_kernel,
        grid=(m // tm,),
        in_specs=[pl.BlockSpec((tm, d), row), pl.BlockSpec((d, 4 * d), fix), pl.BlockSpec((1, d), fix)],
        out_specs=[pl.BlockSpec((tm, d), row)] * 5,
        out_shape=[out] * 5,
        compiler_params=_cparams("parallel"),
        name="hgrn_in",
    )(x, w, lb)


def _rows_from(b, picks, span):
    parts = [jnp.broadcast_to(b[p:p + 1, :], (span, b.shape[1])) for p in picks]
    return parts[0] if len(parts) == 1 else jnp.concatenate(parts, axis=0)


def _hgrn_chunk_kernel(q_ref, k_ref, v_ref, lf_ref, g_ref, ng_ref, o_ref, state_ref):
    c = A_CHUNK
    n_sub = q_ref.shape[0] // c

    @pl.when(pl.program_id(1) == 0)
    def _():
        state_ref[...] = jnp.zeros_like(state_ref)

    ti = lax.broadcasted_iota(I32, (c, c), 0)
    si = lax.broadcasted_iota(I32, (c, c), 1)
    row = lax.broadcasted_iota(I32, (c, A_DIM), 0)
    halves = []
    half = c // 2
    while half >= A_SUB:
        halves.append(half)
        half //= 2
    level_masks = [((ti // (2 * hf)) == (si // (2 * hf))) & ((ti % (2 * hf)) >= hf) & ((si % (2 * hf)) < hf)
                   for hf in halves]
    diag_mask = ((ti // A_SUB) == (si // A_SUB)) & (si <= ti)

    def chunk(ci, carry):
        r0 = pl.multiple_of(ci * c, c)
        for hd in range(A_HEADS):
            cols = slice(hd * A_DIM, (hd + 1) * A_DIM)
            q = q_ref[pl.ds(r0, c), cols]
            k = k_ref[pl.ds(r0, c), cols]
            v = v_ref[pl.ds(r0, c), cols]
            b = lf_ref[pl.ds(r0, c), cols]
            sh = 1
            while sh < c:
                b = b + jnp.where(row >= sh, pltpu.roll(b, sh, axis=0), 0.0)
                sh *= 2
            scores = jnp.zeros((c, c), F32)
            for hf, mask in zip(halves, level_masks):
                ref = _rows_from(b, [blk * 2 * hf + hf - 1 for blk in range(c // (2 * hf))], 2 * hf)
                qt = (q * jnp.exp(jnp.minimum(b - ref, 0.0))).astype(BF16)
                kt = (k * jnp.exp(jnp.minimum(ref - b, 0.0))).astype(BF16)
                scores = scores + jnp.where(mask, _dot_nt(qt, kt), 0.0)
            lo = _rows_from(b, [blk * A_SUB for blk in range(c // A_SUB)], A_SUB)
            hi = _rows_from(b, [blk * A_SUB + A_SUB - 1 for blk in range(c // A_SUB)], A_SUB)
            mid = 0.5 * (lo + hi)
            qt = (q * jnp.exp(b - mid)).astype(BF16)
            kt = (k * jnp.exp(mid - b)).astype(BF16)
            scores = scores + jnp.where(diag_mask, _dot_nt(qt, kt), 0.0)

            st = state_ref[hd]
            b_end = b[c - 1:c, :]
            o = _dot(scores.astype(BF16), v.astype(BF16))
            o = o + _dot_nt((q * jnp.exp(b)).astype(BF16), st.astype(BF16))
            ke = (k * jnp.exp(b_end - b)).astype(BF16)
            state_ref[hd] = st * jnp.exp(b_end) + _dot_tn(v.astype(BF16), ke)

            o = o * lax.rsqrt(jnp.mean(o * o, axis=-1, keepdims=True) + NORM_EPS) * ng_ref[:, cols]
            o_ref[pl.ds(r0, c), cols] = (o * g_ref[pl.ds(r0, c), cols]).astype(o_ref.dtype)
        return carry

    lax.fori_loop(0, n_sub, chunk, 0)


def _hgrn_chunk(q, k, v, lf, g, ng, bsz, seq, tc=256):
    d = q.shape[1]
    n_t = seq // tc
    row = lambda b, i: (b * n_t + i, 0)
    return pl.pallas_call(
        _hgrn_chunk_kernel,
        grid=(bsz, n_t),
        in_specs=[pl.BlockSpec((tc, d), row)] * 5 + [pl.BlockSpec((1, d), lambda b, i: (0, 0))],
        out_specs=pl.BlockSpec((tc, d), row),
        out_shape=jax.ShapeDtypeStruct(q.shape, BF16),
        scratch_shapes=[pltpu.VMEM((A_HEADS, A_DIM, A_DIM), F32)],
        compiler_params=_cparams("parallel", "arbitrary"),
        name="hgrn_chunk",
    )(q, k, v, lf, g, ng)


def _hgrn2_mixer_ln(h, w_in, norm_g, w_out, lb, ln_g, ln_b, alpha, bsz, seq):
    q, k, v, lf, g = _hgrn_in(h, w_in.astype(BF16), lb.reshape(1, -1))
    o = _hgrn_chunk(q, k, v, lf, g, norm_g.reshape(1, -1), bsz, seq)
    return _proj_res_ln(o, w_out.astype(BF16), h, ln_g, ln_b, alpha)


def _rope_table_kernel(pos_ref, cos_ref, sin_ref):
    half = B_HEAD_DIM // 2
    lane = lax.broadcasted_iota(I32, cos_ref.shape, 1)
    j = (lane % half).astype(F32)
    inv_freq = jnp.exp(j * (-math.log(ROPE_THETA) / half))
    ang = pos_ref[...].astype(F32) * inv_freq
    cos_ref[...] = jnp.cos(ang)
    sin_ref[...] = jnp.where(lane < 2 * half, -1.0, 1.0) * jnp.sin(ang)


def _rope_tables(pos_col, tm=1024):
    t = pos_col.shape[0]
    out = jax.ShapeDtypeStruct((t, LANES), F32)
    return pl.pallas_call(
        _rope_table_kernel,
        grid=(t // tm,),
        in_specs=[pl.BlockSpec((tm, 1), lambda i: (i, 0))],
        out_specs=[pl.BlockSpec((tm, LANES), lambda i: (i, 0))] * 2,
        out_shape=[out, out],
        compiler_params=_cparams("parallel"),
        name="rope_tables",
    )(pos_col)


def _dil_attn_kernel(reach, q_ref, kp_ref, kc_ref, vp_ref, vc_ref, cq_ref, sq_ref, cp_ref, sp_ref,
                     o_ref, lse_ref):
    nq = q_ref.shape[0]
    i = pl.program_id(2)
    scale = B_HEAD_DIM ** -0.5
    half_lanes = LANES // 2

    def rope(x, cos, sin):
        return x * cos + pltpu.roll(x, half_lanes, axis=1) * sin

    qi = lax.broadcasted_iota(I32, (nq, 2 * nq), 0)
    ki = lax.broadcasted_iota(I32, (nq, 2 * nq), 1)
    dist = qi + nq - ki
    valid = (dist >= 0) & (dist <= reach) & ((ki >= nq) | (i > 0))
    lane = lax.broadcasted_iota(I32, (1, LANES), 1)
    head_a = ((lane // (B_HEAD_DIM // 2)) % 2) == 0
    v_head_a = lane < B_HEAD_DIM
    cq, sq = cq_ref[...], sq_ref[...]
    ck = jnp.concatenate([cp_ref[...], cq], axis=0)
    sk = jnp.concatenate([sp_ref[...], sq], axis=0)
    for pr in range(B_HEADS // 2):
        cols = slice(pr * LANES, (pr + 1) * LANES)
        q = rope(q_ref[:, cols], cq, sq) * scale
        k = rope(jnp.concatenate([kp_ref[:, cols], kc_ref[:, cols]], axis=0), ck, sk).astype(BF16)
        v = jnp.concatenate([vp_ref[:, cols], vc_ref[:, cols]], axis=0).astype(BF16)
        outs, lses = [], []
        for sel in (head_a, ~head_a):
            s = _dot_nt(jnp.where(sel, q, 0.0).astype(BF16), k)
            s = jnp.where(valid, s, MASK_VALUE)
            m = jnp.max(s, axis=-1, keepdims=True)
            p = jnp.exp(s - m)
            l = jnp.sum(p, axis=-1, keepdims=True)
            outs.append(_dot(p.astype(BF16), v) / l)
            lses.append(m + jnp.log(l))
        o_ref[:, cols] = jnp.where(v_head_a, outs[0], outs[1])
        lse_ref[:, cols] = jnp.where(v_head_a, lses[0], lses[1])


def _residue_perm(tm, dilation):
    per = tm // dilation
    j = jnp.arange(tm)
    src = (j % per) * dilation + j // per
    return (src[:, None] == jnp.arange(tm)[None, :]).astype(BF16)


def _regroup_rows(p, x):
    hi = x.astype(BF16)
    rest = x - hi.astype(F32)
    mid = rest.astype(BF16)
    lo = (rest - mid.astype(F32)).astype(BF16)
    return _dot(p, hi) + _dot(p, mid) + _dot(p, lo)


def _dil_proj_kernel(x_ref, p_ref, w_ref, o_ref):
    xs = _dot(p_ref[...], x_ref[...].astype(BF16)).astype(BF16)
    o_ref[...] = _dot(xs, w_ref[...]).reshape(o_ref.shape)


def _token_proj_kernel(x_ref, w_ref, o_ref):
    o_ref[...] = _dot(x_ref[...].astype(BF16), w_ref[...]).reshape(o_ref.shape)


def _dil_proj(h, w, dilation, bsz, seq, tm=512):
    d, n = w.shape
    n_t = seq // tm
    regroup = dilation > 1
    x_spec, w_spec = pl.BlockSpec((tm, d), lambda i: (i, 0)), pl.BlockSpec((d, n), lambda i: (0, 0))
    p_spec = pl.BlockSpec((tm, tm), lambda i: (0, 0))
    return pl.pallas_call(
        _dil_proj_kernel if regroup else _token_proj_kernel,
        grid=(bsz * n_t,),
        in_specs=[x_spec, p_spec, w_spec] if regroup else [x_spec, w_spec],
        out_specs=pl.BlockSpec((None, dilation, tm // dilation, n), lambda i: (i // n_t, 0, i % n_t, 0)),
        out_shape=jax.ShapeDtypeStruct((bsz, dilation, seq // dilation, n), F32),
        compiler_params=_cparams("parallel"),
        name=f"dil_proj_{dilation}",
    )(*((h, _residue_perm(tm, dilation), w) if regroup else (h, w)))


def _dil_attention(proj, cos, sin, dilation, reach, bsz, seq):
    d = D_MODEL
    sub = seq // dilation
    nb = sub // B_BLOCK
    blk = (None, None, B_BLOCK, d)
    tab = (None, None, B_BLOCK, LANES)
    prev = lambda i: jnp.maximum(i - 1, 0)
    in_specs = [
        pl.BlockSpec(blk, lambda b, r, i: (b, r, i, 0)),
        pl.BlockSpec(blk, lambda b, r, i: (b, r, prev(i), 1)),
        pl.BlockSpec(blk, lambda b, r, i: (b, r, i, 1)),
        pl.BlockSpec(blk, lambda b, r, i: (b, r, prev(i), 2)),
        pl.BlockSpec(blk, lambda b, r, i: (b, r, i, 2)),
        pl.BlockSpec(tab, lambda b, r, i: (b, r, i, 0)),
        pl.BlockSpec(tab, lambda b, r, i: (b, r, i, 0)),
        pl.BlockSpec(tab, lambda b, r, i: (b, r, prev(i), 0)),
        pl.BlockSpec(tab, lambda b, r, i: (b, r, prev(i), 0)),
    ]
    out_spec = pl.BlockSpec(blk, lambda b, r, i: (b, r, i, 0))
    out = jax.ShapeDtypeStruct((bsz, dilation, sub, d), F32)
    return pl.pallas_call(
        functools.partial(_dil_attn_kernel, reach),
        grid=(bsz, dilation, nb),
        in_specs=in_specs,
        out_specs=[out_spec, out_spec],
        out_shape=[out, out],
        compiler_params=_cparams("parallel", "parallel", "arbitrary"),
        name=f"dil_attn_{dilation}",
    )(proj, proj, proj, proj, proj, cos, sin, cos, sin)


def _dil_out_kernel(alpha, o0, o1, o2, l0, l1, l2, p1_ref, p2_ref, w_ref, h_ref, g_ref, b_ref, out_ref):
    tm, d = h_ref.shape
    o1v = _regroup_rows(p1_ref[...], o1[...].reshape(tm, d))
    l1v = _regroup_rows(p1_ref[...], l1[...].reshape(tm, d))
    o2v = _regroup_rows(p2_ref[...], o2[...].reshape(tm, d))
    l2v = _regroup_rows(p2_ref[...], l2[...].reshape(tm, d))
    l0v = l0[...]
    m = jnp.maximum(jnp.maximum(l0v, l1v), l2v)
    e0, e1, e2 = jnp.exp(l0v - m), jnp.exp(l1v - m), jnp.exp(l2v - m)
    o = (e0 * o0[...] + e1 * o1v + e2 * o2v) / (e0 + e1 + e2)
    mix = _dot(o.astype(BF16), w_ref[...])
    out_ref[...] = _layer_norm(alpha * h_ref[...] + mix, g_ref[...], b_ref[...])


def _dil_out(outs, lses, w, h, g, b, alpha, bsz, seq, tm=256):
    m, d = h.shape
    n_t = seq // tm
    row = lambda i: (i, 0)
    fix = lambda i: (0, 0)

    def grouped(a):
        dil = a.shape[1]
        return pl.BlockSpec((None, dil, tm // dil, d), lambda i: (i // n_t, 0, i % n_t, 0))

    perms = [_residue_perm(tm, a.shape[1]).T for a in outs[1:]]
    return pl.pallas_call(
        functools.partial(_dil_out_kernel, alpha),
        grid=(m // tm,),
        in_specs=[pl.BlockSpec((tm, d), row), grouped(outs[1]), grouped(outs[2]),
                  pl.BlockSpec((tm, d), row), grouped(lses[1]), grouped(lses[2]),
                  pl.BlockSpec((tm, tm), fix), pl.BlockSpec((tm, tm), fix),
                  pl.BlockSpec((d, d), fix), pl.BlockSpec((tm, d), row),
                  pl.BlockSpec((1, d), fix), pl.BlockSpec((1, d), fix)],
        out_specs=pl.BlockSpec((tm, d), row),
        out_shape=jax.ShapeDtypeStruct((m, d), F32),
        compiler_params=_cparams("parallel"),
        name="dil_out",
    )(outs[0].reshape(m, d), outs[1], outs[2], lses[0].reshape(m, d), lses[1], lses[2], *perms, w, h, g, b)


def _rope_col_perm():
    half = B_HEAD_DIM // 2
    idx = []
    for pr in range(B_HEADS // 2):
        a, b = 2 * pr * B_HEAD_DIM, (2 * pr + 1) * B_HEAD_DIM
        idx += list(range(a, a + half)) + list(range(b, b + half))
        idx += list(range(a + half, a + 2 * half)) + list(range(b + half, b + 2 * half))
    return jnp.asarray(idx, dtype=I32)


def _dilated_mixer_ln(h, positions, w_in, w_out, ln_g, ln_b, alpha, bsz, seq):
    d = D_MODEL
    perm = _rope_col_perm()
    w = w_in.reshape(d, len(B_GROUPS), 3, d)
    w = jnp.concatenate([w[:, :, 0:2, :][..., perm], w[:, :, 2:3, :]], axis=2)
    w = w.reshape(d, len(B_GROUPS), 3 * d).astype(BF16)
    outs, lses = [], []
    for gi, (window, dilation) in enumerate(B_GROUPS):
        sub = seq // dilation
        proj = _dil_proj(h, w[:, gi, :], dilation, bsz, seq)
        pos =positions.reshape(bsz, sub, dilation).transpose(0, 2, 1).reshape(-1, 1)
        cos, sin = _rope_tables(pos)
        tab = (bsz, dilation, sub, LANES)
        o, l = _dil_attention(proj, cos.reshape(tab), sin.reshape(tab), dilation, window // dilation, bsz, seq)
        outs.append(o)
        lses.append(l)
    return _dil_out(outs, lses, w_out.astype(BF16), h, ln_g, ln_b, alpha, bsz, seq)


def _s5_operators(a_re, a_im, log_dt, b_re, b_im, c_re, c_im):
    hp = lax.Precision.HIGHEST
    L = C_CHUNK
    dt = jnp.exp(log_dt)[:, None]
    lam_re = jnp.exp(a_re * dt) * jnp.cos(a_im * dt)
    lam_im = jnp.exp(a_re * dt) * jnp.sin(a_im * dt)
    den = a_re * a_re + a_im * a_im
    fr = ((lam_re - 1.0) * a_re + lam_im * a_im) / den
    fi = (lam_im * a_re - (lam_re - 1.0) * a_im) / den
    bb_re = fr[..., None] * b_re - fi[..., None] * b_im
    bb_im = fr[..., None] * b_im + fi[..., None] * b_re
    tau = jnp.arange(L + 1, dtype=F32)[:, None, None]
    mag = jnp.exp(a_re * dt * tau)
    pw_re = mag * jnp.cos(a_im * dt * tau)
    pw_im = mag * jnp.sin(a_im * dt * tau)
    cl_re = c_re[None] * pw_re[:, :, None, :] - c_im[None] * pw_im[:, :, None, :]
    cl_im = c_re[None] * pw_im[:, :, None, :] + c_im[None] * pw_re[:, :, None, :]
    kern = (jnp.einsum('tgon,gni->tgoi', cl_re, bb_re, precision=hp)
            - jnp.einsum('tgon,gni->tgoi', cl_im, bb_im, precision=hp)).transpose(0, 1, 3, 2)
    t_idx = jnp.arange(L)
    rev = pw_re[L - 1 - t_idx], pw_im[L - 1 - t_idx]
    p_re = rev[0][..., None] * bb_re[None] - rev[1][..., None] * bb_im[None]
    p_im = rev[0][..., None] * bb_im[None] + rev[1][..., None] * bb_re[None]
    p_op = jnp.concatenate([p_re, p_im], axis=2).transpose(0, 1, 3, 2)
    q_op = jnp.concatenate([cl_re[1:], -cl_im[1:]], axis=3).transpose(0, 1, 3, 2)
    lam_l = jnp.concatenate([pw_re[L], pw_im[L]], axis=-1)
    return kern, p_op, q_op, lam_l


def _s5_tile_operators(kern, p_op, q_op, lam_l, n_chunks):
    gt = C_TILE_GROUPS
    nt = C_GROUPS // gt
    L, ch, ns = C_CHUNK, C_GROUP_CH, 2 * C_STATE
    r128 = jnp.arange(LANES)[:, None]
    rows_q = jnp.arange(gt * ns)[:, None]
    own_d = (r128 // ch == jnp.arange(LANES)[None, :] // ch).astype(F32)
    own_p = (r128 // ch == jnp.arange(gt * ns)[None, :] // ns).astype(F32)
    own_q = (rows_q // ns == jnp.arange(LANES)[None, :] // ch).astype(F32)
    lag_blk = jnp.tile(kern[:L].reshape(L, nt, LANES, ch), (1, 1, 1, gt)) * own_d
    zero = jnp.zeros_like(lag_blk[0])
    m_t = jnp.concatenate(
        [jnp.concatenate([lag_blk[t - s] if t >= s else zero for t in range(L)], axis=-1) for s in range(L)],
        axis=1)
    p_blk = jnp.tile(p_op.reshape(L, nt, LANES, ns), (1, 1, 1, gt)) * own_p
    p_t = p_blk.transpose(1, 0, 2, 3).reshape(nt, L * LANES, gt * ns)
    q_blk = jnp.tile(q_op.reshape(L, nt, gt * ns, ch), (1, 1, 1, gt)) * own_q
    q_t = jnp.concatenate([q_blk[t] for t in range(L)], axis=-1)
    m_t, p_t, q_t = m_t.astype(BF16), p_t.astype(BF16), q_t.astype(BF16)
    lr, li = lam_l[:, :C_STATE], lam_l[:, C_STATE:]
    las, lbs = [], []
    k = 1
    while k < n_chunks:
        las.append(jnp.concatenate([lr, lr], axis=-1).reshape(nt, gt * ns))
        lbs.append(jnp.concatenate([-li, li], axis=-1).reshape(nt, gt * ns))
        lr, li = lr * lr - li * li, 2.0 * lr * li
        k *= 2
    return m_t, p_t, q_t, jnp.stack(las, axis=1), jnp.stack(lbs, axis=1)


def _s5_local_kernel(u_ref, m_ref, p_ref, y_ref, x_ref):
    n = u_ref.shape[0] // C_CHUNK
    u = jnp.concatenate([u_ref[pl.ds(s, n, stride=C_CHUNK), :].astype(BF16) for s in range(C_CHUNK)], axis=1)
    y = _dot(u, m_ref[...])
    for t in range(C_CHUNK):
        y_ref[pl.ds(t, n, stride=C_CHUNK), :] = y[:, t * LANES:(t + 1) * LANES]
    x_ref[...] = _dot(u, p_ref[...])


def _s5_local(h, m_t, p_t, bsz, seq):
    nt, kf, ns = p_t.shape
    n = seq // C_CHUNK
    return pl.pallas_call(
        _s5_local_kernel,
        grid=(nt, bsz),
        in_specs=[pl.BlockSpec((seq, LANES), lambda j, b: (b, j)),
                  pl.BlockSpec((None, kf, kf), lambda j, b: (j, 0, 0)),
                  pl.BlockSpec((None, kf, ns), lambda j, b: (j, 0, 0))],
        out_specs=[pl.BlockSpec((seq, LANES), lambda j, b: (b, j)),
                   pl.BlockSpec((None, n, ns), lambda j, b: (j, b, 0))],
        out_shape=[jax.ShapeDtypeStruct(h.shape, F32), jax.ShapeDtypeStruct((nt, bsz * n, ns), F32)],
        compiler_params=_cparams("parallel", "parallel"),
        name="s5_local",
    )(h, m_t, p_t)


def _s5_state_kernel(y_ref, x_ref, q_ref, la_ref, lb_ref, o_ref):
    n = x_ref.shape[0]
    row = lax.broadcasted_iota(I32, (n, LANES), 0)
    cols = []
    for g in range(C_TILE_GROUPS):
        lanes = slice(g * LANES, (g + 1) * LANES)
        x = x_ref[:, lanes]
        k, sh = 0, 1
        while sh < n:
            v = jnp.where(row >= sh, pltpu.roll(x, sh, axis=0), 0.0)
            x = x + la_ref[k:k + 1, lanes] * v + lb_ref[k:k + 1, lanes] * pltpu.roll(v, C_STATE, axis=1)
            k += 1
            sh *= 2
        cols.append(jnp.where(row >= 1, pltpu.roll(x, 1, axis=0), 0.0).astype(BF16))
    yq = _dot(jnp.concatenate(cols, axis=1), q_ref[...])
    for t in range(C_CHUNK):
        rows = pl.ds(t, n, stride=C_CHUNK)
        o_ref[rows, :] = y_ref[rows, :] + yq[:, t * LANES:(t + 1) * LANES]


def _s5_state_out(y, x_loc, q_t, la, lb, bsz, seq):
    nt, ns, kf = q_t.shape
    n = seq // C_CHUNK
    nk = la.shape[1]
    return pl.pallas_call(
        _s5_state_kernel,
        grid=(nt, bsz),
        in_specs=[pl.BlockSpec((seq, LANES), lambda j, b: (b, j)),
                  pl.BlockSpec((None, n, ns), lambda j, b: (j, b, 0)),
                  pl.BlockSpec((None, ns, kf), lambda j, b: (j, 0, 0)),
                  pl.BlockSpec((None, nk, ns), lambda j, b: (j, 0, 0)),
                  pl.BlockSpec((None, nk, ns), lambda j, b: (j, 0, 0))],
        out_specs=pl.BlockSpec((seq, LANES), lambda j, b: (b, j)),
        out_shape=jax.ShapeDtypeStruct(y.shape, F32),
        compiler_params=_cparams("parallel", "parallel"),
        name="s5_state_out",
    )(y, x_loc, q_t, la, lb)


def _s5_out_kernel(alpha, y_ref, h_ref, dskip_ref, w_ref, g_ref, b_ref, o_ref):
    d = h_ref.shape[1]
    h = h_ref[...]
    z = y_ref[...] + dskip_ref[...] * h
    z = 0.5 * z * (1.0 + jnp.tanh(math.sqrt(2.0 / math.pi) * (z + 0.044715 * (z * z * z))))
    zb = z.astype(BF16)
    val = _dot(zb, w_ref[:, 0:d])
    gate = _dot(zb, w_ref[:, d:2 * d])
    o_ref[...] = _layer_norm(alpha * h + val * _sigmoid(gate), g_ref[...], b_ref[...])


def _s5_out(y, h, d_skip, w_glu, g, b, alpha, tm=256):
    m, d = h.shape
    row = lambda i: (i, 0)
    fix = lambda i: (0, 0)
    return pl.pallas_call(
        functools.partial(_s5_out_kernel, alpha),
        grid=(m // tm,),
        in_specs=[pl.BlockSpec((tm, d), row), pl.BlockSpec((tm, d), row), pl.BlockSpec((1, d), fix),
                  pl.BlockSpec((d, 2 * d), fix), pl.BlockSpec((1, d), fix), pl.BlockSpec((1, d), fix)],
        out_specs=pl.BlockSpec((tm, d), row),
        out_shape=jax.ShapeDtypeStruct((m, d), F32),
        compiler_params=_cparams("parallel"),
        name="s5_out",
    )(y, h, d_skip, w_glu, g, b)


def _s5_mixer_ln(h, a_re, a_im, log_dt, b_re, b_im, c_re, c_im, d_skip, w_glu, ln_g, ln_b, alpha, bsz, seq):
    ops = _s5_operators(a_re, a_im, log_dt, b_re, b_im, c_re, c_im)
    m_t, p_t, q_t, la, lb = _s5_tile_operators(*ops, seq // C_CHUNK)
    y_loc, x_loc = _s5_local(h, m_t, p_t, bsz, seq)
    y = _s5_state_out(y_loc, x_loc, q_t, la, lb, bsz, seq)
    return _s5_out(y, h, d_skip.reshape(1, -1), w_glu.astype(BF16), ln_g, ln_b, alpha)


def _cross_kernel(alpha, h_ref, kv_ref, wq_ref, wo_ref, g_ref, b_ref, wr_ref, br_ref,
                  o_ref, idx_ref, gate_ref):
    d = h_ref.shape[1]
    h = h_ref[...]
    q = (_dot(h.astype(BF16), wq_ref[...]) * (M_HEAD_DIM ** -0.5)).astype(BF16)
    heads = []
    for hd in range(M_HEADS):
        cols = slice(hd * M_HEAD_DIM, (hd + 1) * M_HEAD_DIM)
        s = _dot_nt(q[:, cols], kv_ref[:, cols])
        p = jnp.exp(s - jnp.max(s, axis=-1, keepdims=True))
        l = jnp.sum(p, axis=-1, keepdims=True)
        heads.append((_dot(p.astype(BF16), kv_ref[:, d + hd * M_HEAD_DIM:d + (hd + 1) * M_HEAD_DIM]) / l)
                     .astype(BF16))
    att = _dot(jnp.concatenate(heads, axis=-1), wo_ref[...])
    h2 = _layer_norm(alpha * h + att, g_ref[...], b_ref[...])
    o_ref[...] = h2
    logits = (_dot(h2.astype(BF16), wr_ref[...]) + br_ref[...]).T[0:N_EXPERTS, :]
    eidx = lax.broadcasted_iota(I32, logits.shape, 0)
    vals, idxs = [], []
    for _ in range(TOP_K):
        m = jnp.max(logits, axis=0, keepdims=True)
        pick = jnp.min(jnp.where(logits == m, eidx, N_EXPERTS), axis=0, keepdims=True)
        vals.append(m)
        idxs.append(pick)
        logits = jnp.where(eidx == pick, -jnp.inf, logits)
    es = [jnp.exp(v - vals[0]) for v in vals]
    tot = es[0] + es[1] + es[2] + es[3]
    idx_ref[...] = jnp.concatenate(idxs, axis=0)
    gate_ref[...] = jnp.concatenate([e / tot for e in es], axis=0)


def _cross_attention_router(h, mem_kv, wq, wo, g, b, wr, br, alpha, bsz, seq, tm=512):
    d = D_MODEL
    n_t = seq // tm
    mlen = mem_kv.shape[1]
    row = lambda bi, i: (bi * n_t + i, 0)
    fix = lambda bi, i: (0, 0)
    col = lambda bi, i: (bi, 0, i)
    return pl.pallas_call(
        functools.partial(_cross_kernel, alpha),
        grid=(bsz, n_t),
        in_specs=[pl.BlockSpec((tm, d), row),
                  pl.BlockSpec((None, mlen, 2 * d), lambda bi, i: (bi, 0, 0)),
                  pl.BlockSpec((d, d), fix), pl.BlockSpec((d, d), fix),
                  pl.BlockSpec((1, d), fix), pl.BlockSpec((1, d), fix),
                  pl.BlockSpec((d, LANES), fix), pl.BlockSpec((1, LANES), fix)],
        out_specs=[pl.BlockSpec((tm, d), row),
                   pl.BlockSpec((None, TOP_K, tm), col), pl.BlockSpec((None, TOP_K, tm), col)],
        out_shape=[jax.ShapeDtypeStruct((bsz * seq, d), F32),
                   jax.ShapeDtypeStruct((bsz, TOP_K, seq), I32),
                   jax.ShapeDtypeStruct((bsz, TOP_K, seq), F32)],
        compiler_params=_cparams("parallel", "parallel"),
        name="cross_attn_router",
    )(h, mem_kv, wq, wo, g, b, wr, br)


def _slot_kernel(idx_ref, slot_ref, count_ref, run_ref, base_ref):
    phase = pl.program_id(0)
    first = (pl.program_id(1) == 0) & (pl.program_id(2) == 0)

    @pl.when(first & (phase == 0))
    def _():
        run_ref[...] = jnp.zeros_like(run_ref)

    @pl.when(first & (phase == 1))
    def _():
        counts = run_ref[...]
        count_ref[...] = counts.astype(I32)
        padded = jnp.ceil(counts * (1.0 / MOE_ROWS)) * MOE_ROWS
        row = lax.broadcasted_iota(I32, padded.shape, 0)
        ends = padded
        sh = 1
        while sh < N_EXPERTS:
            ends = ends + jnp.where(row >= sh, pltpu.roll(ends, sh, axis=0), 0.0)
            sh *= 2
        base_ref[...] = ends - padded
        run_ref[...] = jnp.zeros_like(run_ref)

    tn = idx_ref.shape[1]
    idx = idx_ref[...]
    eidx = lax.broadcasted_iota(I32, (N_EXPERTS, tn), 0)
    hits = [eidx == idx[k:k + 1, :] for k in range(TOP_K)]
    onehot = jnp.zeros((N_EXPERTS, tn), F32)
    for hit in hits:
        onehot = onehot + jnp.where(hit, 1.0, 0.0)

    @pl.when(phase == 1)
    def _():
        earlier = lax.broadcasted_iota(I32, (tn, tn), 0) < lax.broadcasted_iota(I32, (tn, tn), 1)
        before = _dot(onehot.astype(BF16), jnp.where(earlier, 1.0, 0.0).astype(BF16))
        before = before + run_ref[:, 0:1] + base_ref[:, 0:1]
        slots = [jnp.sum(jnp.where(hit, before, 0.0), axis=0, keepdims=True) for hit in hits]
        slot_ref[...] = jnp.concatenate(slots, axis=0).astype(I32)

    run_ref[...] = run_ref[...] + jnp.sum(onehot, axis=1, keepdims=True)


def _expert_slots(idx, tn=512):
    bsz, _, seq = idx.shape
    return pl.pallas_call(
        _slot_kernel,
        grid=(2, bsz, seq // tn),
        in_specs=[pl.BlockSpec((None, TOP_K, tn), lambda p, bi, i: (bi, 0, i))],
        out_specs=[pl.BlockSpec((None, TOP_K, tn), lambda p, bi, i: (bi * p, 0, i * p)),
                   pl.BlockSpec((N_EXPERTS, LANES), lambda p, bi, i: (0, 0))],
        out_shape=[jax.ShapeDtypeStruct(idx.shape, I32), jax.ShapeDtypeStruct((N_EXPERTS, LANES), I32)],
        scratch_shapes=[pltpu.VMEM((N_EXPERTS, LANES), F32), pltpu.VMEM((N_EXPERTS, LANES), F32)],
        compiler_params=_cparams("arbitrary", "arbitrary", "arbitrary"),
        name="expert_slots",
    )(idx)


def _dispatch_kernel(tail_ref, slot_ref, x_ref, xs_ref, zero_ref, sem, zsem):
    tm = x_ref.shape[0] * SUBLANES

    @pl.when(pl.program_id(0) == 0)
    def _():
        zero_ref[...] = jnp.zeros_like(zero_ref)

        def clear(e):
            row = pl.multiple_of(jnp.maximum(tail_ref[e], 0), MOE_ROWS)
            return pltpu.make_async_copy(zero_ref, xs_ref.at[pl.ds(row, MOE_ROWS)], zsem)

        for e in range(N_EXPERTS):
            @pl.when(tail_ref[e] >= 0)
            def _(e=e):
                clear(e).start()
        for e in range(N_EXPERTS):
            @pl.when(tail_ref[e] >= 0)
            def _(e=e):
                clear(e).wait()

    for k in range(TOP_K):
        def start(g, c, k=k):
            for u in range(SUBLANES):
                pltpu.make_async_copy(x_ref.at[g, pl.ds(u, 1)],
                                      xs_ref.at[pl.ds(slot_ref[0, k * tm + g * SUBLANES + u], 1)], sem).start()
            return c

        lax.fori_loop(0, tm // SUBLANES, start, 0, unroll=DMA_GROUP_UNROLL)
    for k in range(TOP_K):
        pltpu.make_async_copy(xs_ref.at[pl.ds(0, tm)], xs_ref.at[pl.ds(0, tm)], sem).wait()


def _dispatch(x, slots, tail_rows, n_rows, tm=512):
    t, d = x.shape
    grid_spec = pltpu.PrefetchScalarGridSpec(
        num_scalar_prefetch=1,
        grid=(t // tm,),
        in_specs=[pl.BlockSpec((None, 1, TOP_K * tm), lambda i, tail: (i, 0, 0), memory_space=pltpu.SMEM),
                  pl.BlockSpec((tm // SUBLANES, SUBLANES, d), lambda i, tail: (i, 0, 0))],
        out_specs=pl.BlockSpec(memory_space=pl.ANY),
        scratch_shapes=[pltpu.VMEM((MOE_ROWS, d), x.dtype), pltpu.SemaphoreType.DMA(()),
                        pltpu.SemaphoreType.DMA(())],
    )
    return pl.pallas_call(
        _dispatch_kernel,
        grid_spec=grid_spec,
        out_shape=jax.ShapeDtypeStruct((n_rows, d), x.dtype),
        compiler_params=_cparams("arbitrary"),
        name="moe_dispatch",
    )(tail_rows, slots, x.reshape(t // SUBLANES, SUBLANES, d))


def _expert_ffn_kernel(be_ref, nu_ref, x_ref, w1_ref, b1_ref, w2_ref, b2_ref, y_ref, w1b_ref, w2b_ref):
    d = x_ref.shape[1]
    i = pl.program_id(0)

    @pl.when((i == 0) | (be_ref[i] != be_ref[jnp.maximum(i - 1, 0)]))
    def _():
        w1b_ref[...] = w1_ref[...].astype(BF16)
        w2b_ref[...] = w2_ref[...].astype(BF16)

    @pl.when(i < nu_ref[0])
    def _():
        xb = x_ref[...].astype(BF16)
        glu = jnp.minimum(_dot(xb, w1b_ref[:, 0:d]) + b1_ref[:, 0:d], SWIGLU_LIMIT)
        lin = jnp.clip(_dot(xb, w1b_ref[:, d:2 * d]) + b1_ref[:, d:2 * d], -SWIGLU_LIMIT, SWIGLU_LIMIT)
        act = glu * _sigmoid(SWIGLU_ALPHA * glu) * (lin + 1.0)
        y_ref[...] = _dot(act.astype(BF16), w2b_ref[...]) + b2_ref[...]

    @pl.when(i >= nu_ref[0])
    def _():
        y_ref[...] = jnp.zeros_like(y_ref)


def _expert_ffn(xs, block_e, n_used, w1_all, b1, w2_all, b2, layer):
    n_rows, d = xs.shape
    nb = n_rows // MOE_ROWS
    wsel = lambda i, be, nu: (be[i], 0, 0)
    wsel_l = lambda i, be, nu: (layer, be[i], 0, 0)
    xsel = lambda i, be, nu: (jnp.minimum(i, nu[0] - 1), 0)
    grid_spec = pltpu.PrefetchScalarGridSpec(
        num_scalar_prefetch=2,
        grid=(nb,),
        in_specs=[pl.BlockSpec((MOE_ROWS, d), xsel),
                  pl.BlockSpec((None, None, d, 2 * d), wsel_l), pl.BlockSpec((None, 1, 2 * d), wsel),
                  pl.BlockSpec((None, None, d, d), wsel_l), pl.BlockSpec((None, 1, d), wsel)],
        out_specs=pl.BlockSpec((MOE_ROWS, d), lambda i, be, nu: (i, 0)),
        scratch_shapes=[pltpu.VMEM((d, 2 * d), BF16), pltpu.VMEM((d, d), BF16)],
    )
    return pl.pallas_call(
        _expert_ffn_kernel,
        grid_spec=grid_spec,
        out_shape=jax.ShapeDtypeStruct((n_rows, d), F32),
        compiler_params=_cparams("arbitrary"),
        name="expert_ffn",
    )(block_e, n_used, xs, w1_all, b1, w2_all, b2)


def _combine_kernel(alpha, slot_ref, nslot_ref, gate_ref, h_ref, g_ref, b_ref, ys_ref, o_ref, buf, sem):
    tm = h_ref.shape[0]
    i = pl.program_id(0)

    def gather(s_ref, half):
        for k in range(TOP_K):
            def start(g, c, k=k):
                for u in range(SUBLANES):
                    pltpu.make_async_copy(ys_ref.at[pl.ds(s_ref[0, k * tm + g * SUBLANES + u], 1)],
                                          buf.at[half, k, g, pl.ds(u, 1)], sem.at[half]).start()
                return c

            lax.fori_loop(0, tm // SUBLANES, start, 0, unroll=DMA_GROUP_UNROLL)

    @pl.when(i == 0)
    def _():
        gather(slot_ref, 0)

    @pl.when(i + 1 < pl.num_programs(0))
    def _():
        gather(nslot_ref, (i + 1) % 2)

    cur = i % 2
    for k in range(TOP_K):
        pltpu.make_async_copy(ys_ref.at[pl.ds(0, tm)], ys_ref.at[pl.ds(0, tm)], sem.at[cur]).wait()
    gate = gate_ref[...]
    rows = lambda k: buf[cur, k].reshape(h_ref.shape)
    y = rows(0) * gate[:, 0:1]
    for k in range(1, TOP_K):
        y = y + rows(k) * gate[:, k:k + 1]
    o_ref[...] = _layer_norm(alpha * h_ref[...] + y, g_ref[...], b_ref[...])


def _combine_ln(ys, slots, gates, h, g, b, alpha, tm=256):
    t, d = h.shape
    n_t = t // tm
    row = lambda i: (i, 0)
    fix = lambda i: (0, 0)
    slot_block = (None, 1, TOP_K * tm)
    return pl.pallas_call(
        functools.partial(_combine_kernel, alpha),
        grid=(n_t,),
        in_specs=[pl.BlockSpec(slot_block, lambda i: (i, 0, 0), memory_space=pltpu.SMEM),
                  pl.BlockSpec(slot_block, lambda i: (jnp.minimum(i + 1, n_t - 1), 0, 0),
                               memory_space=pltpu.SMEM),
                  pl.BlockSpec((tm, TOP_K), row), pl.BlockSpec((tm, d), row),
                  pl.BlockSpec((1, d), fix), pl.BlockSpec((1, d), fix),
                  pl.BlockSpec(memory_space=pl.ANY)],
        out_specs=pl.BlockSpec((tm, d), row),
        out_shape=jax.ShapeDtypeStruct((t, d), F32),
        scratch_shapes=[pltpu.VMEM((2, TOP_K, tm // SUBLANES, SUBLANES, d), F32), pltpu.SemaphoreType.DMA((2,))],
        compiler_params=_cparams("arbitrary"),
        name="moe_combine_ln",
    )(slots, slots, gates, h, g, b, ys)


def _tile_slots(slot, tm):
    bsz, _, seq = slot.shape
    s = slot.reshape(bsz, TOP_K, seq // tm, tm).transpose(0, 2, 1, 3)
    return s.reshape(bsz * (seq // tm), 1, TOP_K * tm)


def _moe_ln(h2, idx, gates, w1_all, b1, w2_all, b2, layer, ln_g, ln_b, alpha, tm_d=512, tm_c=256):
    t, d = h2.shape
    slot, counts = _expert_slots(idx)
    blocks_per_e = (counts[:, 0] + MOE_ROWS - 1) // MOE_ROWS
    block_end = jnp.cumsum(blocks_per_e)
    nb = t * TOP_K // MOE_ROWS + N_EXPERTS
    block_e = jnp.sum((block_end[None, :] <= jnp.arange(nb, dtype=I32)[:, None]).astype(I32), axis=1)
    block_e = jnp.minimum(block_e, N_EXPERTS - 1)
    n_used = block_end[-1:].astype(I32)
    tail_rows = jnp.where(blocks_per_e > 0, (block_end - 1) * MOE_ROWS, -1).astype(I32)
    xs = _dispatch(h2, _tile_slots(slot, tm_d), tail_rows, nb * MOE_ROWS, tm_d)
    ys = _expert_ffn(xs, block_e, n_used, w1_all, b1[:, None, :], w2_all, b2[:, None, :], layer)
    gates_t = gates.transpose(0, 2, 1).reshape(t, TOP_K)
    return _combine_ln(ys, _tile_slots(slot, tm_c), gates_t, h2, ln_g[None], ln_b[None], alpha, tm_c)


def kernel(x, mem, positions, ln_g, ln_b, a_w_in, a_lower_bounds, a_norm_g, a_w_out, b_w_in, b_w_out,
           c_a_re, c_a_im, c_log_dt, c_b_re, c_b_im, c_c_re, c_c_im, c_d, c_w_glu, m_w_kv, m_w_q, m_w_o,
           r_w, r_b, e_w1, e_b1, e_w2, e_b2):
    bsz, seq, d = x.shape
    depth = ln_g.shape[0]
    alpha = (2 * depth) ** 0.25
    t = bsz * seq
    lb = jax.nn.softmax(a_lower_bounds.astype(F32), axis=0)
    lb = jnp.cumsum(lb, axis=0) - lb[0]
    mem_kv = _matmul(mem.reshape(-1, d), m_w_kv.astype(BF16), BF16, 512, 1024)
    mem_kv = mem_kv.reshape(bsz, mem.shape[1], 2 * d)
    h = x.reshape(t, d)
    for layer in range(depth):
        kind, j = layer % N_MIXERS, layer // N_MIXERS
        g1, b1 = ln_g[layer, 0][None], ln_b[layer, 0][None]
        if kind == 0:
            h = _hgrn2_mixer_ln(h, a_w_in[j], a_norm_g[j], a_w_out[j], lb[layer], g1, b1, alpha, bsz, seq)
        elif kind == 1:
            h = _dilated_mixer_ln(h, positions, b_w_in[j], b_w_out[j], g1, b1, alpha, bsz, seq)
        else:
            h = _s5_mixer_ln(h, c_a_re[j], c_a_im[j], c_log_dt[j], c_b_re[j], c_b_im[j], c_c_re[j],
                             c_c_im[j], c_d[j], c_w_glu[j], g1, b1, alpha, bsz, seq)
        wr = jnp.pad(r_w[layer], ((0, 0), (0, LANES - N_EXPERTS))).astype(BF16)
        br = jnp.pad(r_b[layer], (0, LANES - N_EXPERTS))[None]
        h, idx, gates = _cross_attention_router(h, mem_kv, m_w_q[layer].astype(BF16), m_w_o[layer].astype(BF16),
                                                ln_g[layer, 1][None], ln_b[layer, 1][None], wr, br,
                                                alpha, bsz, seq)
        h = _moe_ln(h, idx, gates, e_w1, e_b1[layer], e_w2, e_b2[layer], layer,
                    ln_g[layer, 2], ln_b[layer, 2], alpha)
    return h.reshape(bsz, seq, d)
```

```python
import functools
import math

import jax
import jax.numpy as jnp
from jax import lax
from jax.experimental import pallas as pl
from jax.experimental.pallas import tpu as pltpu

F32 = jnp.float32
BF16 = jnp.bfloat16
I32 = jnp.int32

D_MODEL = 1024
N_MIXERS = 3
NORM_EPS = 1e-5
ROPE_THETA = 10000.0
MIN_FORGET = 1e-6
MASK_VALUE = -1e30

A_HEADS = 8
A_DIM = 128
A_CHUNK = 64
A_SUB = 8

B_GROUPS = ((128, 1), (512, 4), (2048, 16))
B_HEAD_DIM = 64
B_HEADS = 16
B_BLOCK = 128

C_GROUP_CH = 16
C_GROUPS = 64
C_STATE = 64
C_CHUNK = 16
C_TILE_GROUPS = 8

M_HEADS = 4
M_HEAD_DIM = 256

N_EXPERTS = 32
TOP_K = 4
SWIGLU_ALPHA = 1.702
SWIGLU_LIMIT = 7.0
MOE_ROWS = 512
SUBLANES = 8
DMA_GROUP_UNROLL = 4

LANES = 128
VMEM_LIMIT = 48 * 1024 * 1024


def _cparams(*sem):
    return pltpu.CompilerParams(dimension_semantics=sem, vmem_limit_bytes=VMEM_LIMIT)


def _layer_norm(z, g, b):
    mu = jnp.mean(z, axis=-1, keepdims=True)
    zc = z - mu
    var = jnp.mean(zc * zc, axis=-1, keepdims=True)
    return zc * lax.rsqrt(var + NORM_EPS) * g + b


def _sigmoid(x):
    return 1.0 / (1.0 + jnp.exp(-x))


def _dot(a, b):
    return jnp.dot(a, b, preferred_element_type=F32)


def _dot_nt(a, b):
    return lax.dot_general(a, b, (((1,), (1,)), ((), ())), preferred_element_type=F32)


def _dot_tn(a, b):
    return lax.dot_general(a, b, (((0,), (0,)), ((), ())), preferred_element_type=F32)


def _matmul_kernel(x_ref, w_ref, o_ref):
    o_ref[...] = _dot(x_ref[...].astype(BF16), w_ref[...]).astype(o_ref.dtype)


def _matmul(x, w, out_dtype, tm, tn):
    m, k = x.shape
    n = w.shape[1]
    return pl.pallas_call(
        _matmul_kernel,
        grid=(n // tn, m // tm),
        in_specs=[pl.BlockSpec((tm, k), lambda j, i: (i, 0)),
                  pl.BlockSpec((k, tn), lambda j, i: (0, j))],
        out_specs=pl.BlockSpec((tm, tn), lambda j, i: (i, j)),
        out_shape=jax.ShapeDtypeStruct((m, n), out_dtype),
        compiler_params=_cparams("parallel", "parallel"),
        name="matmul",
    )(x, w)


def _proj_res_ln_kernel(alpha, a_ref, w_ref, h_ref, g_ref, b_ref, o_ref):
    mix = _dot(a_ref[...].astype(BF16), w_ref[...])
    o_ref[...] = _layer_norm(alpha * h_ref[...] + mix, g_ref[...], b_ref[...])


def _proj_res_ln(a, w, h, g, b, alpha, tm=512):
    m, d = h.shape
    k = a.shape[1]
    row = lambda i: (i, 0)
    fix = lambda i: (0, 0)
    return pl.pallas_call(
        functools.partial(_proj_res_ln_kernel, alpha),
        grid=(m // tm,),
        in_specs=[pl.BlockSpec((tm, k), row), pl.BlockSpec((k, d), fix),
                  pl.BlockSpec((tm, d), row), pl.BlockSpec((1, d), fix), pl.BlockSpec((1, d), fix)],
        out_specs=pl.BlockSpec((tm, d), row),
        out_shape=jax.ShapeDtypeStruct((m, d), F32),
        compiler_params=_cparams("parallel"),
        name="proj_res_ln",
    )(a, w, h, g, b)


def _hgrn_in_kernel(x_ref, w_ref, lb_ref, q_ref, k_ref, v_ref, lf_ref, g_ref):
    d = x_ref.shape[1]
    xb = x_ref[...].astype(BF16)
    q = _dot(xb, w_ref[:, 0:d])
    q_ref[...] = q * _sigmoid(q)
    f = _dot(xb, w_ref[:, d:2 * d])
    lb = lb_ref[...]
    fg = lb + (1.0 - lb) * _sigmoid(f)
    lf_ref[...] = jnp.log(jnp.maximum(fg, MIN_FORGET))
    k_ref[...] = 1.0 - fg
    v_ref[...] = _dot(xb, w_ref[:, 2 * d:3 * d])
    g = _dot(xb, w_ref[:, 3 * d:4 * d])
    g_ref[...] = g * _sigmoid(g)


def _hgrn_in(x, w, lb, tm=256):
    m, d = x.shape
    row = lambda i: (i, 0)
    fix = lambda i: (0, 0)
    out = jax.ShapeDtypeStruct((m, d), F32)
    return pl.pallas_call(
        _hgrn_in_kernel,
        grid=(m // tm,),
        in_specs=[pl.BlockSpec((tm, d), row), pl.BlockSpec((d, 4 * d), fix), pl.BlockSpec((1, d), fix)],
        out_specs=[pl.BlockSpec((tm, d), row)] * 5,
        out_shape=[out] * 5,
        compiler_params=_cparams("parallel"),
        name="hgrn_in",
    )(x, w, lb)


def _rows_from(b, picks, span):
    parts = [jnp.broadcast_to(b[p:p + 1, :], (span, b.shape[1])) for p in picks]
    return parts[0] if len(parts) == 1 else jnp.concatenate(parts, axis=0)


def _hgrn_chunk_kernel(q_ref, k_ref, v_ref, lf_ref, g_ref, ng_ref, o_ref, state_ref):
    c = A_CHUNK
    n_sub = q_ref.shape[0] // c

    @pl.when(pl.program_id(1) == 0)
    def _():
        state_ref[...] = jnp.zeros_like(state_ref)

    ti = lax.broadcasted_iota(I32, (c, c), 0)
    si = lax.broadcasted_iota(I32, (c, c), 1)
    row = lax.broadcasted_iota(I32, (c, A_DIM), 0)
    halves = []
    half = c // 2
    while half >= A_SUB:
        halves.append(half)
        half //= 2
    level_masks = [((ti // (2 * hf)) == (si // (2 * hf))) & ((ti % (2 * hf)) >= hf) & ((si % (2 * hf)) < hf)
                   for hf in halves]
    diag_mask = ((ti // A_SUB) == (si // A_SUB)) & (si <= ti)

    def chunk(ci, carry):
        r0 = pl.multiple_of(ci * c, c)
        for hd in range(A_HEADS):
            cols = slice(hd * A_DIM, (hd + 1) * A_DIM)
            q = q_ref[pl.ds(r0, c), cols]
            k = k_ref[pl.ds(r0, c), cols]
            v = v_ref[pl.ds(r0, c), cols]
            b = lf_ref[pl.ds(r0, c), cols]
            sh = 1
            while sh < c:
                b = b + jnp.where(row >= sh, pltpu.roll(b, sh, axis=0), 0.0)
                sh *= 2
            scores = jnp.zeros((c, c), F32)
            for hf, mask in zip(halves, level_masks):
                ref = _rows_from(b, [blk * 2 * hf + hf - 1 for blk in range(c // (2 * hf))], 2 * hf)
                qt = (q * jnp.exp(jnp.minimum(b - ref, 0.0))).astype(BF16)
                kt = (k * jnp.exp(jnp.minimum(ref - b, 0.0))).astype(BF16)
                scores = scores + jnp.where(mask, _dot_nt(qt, kt), 0.0)
            lo = _rows_from(b, [blk * A_SUB for blk in range(c // A_SUB)], A_SUB)
            hi = _rows_from(b, [blk * A_SUB + A_SUB - 1 for blk in range(c // A_SUB)], A_SUB)
            mid = 0.5 * (lo + hi)
            qt = (q * jnp.exp(b - mid)).astype(BF16)
            kt = (k * jnp.exp(mid - b)).astype(BF16)
            scores = scores + jnp.where(diag_mask, _dot_nt(qt, kt), 0.0)

            st = state_ref[hd]
            b_end = b[c - 1:c, :]
            o = _dot(scores.astype(BF16), v.astype(BF16))
            o = o + _dot_nt((q * jnp.exp(b)).astype(BF16), st.astype(BF16))
            ke = (k * jnp.exp(b_end - b)).astype(BF16)
            state_ref[hd] = st * jnp.exp(b_end) + _dot_tn(v.astype(BF16), ke)

            o = o * lax.rsqrt(jnp.mean(o * o, axis=-1, keepdims=True) + NORM_EPS) * ng_ref[:, cols]
            o_ref[pl.ds(r0, c), cols] = (o * g_ref[pl.ds(r0, c), cols]).astype(o_ref.dtype)
        return carry

    lax.fori_loop(0, n_sub, chunk, 0)


def _hgrn_chunk(q, k, v, lf, g, ng, bsz, seq, tc=256):
    d = q.shape[1]
    n_t = seq // tc
    row = lambda b, i: (b * n_t + i, 0)
    return pl.pallas_call(
        _hgrn_chunk_kernel,
        grid=(bsz, n_t),
        in_specs=[pl.BlockSpec((tc, d), row)] * 5 + [pl.BlockSpec((1, d), lambda b, i: (0, 0))],
        out_specs=pl.BlockSpec((tc, d), row),
        out_shape=jax.ShapeDtypeStruct(q.shape, BF16),
        scratch_shapes=[pltpu.VMEM((A_HEADS, A_DIM, A_DIM), F32)],
        compiler_params=_cparams("parallel", "arbitrary"),
        name="hgrn_chunk",
    )(q, k, v, lf, g, ng)


def _hgrn2_mixer_ln(h, w_in, norm_g, w_out, lb, ln_g, ln_b, alpha, bsz, seq):
    q, k, v, lf, g = _hgrn_in(h, w_in.astype(BF16), lb.reshape(1, -1))
    o = _hgrn_chunk(q, k, v, lf, g, norm_g.reshape(1, -1), bsz, seq)
    return _proj_res_ln(o, w_out.astype(BF16), h, ln_g, ln_b, alpha)


def _rope_table_kernel(pos_ref, cos_ref, sin_ref):
    half = B_HEAD_DIM // 2
    lane = lax.broadcasted_iota(I32, cos_ref.shape, 1)
    j = (lane % half).astype(F32)
    inv_freq = jnp.exp(j * (-math.log(ROPE_THETA) / half))
    ang = pos_ref[...].astype(F32) * inv_freq
    cos_ref[...] = jnp.cos(ang)
    sin_ref[...] = jnp.where(lane < 2 * half, -1.0, 1.0) * jnp.sin(ang)


def _rope_tables(pos_col, tm=1024):
    t = pos_col.shape[0]
    out = jax.ShapeDtypeStruct((t, LANES), F32)
    return pl.pallas_call(
        _rope_table_kernel,
        grid=(t // tm,),
        in_specs=[pl.BlockSpec((tm, 1), lambda i: (i, 0))],
        out_specs=[pl.BlockSpec((tm, LANES), lambda i: (i, 0))] * 2,
        out_shape=[out, out],
        compiler_params=_cparams("parallel"),
        name="rope_tables",
    )(pos_col)


def _dil_attn_kernel(reach, q_ref, kp_ref, kc_ref, vp_ref, vc_ref, cq_ref, sq_ref, cp_ref, sp_ref,
                     o_ref, lse_ref):
    nq = q_ref.shape[0]
    i = pl.program_id(2)
    scale = B_HEAD_DIM ** -0.5
    half_lanes = LANES // 2

    def rope(x, cos, sin):
        return x * cos + pltpu.roll(x, half_lanes, axis=1) * sin

    qi = lax.broadcasted_iota(I32, (nq, 2 * nq), 0)
    ki = lax.broadcasted_iota(I32, (nq, 2 * nq), 1)
    dist = qi + nq - ki
    valid = (dist >= 0) & (dist <= reach) & ((ki >= nq) | (i > 0))
    lane = lax.broadcasted_iota(I32, (1, LANES), 1)
    head_a = ((lane // (B_HEAD_DIM // 2)) % 2) == 0
    v_head_a = lane < B_HEAD_DIM
    cq, sq = cq_ref[...], sq_ref[...]
    ck = jnp.concatenate([cp_ref[...], cq], axis=0)
    sk = jnp.concatenate([sp_ref[...], sq], axis=0)
    for pr in range(B_HEADS // 2):
        cols = slice(pr * LANES, (pr + 1) * LANES)
        q = rope(q_ref[:, cols], cq, sq) * scale
        k = rope(jnp.concatenate([kp_ref[:, cols], kc_ref[:, cols]], axis=0), ck, sk).astype(BF16)
        v = jnp.concatenate([vp_ref[:, cols], vc_ref[:, cols]], axis=0).astype(BF16)
        outs, lses = [], []
        for sel in (head_a, ~head_a):
            s = _dot_nt(jnp.where(sel, q, 0.0).astype(BF16), k)
            s = jnp.where(valid, s, MASK_VALUE)
            m = jnp.max(s, axis=-1, keepdims=True)
            p = jnp.exp(s - m)
            l = jnp.sum(p, axis=-1, keepdims=True)
            outs.append(_dot(p.astype(BF16), v) / l)
            lses.append(m + jnp.log(l))
        o_ref[:, cols] = jnp.where(v_head_a, outs[0], outs[1])
        lse_ref[:, cols] = jnp.where(v_head_a, lses[0], lses[1])


def _residue_perm(tm, dilation):
    per = tm // dilation
    j = jnp.arange(tm)
    src = (j % per) * dilation + j // per
    return (src[:, None] == jnp.arange(tm)[None, :]).astype(BF16)


def _regroup_rows(p, x):
    hi = x.astype(BF16)
    rest = x - hi.astype(F32)
    mid = rest.astype(BF16)
    lo = (rest - mid.astype(F32)).astype(BF16)
    return _dot(p, hi) + _dot(p, mid) + _dot(p, lo)


def _dil_proj_kernel(x_ref, p_ref, w_ref, o_ref):
    xs = _dot(p_ref[...], x_ref[...].astype(BF16)).astype(BF16)
    o_ref[...] = _dot(xs, w_ref[...]).reshape(o_ref.shape)


def _token_proj_kernel(x_ref, w_ref, o_ref):
    o_ref[...] = _dot(x_ref[...].astype(BF16), w_ref[...]).reshape(o_ref.shape)


def _dil_proj(h, w, dilation, bsz, seq, tm=512):
    d, n = w.shape
    n_t = seq // tm
    regroup = dilation > 1
    x_spec, w_spec = pl.BlockSpec((tm, d), lambda i: (i, 0)), pl.BlockSpec((d, n), lambda i: (0, 0))
    p_spec = pl.BlockSpec((tm, tm), lambda i: (0, 0))
    return pl.pallas_call(
        _dil_proj_kernel if regroup else _token_proj_kernel,
        grid=(bsz * n_t,),
        in_specs=[x_spec, p_spec, w_spec] if regroup else [x_spec, w_spec],
        out_specs=pl.BlockSpec((None, dilation, tm // dilation, n), lambda i: (i // n_t, 0, i % n_t, 0)),
        out_shape=jax.ShapeDtypeStruct((bsz, dilation, seq // dilation, n), F32),
        compiler_params=_cparams("parallel"),
        name=f"dil_proj_{dilation}",
    )(*((h, _residue_perm(tm, dilation), w) if regroup else (h, w)))


def _dil_attention(proj, cos, sin, dilation, reach, bsz, seq):
    d = D_MODEL
    sub = seq // dilation
    nb = sub // B_BLOCK
    blk = (None, None, B_BLOCK, d)
    tab = (None, None, B_BLOCK, LANES)
    prev = lambda i: jnp.maximum(i - 1, 0)
    in_specs = [
        pl.BlockSpec(blk, lambda b, r, i: (b, r, i, 0)),
        pl.BlockSpec(blk, lambda b, r, i: (b, r, prev(i), 1)),
        pl.BlockSpec(blk, lambda b, r, i: (b, r, i, 1)),
        pl.BlockSpec(blk, lambda b, r, i: (b, r, prev(i), 2)),
        pl.BlockSpec(blk, lambda b, r, i: (b, r, i, 2)),
        pl.BlockSpec(tab, lambda b, r, i: (b, r, i, 0)),
        pl.BlockSpec(tab, lambda b, r, i: (b, r, i, 0)),
        pl.BlockSpec(tab, lambda b, r, i: (b, r, prev(i), 0)),
        pl.BlockSpec(tab, lambda b, r, i: (b, r, prev(i), 0)),
    ]
    out_spec = pl.BlockSpec(blk, lambda b, r, i: (b, r, i, 0))
    out = jax.ShapeDtypeStruct((bsz, dilation, sub, d), F32)
    return pl.pallas_call(
        functools.partial(_dil_attn_kernel, reach),
        grid=(bsz, dilation, nb),
        in_specs=in_specs,
        out_specs=[out_spec, out_spec],
        out_shape=[out, out],
        compiler_params=_cparams("parallel", "parallel", "arbitrary"),
        name=f"dil_attn_{dilation}",
    )(proj, proj, proj, proj, proj, cos, sin, cos, sin)


def _dil_out_kernel(alpha, o0, o1, o2, l0, l1, l2, p1_ref, p2_ref, w_ref, h_ref, g_ref, b_ref, out_ref):
    tm, d = h_ref.shape
    o1v = _regroup_rows(p1_ref[...], o1[...].reshape(tm, d))
    l1v = _regroup_rows(p1_ref[...], l1[...].reshape(tm, d))
    o2v = _regroup_rows(p2_ref[...], o2[...].reshape(tm, d))
    l2v = _regroup_rows(p2_ref[...], l2[...].reshape(tm, d))
    l0v = l0[...]
    m = jnp.maximum(jnp.maximum(l0v, l1v), l2v)
    e0, e1, e2 = jnp.exp(l0v - m), jnp.exp(l1v - m), jnp.exp(l2v - m)
    o = (e0 * o0[...] + e1 * o1v + e2 * o2v) / (e0 + e1 + e2)
    mix = _dot(o.astype(BF16), w_ref[...])
    out_ref[...] = _layer_norm(alpha * h_ref[...] + mix, g_ref[...], b_ref[...])


def _dil_out(outs, lses, w, h, g, b, alpha, bsz, seq, tm=256):
    m, d = h.shape
    n_t = seq // tm
    row = lambda i: (i, 0)
    fix = lambda i: (0, 0)

    def grouped(a):
        dil = a.shape[1]
        return pl.BlockSpec((None, dil, tm // dil, d), lambda i: (i // n_t, 0, i % n_t, 0))

    perms = [_residue_perm(tm, a.shape[1]).T for a in outs[1:]]
    return pl.pallas_call(
        functools.partial(_dil_out_kernel, alpha),
        grid=(m // tm,),
        in_specs=[pl.BlockSpec((tm, d), row), grouped(outs[1]), grouped(outs[2]),
                  pl.BlockSpec((tm, d), row), grouped(lses[1]), grouped(lses[2]),
                  pl.BlockSpec((tm, tm), fix), pl.BlockSpec((tm, tm), fix),
                  pl.BlockSpec((d, d), fix), pl.BlockSpec((tm, d), row),
                  pl.BlockSpec((1, d), fix), pl.BlockSpec((1, d), fix)],
        out_specs=pl.BlockSpec((tm, d), row),
        out_shape=jax.ShapeDtypeStruct((m, d), F32),
        compiler_params=_cparams("parallel"),
        name="dil_out",
    )(outs[0].reshape(m, d), outs[1], outs[2], lses[0].reshape(m, d), lses[1], lses[2], *perms, w, h, g, b)


def _rope_col_perm():
    half = B_HEAD_DIM // 2
    idx = []
    for pr in range(B_HEADS // 2):
        a, b = 2 * pr * B_HEAD_DIM, (2 * pr + 1) * B_HEAD_DIM
        idx += list(range(a, a + half)) + list(range(b, b + half))
        idx += list(range(a + half, a + 2 * half)) + list(range(b + half, b + 2 * half))
    return jnp.asarray(idx, dtype=I32)


def _dilated_mixer_ln(h, positions, w_in, w_out, ln_g, ln_b, alpha, bsz, seq):
    d = D_MODEL
    perm = _rope_col_perm()
    w = w_in.reshape(d, len(B_GROUPS), 3, d)
    w = jnp.concatenate([w[:, :, 0:2, :][..., perm], w[:, :, 2:3, :]], axis=2)
    w = w.reshape(d, len(B_GROUPS), 3 * d).astype(BF16)
    outs, lses = [], []
    for gi, (window, dilation) in enumerate(B_GROUPS):
        sub = seq // dilation
        proj = _dil_proj(h, w[:, gi, :], dilation, bsz, seq)
        pos =positions.reshape(bsz, sub, dilation).transpose(0, 2, 1).reshape(-1, 1)
        cos, sin = _rope_tables(pos)
        tab = (bsz, dilation, sub, LANES)
        o, l = _dil_attention(proj, cos.reshape(tab), sin.reshape(tab), dilation, window // dilation, bsz, seq)
        outs.append(o)
        lses.append(l)
    return _dil_out(outs, lses, w_out.astype(BF16), h, ln_g, ln_b, alpha, bsz, seq)


def _s5_operators(a_re, a_im, log_dt, b_re, b_im, c_re, c_im):
    hp = lax.Precision.HIGHEST
    L = C_CHUNK
    dt = jnp.exp(log_dt)[:, None]
    lam_re = jnp.exp(a_re * dt) * jnp.cos(a_im * dt)
    lam_im = jnp.exp(a_re * dt) * jnp.sin(a_im * dt)
    den = a_re * a_re + a_im * a_im
    fr = ((lam_re - 1.0) * a_re + lam_im * a_im) / den
    fi = (lam_im * a_re - (lam_re - 1.0) * a_im) / den
    bb_re = fr[..., None] * b_re - fi[..., None] * b_im
    bb_im = fr[..., None] * b_im + fi[..., None] * b_re
    tau = jnp.arange(L + 1, dtype=F32)[:, None, None]
    mag = jnp.exp(a_re * dt * tau)
    pw_re = mag * jnp.cos(a_im * dt * tau)
    pw_im = mag * jnp.sin(a_im * dt * tau)
    cl_re = c_re[None] * pw_re[:, :, None, :] - c_im[None] * pw_im[:, :, None, :]
    cl_im = c_re[None] * pw_im[:, :, None, :] + c_im[None] * pw_re[:, :, None, :]
    kern = (jnp.einsum('tgon,gni->tgoi', cl_re, bb_re, precision=hp)
            - jnp.einsum('tgon,gni->tgoi', cl_im, bb_im, precision=hp)).transpose(0, 1, 3, 2)
    t_idx = jnp.arange(L)
    rev = pw_re[L - 1 - t_idx], pw_im[L - 1 - t_idx]
    p_re = rev[0][..., None] * bb_re[None] - rev[1][..., None] * bb_im[None]
    p_im = rev[0][..., None] * bb_im[None] + rev[1][..., None] * bb_re[None]
    p_op = jnp.concatenate([p_re, p_im], axis=2).transpose(0, 1, 3, 2)
    q_op = jnp.concatenate([cl_re[1:], -cl_im[1:]], axis=3).transpose(0, 1, 3, 2)
    lam_l = jnp.concatenate([pw_re[L], pw_im[L]], axis=-1)
    return kern, p_op, q_op, lam_l


def _s5_tile_operators(kern, p_op, q_op, lam_l, n_chunks):
    gt = C_TILE_GROUPS
    nt = C_GROUPS // gt
    L, ch, ns = C_CHUNK, C_GROUP_CH, 2 * C_STATE
    r128 = jnp.arange(LANES)[:, None]
    rows_q = jnp.arange(gt * ns)[:, None]
    own_d = (r128 // ch == jnp.arange(LANES)[None, :] // ch).astype(F32)
    own_p = (r128 // ch == jnp.arange(gt * ns)[None, :] // ns).astype(F32)
    own_q = (rows_q // ns == jnp.arange(LANES)[None, :] // ch).astype(F32)
    lag_blk = jnp.tile(kern[:L].reshape(L, nt, LANES, ch), (1, 1, 1, gt)) * own_d
    zero = jnp.zeros_like(lag_blk[0])
    m_t = jnp.concatenate(
        [jnp.concatenate([lag_blk[t - s] if t >= s else zero for t in range(L)], axis=-1) for s in range(L)],
        axis=1)
    p_blk = jnp.tile(p_op.reshape(L, nt, LANES, ns), (1, 1, 1, gt)) * own_p
    p_t = p_blk.transpose(1, 0, 2, 3).reshape(nt, L * LANES, gt * ns)
    q_blk = jnp.tile(q_op.reshape(L, nt, gt * ns, ch), (1, 1, 1, gt)) * own_q
    q_t = jnp.concatenate([q_blk[t] for t in range(L)], axis=-1)
    m_t, p_t, q_t = m_t.astype(BF16), p_t.astype(BF16), q_t.astype(BF16)
    lr, li = lam_l[:, :C_STATE], lam_l[:, C_STATE:]
    las, lbs = [], []
    k = 1
    while k < n_chunks:
        las.append(jnp.concatenate([lr, lr], axis=-1).reshape(nt, gt * ns))
        lbs.append(jnp.concatenate([-li, li], axis=-1).reshape(nt, gt * ns))
        lr, li = lr * lr - li * li, 2.0 * lr * li
        k *= 2
    return m_t, p_t, q_t, jnp.stack(las, axis=1), jnp.stack(lbs, axis=1)


def _s5_local_kernel(u_ref, m_ref, p_ref, y_ref, x_ref):
    n = u_ref.shape[0] // C_CHUNK
    u = jnp.concatenate([u_ref[pl.ds(s, n, stride=C_CHUNK), :].astype(BF16) for s in range(C_CHUNK)], axis=1)
    y = _dot(u, m_ref[...])
    for t in range(C_CHUNK):
        y_ref[pl.ds(t, n, stride=C_CHUNK), :] = y[:, t * LANES:(t + 1) * LANES]
    x_ref[...] = _dot(u, p_ref[...])


def _s5_local(h, m_t, p_t, bsz, seq):
    nt, kf, ns = p_t.shape
    n = seq // C_CHUNK
    return pl.pallas_call(
        _s5_local_kernel,
        grid=(nt, bsz),
        in_specs=[pl.BlockSpec((seq, LANES), lambda j, b: (b, j)),
                  pl.BlockSpec((None, kf, kf), lambda j, b: (j, 0, 0)),
                  pl.BlockSpec((None, kf, ns), lambda j, b: (j, 0, 0))],
        out_specs=[pl.BlockSpec((seq, LANES), lambda j, b: (b, j)),
                   pl.BlockSpec((None, n, ns), lambda j, b: (j, b, 0))],
        out_shape=[jax.ShapeDtypeStruct(h.shape, F32), jax.ShapeDtypeStruct((nt, bsz * n, ns), F32)],
        compiler_params=_cparams("parallel", "parallel"),
        name="s5_local",
    )(h, m_t, p_t)


def _s5_state_kernel(y_ref, x_ref, q_ref, la_ref, lb_ref, o_ref):
    n = x_ref.shape[0]
    row = lax.broadcasted_iota(I32, (n, LANES), 0)
    cols = []
    for g in range(C_TILE_GROUPS):
        lanes = slice(g * LANES, (g + 1) * LANES)
        x = x_ref[:, lanes]
        k, sh = 0, 1
        while sh < n:
            v = jnp.where(row >= sh, pltpu.roll(x, sh, axis=0), 0.0)
            x = x + la_ref[k:k + 1, lanes] * v + lb_ref[k:k + 1, lanes] * pltpu.roll(v, C_STATE, axis=1)
            k += 1
            sh *= 2
        cols.append(jnp.where(row >= 1, pltpu.roll(x, 1, axis=0), 0.0).astype(BF16))
    yq = _dot(jnp.concatenate(cols, axis=1), q_ref[...])
    for t in range(C_CHUNK):
        rows = pl.ds(t, n, stride=C_CHUNK)
        o_ref[rows, :] = y_ref[rows, :] + yq[:, t * LANES:(t + 1) * LANES]


def _s5_state_out(y, x_loc, q_t, la, lb, bsz, seq):
    nt, ns, kf = q_t.shape
    n = seq // C_CHUNK
    nk = la.shape[1]
    return pl.pallas_call(
        _s5_state_kernel,
        grid=(nt, bsz),
        in_specs=[pl.BlockSpec((seq, LANES), lambda j, b: (b, j)),
                  pl.BlockSpec((None, n, ns), lambda j, b: (j, b, 0)),
                  pl.BlockSpec((None, ns, kf), lambda j, b: (j, 0, 0)),
                  pl.BlockSpec((None, nk, ns), lambda j, b: (j, 0, 0)),
                  pl.BlockSpec((None, nk, ns), lambda j, b: (j, 0, 0))],
        out_specs=pl.BlockSpec((seq, LANES), lambda j, b: (b, j)),
        out_shape=jax.ShapeDtypeStruct(y.shape, F32),
        compiler_params=_cparams("parallel", "parallel"),
        name="s5_state_out",
    )(y, x_loc, q_t, la, lb)


def _s5_out_kernel(alpha, y_ref, h_ref, dskip_ref, w_ref, g_ref, b_ref, o_ref):
    d = h_ref.shape[1]
    h = h_ref[...]
    z = y_ref[...] + dskip_ref[...] * h
    z = 0.5 * z * (1.0 + jnp.tanh(math.sqrt(2.0 / math.pi) * (z + 0.044715 * (z * z * z))))
    zb = z.astype(BF16)
    val = _dot(zb, w_ref[:, 0:d])
    gate = _dot(zb, w_ref[:, d:2 * d])
    o_ref[...] = _layer_norm(alpha * h + val * _sigmoid(gate), g_ref[...], b_ref[...])


def _s5_out(y, h, d_skip, w_glu, g, b, alpha, tm=256):
    m, d = h.shape
    row = lambda i: (i, 0)
    fix = lambda i: (0, 0)
    return pl.pallas_call(
        functools.partial(_s5_out_kernel, alpha),
        grid=(m // tm,),
        in_specs=[pl.BlockSpec((tm, d), row), pl.BlockSpec((tm, d), row), pl.BlockSpec((1, d), fix),
                  pl.BlockSpec((d, 2 * d), fix), pl.BlockSpec((1, d), fix), pl.BlockSpec((1, d), fix)],
        out_specs=pl.BlockSpec((tm, d), row),
        out_shape=jax.ShapeDtypeStruct((m, d), F32),
        compiler_params=_cparams("parallel"),
        name="s5_out",
    )(y, h, d_skip, w_glu, g, b)


def _s5_mixer_ln(h, a_re, a_im, log_dt, b_re, b_im, c_re, c_im, d_skip, w_glu, ln_g, ln_b, alpha, bsz, seq):
    ops = _s5_operators(a_re, a_im, log_dt, b_re, b_im, c_re, c_im)
    m_t, p_t, q_t, la, lb = _s5_tile_operators(*ops, seq // C_CHUNK)
    y_loc, x_loc = _s5_local(h, m_t, p_t, bsz, seq)
    y = _s5_state_out(y_loc, x_loc, q_t, la, lb, bsz, seq)
    return _s5_out(y, h, d_skip.reshape(1, -1), w_glu.astype(BF16), ln_g, ln_b, alpha)


def _cross_kernel(alpha, h_ref, kv_ref, wq_ref, wo_ref, g_ref, b_ref, wr_ref, br_ref,
                  o_ref, idx_ref, gate_ref):
    d = h_ref.shape[1]
    h = h_ref[...]
    q = (_dot(h.astype(BF16), wq_ref[...]) * (M_HEAD_DIM ** -0.5)).astype(BF16)
    heads = []
    for hd in range(M_HEADS):
        cols = slice(hd * M_HEAD_DIM, (hd + 1) * M_HEAD_DIM)
        s = _dot_nt(q[:, cols], kv_ref[:, cols])
        p = jnp.exp(s - jnp.max(s, axis=-1, keepdims=True))
        l = jnp.sum(p, axis=-1, keepdims=True)
        heads.append((_dot(p.astype(BF16), kv_ref[:, d + hd * M_HEAD_DIM:d + (hd + 1) * M_HEAD_DIM]) / l)
                     .astype(BF16))
    att = _dot(jnp.concatenate(heads, axis=-1), wo_ref[...])
    h2 = _layer_norm(alpha * h + att, g_ref[...], b_ref[...])
    o_ref[...] = h2
    logits = (_dot(h2.astype(BF16), wr_ref[...]) + br_ref[...]).T[0:N_EXPERTS, :]
    eidx = lax.broadcasted_iota(I32, logits.shape, 0)
    vals, idxs = [], []
    for _ in range(TOP_K):
        m = jnp.max(logits, axis=0, keepdims=True)
        pick = jnp.min(jnp.where(logits == m, eidx, N_EXPERTS), axis=0, keepdims=True)
        vals.append(m)
        idxs.append(pick)
        logits = jnp.where(eidx == pick, -jnp.inf, logits)
    es = [jnp.exp(v - vals[0]) for v in vals]
    tot = es[0] + es[1] + es[2] + es[3]
    idx_ref[...] = jnp.concatenate(idxs, axis=0)
    gate_ref[...] = jnp.concatenate([e / tot for e in es], axis=0)


def _cross_attention_router(h, mem_kv, wq, wo, g, b, wr, br, alpha, bsz, seq, tm=512):
    d = D_MODEL
    n_t = seq // tm
    mlen = mem_kv.shape[1]
    row = lambda bi, i: (bi * n_t + i, 0)
    fix = lambda bi, i: (0, 0)
    col = lambda bi, i: (bi, 0, i)
    return pl.pallas_call(
        functools.partial(_cross_kernel, alpha),
        grid=(bsz, n_t),
        in_specs=[pl.BlockSpec((tm, d), row),
                  pl.BlockSpec((None, mlen, 2 * d), lambda bi, i: (bi, 0, 0)),
                  pl.BlockSpec((d, d), fix), pl.BlockSpec((d, d), fix),
                  pl.BlockSpec((1, d), fix), pl.BlockSpec((1, d), fix),
                  pl.BlockSpec((d, LANES), fix), pl.BlockSpec((1, LANES), fix)],
        out_specs=[pl.BlockSpec((tm, d), row),
                   pl.BlockSpec((None, TOP_K, tm), col), pl.BlockSpec((None, TOP_K, tm), col)],
        out_shape=[jax.ShapeDtypeStruct((bsz * seq, d), F32),
                   jax.ShapeDtypeStruct((bsz, TOP_K, seq), I32),
                   jax.ShapeDtypeStruct((bsz, TOP_K, seq), F32)],
        compiler_params=_cparams("parallel", "parallel"),
        name="cross_attn_router",
    )(h, mem_kv, wq, wo, g, b, wr, br)


def _slot_kernel(idx_ref, slot_ref, count_ref, run_ref, base_ref):
    phase = pl.program_id(0)
    first = (pl.program_id(1) == 0) & (pl.program_id(2) == 0)

    @pl.when(first & (phase == 0))
    def _():
        run_ref[...] = jnp.zeros_like(run_ref)

    @pl.when(first & (phase == 1))
    def _():
        counts = run_ref[...]
        count_ref[...] = counts.astype(I32)
        padded = jnp.ceil(counts * (1.0 / MOE_ROWS)) * MOE_ROWS
        row = lax.broadcasted_iota(I32, padded.shape, 0)
        ends = padded
        sh = 1
        while sh < N_EXPERTS:
            ends = ends + jnp.where(row >= sh, pltpu.roll(ends, sh, axis=0), 0.0)
            sh *= 2
        base_ref[...] = ends - padded
        run_ref[...] = jnp.zeros_like(run_ref)

    tn = idx_ref.shape[1]
    idx = idx_ref[...]
    eidx = lax.broadcasted_iota(I32, (N_EXPERTS, tn), 0)
    hits = [eidx == idx[k:k + 1, :] for k in range(TOP_K)]
    onehot = jnp.zeros((N_EXPERTS, tn), F32)
    for hit in hits:
        onehot = onehot + jnp.where(hit, 1.0, 0.0)

    @pl.when(phase == 1)
    def _():
        earlier = lax.broadcasted_iota(I32, (tn, tn), 0) < lax.broadcasted_iota(I32, (tn, tn), 1)
        before = _dot(onehot.astype(BF16), jnp.where(earlier, 1.0, 0.0).astype(BF16))
        before = before + run_ref[:, 0:1] + base_ref[:, 0:1]
        slots = [jnp.sum(jnp.where(hit, before, 0.0), axis=0, keepdims=True) for hit in hits]
        slot_ref[...] = jnp.concatenate(slots, axis=0).astype(I32)

    run_ref[...] = run_ref[...] + jnp.sum(onehot, axis=1, keepdims=True)


def _expert_slots(idx, tn=512):
    bsz, _, seq = idx.shape
    return pl.pallas_call(
        _slot_kernel,
        grid=(2, bsz, seq // tn),
        in_specs=[pl.BlockSpec((None, TOP_K, tn), lambda p, bi, i: (bi, 0, i))],
        out_specs=[pl.BlockSpec((None, TOP_K, tn), lambda p, bi, i: (bi * p, 0, i * p)),
                   pl.BlockSpec((N_EXPERTS, LANES), lambda p, bi, i: (0, 0))],
        out_shape=[jax.ShapeDtypeStruct(idx.shape, I32), jax.ShapeDtypeStruct((N_EXPERTS, LANES), I32)],
        scratch_shapes=[pltpu.VMEM((N_EXPERTS, LANES), F32), pltpu.VMEM((N_EXPERTS, LANES), F32)],
        compiler_params=_cparams("arbitrary", "arbitrary", "arbitrary"),
        name="expert_slots",
    )(idx)


def _dispatch_kernel(tail_ref, slot_ref, x_ref, xs_ref, zero_ref, sem, zsem):
    tm = x_ref.shape[0] * SUBLANES

    @pl.when(pl.program_id(0) == 0)
    def _():
        zero_ref[...] = jnp.zeros_like(zero_ref)

        def clear(e):
            row = pl.multiple_of(jnp.maximum(tail_ref[e], 0), MOE_ROWS)
            return pltpu.make_async_copy(zero_ref, xs_ref.at[pl.ds(row, MOE_ROWS)], zsem)

        for e in range(tail_ref.shape[0]):
            @pl.when(tail_ref[e] >= 0)
            def _(e=e):
                clear(e).start()
        for e in range(tail_ref.shape[0]):
            @pl.when(tail_ref[e] >= 0)
            def _(e=e):
                clear(e).wait()

    for k in range(TOP_K):
        def start(g, c, k=k):
            for u in range(SUBLANES):
                pltpu.make_async_copy(x_ref.at[g, pl.ds(u, 1)],
                                      xs_ref.at[pl.ds(slot_ref[0, k * tm + g * SUBLANES + u], 1)], sem).start()
            return c

        lax.fori_loop(0, tm // SUBLANES, start, 0, unroll=DMA_GROUP_UNROLL)
    for k in range(TOP_K):
        pltpu.make_async_copy(xs_ref.at[pl.ds(0, tm)], xs_ref.at[pl.ds(0, tm)], sem).wait()


def _dispatch(x, slots, tail_rows, n_rows, tm=512):
    t, d = x.shape
    grid_spec = pltpu.PrefetchScalarGridSpec(
        num_scalar_prefetch=1,
        grid=(t // tm,),
        in_specs=[pl.BlockSpec((None, 1, TOP_K * tm), lambda i, tail: (i, 0, 0), memory_space=pltpu.SMEM),
                  pl.BlockSpec((tm // SUBLANES, SUBLANES, d), lambda i, tail: (i, 0, 0))],
        out_specs=pl.BlockSpec(memory_space=pl.ANY),
        scratch_shapes=[pltpu.VMEM((MOE_ROWS, d), x.dtype), pltpu.SemaphoreType.DMA(()),
                        pltpu.SemaphoreType.DMA(())],
    )
    return pl.pallas_call(
        _dispatch_kernel,
        grid_spec=grid_spec,
        out_shape=jax.ShapeDtypeStruct((n_rows, d), x.dtype),
        compiler_params=_cparams("arbitrary"),
        name="moe_dispatch",
    )(tail_rows, slots, x.reshape(t // SUBLANES, SUBLANES, d))


def _expert_ffn_kernel(be_ref, nu_ref, x_ref, w1_ref, b1_ref, w2_ref, b2_ref, y_ref, w1b_ref, w2b_ref):
    d = x_ref.shape[1]
    i = pl.program_id(0)

    @pl.when((i == 0) | (be_ref[i] != be_ref[jnp.maximum(i - 1, 0)]))
    def _():
        w1b_ref[...] = w1_ref[...].astype(BF16)
        w2b_ref[...] = w2_ref[...].astype(BF16)

    @pl.when(i < nu_ref[0])
    def _():
        xb = x_ref[...].astype(BF16)
        glu = jnp.minimum(_dot(xb, w1b_ref[:, 0:d]) + b1_ref[:, 0:d], SWIGLU_LIMIT)
        lin = jnp.clip(_dot(xb, w1b_ref[:, d:2 * d]) + b1_ref[:, d:2 * d], -SWIGLU_LIMIT, SWIGLU_LIMIT)
        act = glu * _sigmoid(SWIGLU_ALPHA * glu) * (lin + 1.0)
        y_ref[...] = _dot(act.astype(BF16), w2b_ref[...]) + b2_ref[...]

    @pl.when(i >= nu_ref[0])
    def _():
        y_ref[...] = jnp.zeros_like(y_ref)


def _expert_ffn(xs, block_e, n_used, w1_all, b1, w2_all, b2, layer):
    n_rows, d = xs.shape
    nb = n_rows // MOE_ROWS
    wsel = lambda i, be, nu: (be[i], 0, 0)
    wsel_l = lambda i, be, nu: (layer, be[i], 0, 0)
    xsel = lambda i, be, nu: (jnp.minimum(i, nu[0] - 1), 0)
    grid_spec = pltpu.PrefetchScalarGridSpec(
        num_scalar_prefetch=2,
        grid=(nb,),
        in_specs=[pl.BlockSpec((MOE_ROWS, d), xsel),
                  pl.BlockSpec((None, None, d, 2 * d), wsel_l), pl.BlockSpec((None, 1, 2 * d), wsel),
                  pl.BlockSpec((None, None, d, d), wsel_l), pl.BlockSpec((None, 1, d), wsel)],
        out_specs=pl.BlockSpec((MOE_ROWS, d), lambda i, be, nu: (i, 0)),
        scratch_shapes=[pltpu.VMEM((d, 2 * d), BF16), pltpu.VMEM((d, d), BF16)],
    )
    return pl.pallas_call(
        _expert_ffn_kernel,
        grid_spec=grid_spec,
        out_shape=jax.ShapeDtypeStruct((n_rows, d), F32),
        compiler_params=_cparams("arbitrary"),
        name="expert_ffn",
    )(block_e, n_used, xs, w1_all, b1, w2_all, b2)


def _combine_kernel(alpha, slot_ref, nslot_ref, gate_ref, h_ref, g_ref, b_ref, ys_ref, o_ref, buf, sem):
    tm = h_ref.shape[0]
    i = pl.program_id(0)

    def gather(s_ref, half):
        for k in range(TOP_K):
            def start(g, c, k=k):
                for u in range(SUBLANES):
                    pltpu.make_async_copy(ys_ref.at[pl.ds(s_ref[0, k * tm + g * SUBLANES + u], 1)],
                                          buf.at[half, k, g, pl.ds(u, 1)], sem.at[half]).start()
                return c

            lax.fori_loop(0, tm // SUBLANES, start, 0, unroll=DMA_GROUP_UNROLL)

    @pl.when(i == 0)
    def _():
        gather(slot_ref, 0)

    @pl.when(i + 1 < pl.num_programs(0))
    def _():
        gather(nslot_ref, (i + 1) % 2)

    cur = i % 2
    for k in range(TOP_K):
        pltpu.make_async_copy(ys_ref.at[pl.ds(0, tm)], ys_ref.at[pl.ds(0, tm)], sem.at[cur]).wait()
    gate = gate_ref[...]
    rows = lambda k: buf[cur, k].reshape(h_ref.shape)
    y = rows(0) * gate[:, 0:1]
    for k in range(1, TOP_K):
        y = y + rows(k) * gate[:, k:k + 1]
    o_ref[...] = _layer_norm(alpha * h_ref[...] + y, g_ref[...], b_ref[...])


def _combine_ln(ys, slots, gates, h, g, b, alpha, tm=256):
    t, d = h.shape
    n_t = t // tm
    row = lambda i: (i, 0)
    fix = lambda i: (0, 0)
    slot_block = (None, 1, TOP_K * tm)
    return pl.pallas_call(
        functools.partial(_combine_kernel, alpha),
        grid=(n_t,),
        in_specs=[pl.BlockSpec(slot_block, lambda i: (i, 0, 0), memory_space=pltpu.SMEM),
                  pl.BlockSpec(slot_block, lambda i: (jnp.minimum(i + 1, n_t - 1), 0, 0),
                               memory_space=pltpu.SMEM),
                  pl.BlockSpec((tm, TOP_K), row), pl.BlockSpec((tm, d), row),
                  pl.BlockSpec((1, d), fix), pl.BlockSpec((1, d), fix),
                  pl.BlockSpec(memory_space=pl.ANY)],
        out_specs=pl.BlockSpec((tm, d), row),
        out_shape=jax.ShapeDtypeStruct((t, d), F32),
        scratch_shapes=[pltpu.VMEM((2, TOP_K, tm // SUBLANES, SUBLANES, d), F32), pltpu.SemaphoreType.DMA((2,))],
        compiler_params=_cparams("arbitrary"),
        name="moe_combine_ln",
    )(slots, slots, gates, h, g, b, ys)


def _tile_slots(slot, tm):
    bsz, _, seq = slot.shape
    s = slot.reshape(bsz, TOP_K, seq // tm, tm).transpose(0, 2, 1, 3)
    return s.reshape(bsz * (seq // tm), 1, TOP_K * tm)


def _moe_ln(h2, idx, gates, w1_all, b1, w2_all, b2, layer, ln_g, ln_b, alpha, tm_d=512, tm_c=256):
    t, d = h2.shape
    slot, counts = _expert_slots(idx)
    blocks_per_e = (counts[:, 0] + MOE_ROWS - 1) // MOE_ROWS
    block_end = jnp.cumsum(blocks_per_e)
    nb = t * TOP_K // MOE_ROWS + N_EXPERTS
    block_e = jnp.sum((block_end[None, :] <= jnp.arange(nb, dtype=I32)[:, None]).astype(I32), axis=1)
    block_e = jnp.minimum(block_e, N_EXPERTS - 1)
    n_used = block_end[-1:].astype(I32)
    tail_rows = jnp.where(blocks_per_e > 0, (block_end - 1) * MOE_ROWS, -1)
    spare = n_used[0] + jnp.arange(N_EXPERTS, dtype=I32)
    tail_rows = jnp.concatenate([tail_rows, jnp.where(spare < nb, spare * MOE_ROWS, -1)]).astype(I32)
    xs = _dispatch(h2, _tile_slots(slot, tm_d), tail_rows, nb * MOE_ROWS, tm_d)
    ys = _expert_ffn(xs, block_e, n_used, w1_all, b1[:, None, :], w2_all, b2[:, None, :], layer)
    gates_t = gates.transpose(0, 2, 1).reshape(t, TOP_K)
    return _combine_ln(ys, _tile_slots(slot, tm_c), gates_t, h2, ln_g[None], ln_b[None], alpha, tm_c)


def kernel(x, mem, positions, ln_g, ln_b, a_w_in, a_lower_bounds, a_norm_g, a_w_out, b_w_in, b_w_out,
           c_a_re, c_a_im, c_log_dt, c_b_re, c_b_im, c_c_re, c_c_im, c_d, c_w_glu, m_w_kv, m_w_q, m_w_o,
           r_w, r_b, e_w1, e_b1, e_w2, e_b2):
    bsz, seq, d = x.shape
    depth = ln_g.shape[0]
    alpha = (2 * depth) ** 0.25
    t = bsz * seq
    lb = jax.nn.softmax(a_lower_bounds.astype(F32), axis=0)
    lb = jnp.cumsum(lb, axis=0) - lb[0]
    mem_kv = _matmul(mem.reshape(-1, d), m_w_kv.astype(BF16), BF16, 512, 1024)
    mem_kv = mem_kv.reshape(bsz, mem.shape[1], 2 * d)
    h = x.reshape(t, d)
    for layer in range(depth):
        kind, j = layer % N_MIXERS, layer // N_MIXERS
        g1, b1 = ln_g[layer, 0][None], ln_b[layer, 0][None]
        if kind == 0:
            h = _hgrn2_mixer_ln(h, a_w_in[j], a_norm_g[j], a_w_out[j], lb[layer], g1, b1, alpha, bsz, seq)
        elif kind == 1:
            h = _dilated_mixer_ln(h, positions, b_w_in[j], b_w_out[j], g1, b1, alpha, bsz, seq)
        else:
            h = _s5_mixer_ln(h, c_a_re[j], c_a_im[j], c_log_dt[j], c_b_re[j], c_b_im[j], c_c_re[j],
                             c_c_im[j], c_d[j], c_w_glu[j], g1, b1, alpha, bsz, seq)
        wr = jnp.pad(r_w[layer], ((0, 0), (0, LANES - N_EXPERTS))).astype(BF16)
        br = jnp.pad(r_b[layer], (0, LANES - N_EXPERTS))[None]
        h, idx, gates = _cross_attention_router(h, mem_kv, m_w_q[layer].astype(BF16), m_w_o[layer].astype(BF16),
                                                ln_g[layer, 1][None], ln_b[layer, 1][None], wr, br,
                                                alpha, bsz, seq)
        h = _moe_ln(h, idx, gates, e_w1, e_b1[layer], e_w2, e_b2[layer], layer,
                    ln_g[layer, 2], ln_b[layer, 2], alpha)
    return h.reshape(bsz, seq, d)
```
